```python
import jax, jax.numpy as jnp
from jax import lax
import numpy as np

D_MODEL = 2048
BATCH = 4
SEQ = 2048
DEPTH = 2
DEC_BATCH = 128
DEC_SEQ = 4
PAST_LEN = 16384
PAGE_SIZE = 128

N_BRANCH = 4
BRANCH_W = D_MODEL // N_BRANCH
A_W = BRANCH_W
A_GROUPS = 4
A_GW = A_W // A_GROUPS
A_CHUNK = 128
B_W = BRANCH_W
B_HEADS = 4
B_DK = B_W // B_HEADS
B_DV = B_W // B_HEADS
LB_FLOOR = 1e-30
C_W = BRANCH_W
C_HEADDIM = 64
C_HEADS = C_W // C_HEADDIM
C_GROUPS = 2
C_DSTATE = 128
C_CONV = 4
C_CONV_DIM = C_W + 2 * C_GROUPS * C_DSTATE
D_W = BRANCH_W
D_HEADS = 4
D_DK = D_W // D_HEADS
D_DV = D_W // D_HEADS
NEG_BIG = -1e30
IN_SPLITS = (A_W, A_W, B_W, B_W, B_W, B_W, C_W, C_CONV_DIM, C_HEADS, D_W, D_W, D_W, D_W, D_HEADS, D_HEADS, N_BRANCH * D_MODEL)
IN_COLS = 2 * A_W + 4 * B_W + C_W + C_CONV_DIM + C_HEADS + 4 * D_W + 2 * D_HEADS + N_BRANCH * D_MODEL
PEER_NKEYS = 128
PEER_N = PEER_NKEYS * PEER_NKEYS
PEER_HEADS = 8
PEER_DKEY = 256
PEER_TOPK = 16
PEER_BLOCK = 128
SCAN_CHUNK = 64
EPS = 1e-6

kernel_name = "hybrid_gmlp_hgrn2_ssd_mlstm_peer_step"


def rmsnorm(x, w):
    xf = x.astype(jnp.float32)
    y = xf * lax.rsqrt(jnp.mean(xf * xf, axis=-1, keepdims=True) + EPS)
    return (y * w.astype(jnp.float32)).astype(x.dtype)


def layernorm(x, w, b):
    xf = x.astype(jnp.float32)
    xc = xf - jnp.mean(xf, axis=-1, keepdims=True)
    var = jnp.mean(xc * xc, axis=-1, keepdims=True)
    return (xc * lax.rsqrt(var + EPS) * w + b).astype(x.dtype)


def split_cols(h):
    idx = [int(i) for i in np.cumsum(np.array(IN_SPLITS))[:-1]]
    return jnp.split(h, idx, axis=-1)


def causal_conv(xbc, buf, w, b):
    T = xbc.shape[1]
    full = jnp.concatenate([buf.astype(xbc.dtype), xbc], axis=1)
    out = b
    for j in range(C_CONV):
        out = out + full[:, j:j + T] * w[j]
    return out, full[:, T:]


def _chunk_len(T):
    return SCAN_CHUNK if T % SCAN_CHUNK == 0 else T


def _to_chunks(a, n, c):
    return jnp.moveaxis(a.reshape((a.shape[0], n, c) + a.shape[2:]), 1, 0)


def _from_chunks(a):
    n, B, c = a.shape[:3]
    return jnp.moveaxis(a, 0, 1).reshape((B, n * c) + a.shape[3:])


def gla_scan(q, k, v, logf, s0):
    T = q.shape[1]
    c = _chunk_len(T)
    n = T // c
    mask = jnp.tril(jnp.ones((c, c), dtype=bool))[None, :, :, None, None]
    scalar = logf.shape[-1] == 1

    def step(S, inp):
        qc, kc, vc, gc = inp
        G = jnp.cumsum(gc, axis=1)
        diff = G[:, :, None] - G[:, None, :]
        dec = jnp.where(mask, jnp.exp(jnp.where(mask, diff, 0.0)), 0.0)
        if scalar:
            scores = jnp.einsum('bthk,bshk->btsh', qc, kc) * dec[..., 0]
        else:
            scores = jnp.einsum('bthk,bshk,btshk->btsh', qc, kc, dec)
        o = jnp.einsum('btsh,bshv->bthv', scores, vc) + jnp.einsum('bthk,bhkv->bthv', qc * jnp.exp(G), S)
        g_last = G[:, -1]
        k_dec = kc * jnp.exp(g_last[:, None] - G)
        S_new = jnp.exp(g_last)[..., None] * S + jnp.einsum('bshk,bshv->bhkv', k_dec, vc)
        return S_new, o

    xs = tuple(_to_chunks(a.astype(jnp.float32), n, c) for a in (q, k, v, logf))
    S_fin, o = lax.scan(step, s0.astype(jnp.float32), xs)
    return _from_chunks(o), S_fin


def mlstm_scan(q, k, v, ig, lf, c0, n0, m0):
    T = q.shape[1]
    c = _chunk_len(T)
    n = T // c
    mask = jnp.tril(jnp.ones((c, c), dtype=bool))[None, :, :, None]

    def step(carry, inp):
        Cs, ns, ms = carry
        qc, kc, vc, ic, fc = inp
        b = jnp.cumsum(fc, axis=1)
        dmat = b[:, :, None] - b[:, None, :] + ic[:, None, :]
        dmat = jnp.where(mask, dmat, NEG_BIG)
        inter = b + ms[:, None]
        mt = jnp.maximum(inter, jnp.max(dmat, axis=2))
        w_intra = jnp.where(mask, jnp.exp(dmat - mt[:, :, None]), 0.0)
        w_inter = jnp.exp(inter - mt)
        qk = jnp.einsum('bthk,bshk->btsh', qc, kc) * w_intra
        num = w_inter[..., None] * jnp.einsum('bthk,bhkv->bthv', qc, Cs) + jnp.einsum('btsh,bshv->bthv', qk, vc)
        den = w_inter * jnp.einsum('bthk,bhk->bth', qc, ns) + jnp.sum(qk, axis=2)
        h = num / jnp.maximum(jnp.abs(den), jnp.exp(-mt))[..., None]
        m_new = mt[:, -1]
        w_k = jnp.exp(b[:, -1:] - b + ic - m_new[:, None])
        decay = jnp.exp(b[:, -1] + ms - m_new)
        C_new = decay[..., None, None] * Cs + jnp.einsum('bsh,bshk,bshv->bhkv', w_k, kc, vc)
        n_new = decay[..., None] * ns + jnp.einsum('bsh,bshk->bhk', w_k, kc)
        return (C_new, n_new, m_new), h

    xs = tuple(_to_chunks(a.astype(jnp.float32), n, c) for a in (q, k, v, ig, lf))
    init = (c0.astype(jnp.float32), n0.astype(jnp.float32), m0.astype(jnp.float32))
    (C_fin, n_fin, m_fin), h = lax.scan(step, init, xs)
    return _from_chunks(h), C_fin, n_fin, m_fin


def mixer_layer(xn, lb, p, l, s_hgrn, s_ssm, s_conv, s_c, s_n, s_m):
    Bsz, T, _ = xn.shape
    f32 = jnp.float32
    h = jnp.einsum('btd,dc->btc', xn, p['w_in'][l])
    (a_u, a_v, b_q, b_f, b_i, b_g, c_z, c_xbc, c_dt, d_q, d_k, d_v, d_o, d_ig, d_fg, gates) = split_cols(h)

    a_u = jax.nn.gelu(a_u, approximate=False)
    a_v = layernorm(jax.nn.gelu(a_v, approximate=False), p['a_ln_w'][l], p['a_ln_b'][l])
    c_len = A_CHUNK if T % A_CHUNK == 0 else T
    n_ch = T // c_len
    w_s = jnp.tril(p['a_ws'][l][:, :c_len, :c_len])
    b_s = p['a_bs'][l][:, :c_len]
    vr = a_v.reshape(Bsz, n_ch, c_len, A_GROUPS, A_GW)
    sp = jnp.einsum('gts,bnsgc->bntgc', w_s, vr) + b_s.T[None, None, :, :, None]
    out_a = a_u * sp.reshape(Bsz, T, A_W)

    qb = jax.nn.silu(b_q).reshape(Bsz, T, B_HEADS, B_DK)
    fz = b_f.astype(f32).reshape(Bsz, T, B_HEADS, B_DK)
    lbh = lb.reshape(B_HEADS, B_DK)
    logf_b = jnp.logaddexp(jnp.log(jnp.maximum(lbh, LB_FLOOR)), jnp.log1p(-lbh) + jax.nn.log_sigmoid(fz))
    kb = (1.0 - lbh) * jax.nn.sigmoid(-fz)
    ob, s_hgrn_new = gla_scan(qb, kb, b_i.reshape(Bsz, T, B_HEADS, B_DV), logf_b, s_hgrn)
    ob = rmsnorm(ob, p['b_norm_w'][l]) * jax.nn.silu(b_g.reshape(Bsz, T, B_HEADS, B_DV))
    out_b = ob.reshape(Bsz, T, B_W)

    xbc, conv_new = causal_conv(c_xbc, s_conv, p['c_conv_w'][l], p['c_conv_b'][l])
    xbc = jax.nn.silu(xbc)
    xs, bm, cm = jnp.split(xbc, [C_W, C_W + C_GROUPS * C_DSTATE], axis=-1)
    xs = xs.reshape(Bsz, T, C_HEADS, C_HEADDIM)
    rep = C_HEADS // C_GROUPS
    bm = jnp.repeat(bm.reshape(Bsz, T, C_GROUPS, C_DSTATE), rep, axis=2)
    cm = jnp.repeat(cm.reshape(Bsz, T, C_GROUPS, C_DSTATE), rep, axis=2)
    dt = jax.nn.softplus(c_dt.astype(f32) + p['c_dt_bias'][l])
    a_neg = -jnp.exp(p['c_a_log'][l])
    yc, s_ssm_new = gla_scan(cm, bm * dt[..., None], xs, (dt * a_neg)[..., None], s_ssm)
    yc = yc + p['c_d'][l][:, None] * xs
    yc = yc.reshape(Bsz, T, C_W) * jax.nn.silu(c_z)
    yc = rmsnorm(yc.reshape(Bsz, T, C_GROUPS, C_W // C_GROUPS), p['c_norm_w'][l].reshape(C_GROUPS, C_W // C_GROUPS))
    out_c = yc.reshape(Bsz, T, C_W)

    qd = d_q.reshape(Bsz, T, D_HEADS, D_DK) * (D_DK ** -0.5)
    kd = d_k.reshape(Bsz, T, D_HEADS, D_DK)
    vd = d_v.reshape(Bsz, T, D_HEADS, D_DV)
    igd = d_ig.astype(f32) + p['d_ig_b'][l]
    lfd = jax.nn.log_sigmoid(d_fg.astype(f32) + p['d_fg_b'][l])
    hd, c_new, n_new, m_new = mlstm_scan(qd, kd, vd, igd, lfd, s_c, s_n, s_m)
    hd = rmsnorm(hd, p['d_norm_w'][l].reshape(D_HEADS, D_DV))
    out_d = jax.nn.sigmoid(d_o) * hd.reshape(Bsz, T, D_W)

    br = jnp.stack([out_a, out_b.astype(xn.dtype), out_c.astype(xn.dtype), out_d.astype(xn.dtype)], axis=2)
    proj = jnp.einsum('btnc,ncd->btnd', br, p['w_branch'][l])
    g = jax.nn.sigmoid(gates.reshape(Bsz, T, N_BRANCH, D_MODEL))
    mix = jnp.einsum('btnd,de->bte', g * proj, p['w_out'][l])
    return mix, (s_hgrn_new, s_ssm_new, conv_new, c_new, n_new, m_new, a_v)


def peer(x, wq, keys, u_tab, v_tab):
    Bsz, T, D = x.shape
    ntok = Bsz * T
    xt = x.reshape(ntok, D)
    q = (xt @ wq).reshape(ntok, PEER_HEADS, 2, PEER_DKEY // 2)
    s = jnp.einsum('thpc,hpnc->thpn', q, keys).astype(jnp.float32)
    s_top, i_top = lax.top_k(s, PEER_TOPK)
    cand = (s_top[:, :, 0, :, None] + s_top[:, :, 1, None, :]).reshape(ntok, PEER_HEADS, PEER_TOPK * PEER_TOPK)
    cidx = (i_top[:, :, 0, :, None] * PEER_NKEYS + i_top[:, :, 1, None, :]).reshape(ntok, PEER_HEADS, PEER_TOPK * PEER_TOPK)
    f_s, f_i = lax.top_k(cand, PEER_TOPK)
    experts = jnp.take_along_axis(cidx, f_i, axis=-1)
    gate = jax.nn.softmax(f_s, axis=-1).astype(x.dtype)
    nb = -(-ntok // PEER_BLOCK)
    pad = nb * PEER_BLOCK - ntok
    xp = jnp.pad(xt, ((0, pad), (0, 0))).reshape(nb, PEER_BLOCK, D)
    ep = jnp.pad(experts, ((0, pad), (0, 0), (0, 0))).reshape(nb, PEER_BLOCK, PEER_HEADS, PEER_TOPK)
    gp = jnp.pad(gate, ((0, pad), (0, 0), (0, 0))).reshape(nb, PEER_BLOCK, PEER_HEADS, PEER_TOPK)

    def block(args):
        xb, eb, gb = args
        act = jax.nn.gelu(jnp.einsum('thkd,td->thk', u_tab[eb], xb), approximate=False)
        return jnp.einsum('thk,thkd->td', act * gb, v_tab[eb])

    y = lax.map(block, (xp, ep, gp)).reshape(nb * PEER_BLOCK, D)[:ntok]
    return y.reshape(Bsz, T, D)


def run_trunk(x, st_hgrn, st_ssm, st_conv, st_c, st_n, st_m, p):
    lbs = jax.nn.softmax(p['b_lb_logits'].astype(jnp.float32), axis=0)
    lbs = jnp.cumsum(lbs, axis=0) - lbs[0]
    news = []
    for l in range(DEPTH):
        xn = rmsnorm(x, p['norm1_w'][l])
        mix, st = mixer_layer(xn, lbs[l], p, l, st_hgrn[l], st_ssm[l], st_conv[l], st_c[l], st_n[l], st_m[l])
        x = x + mix.astype(x.dtype)
        x = x + peer(rmsnorm(x, p['norm2_w'][l]), p['peer_wq'][l], p['peer_keys'][l], p['peer_u'][l], p['peer_v'][l]).astype(x.dtype)
        news.append(st)
    y = rmsnorm(x, p['final_norm_w'])
    stacked = [jnp.stack([s[i] for s in news], axis=0) for i in range(7)]
    return y, stacked


def setup_inputs(seed: int = 0) -> dict:
    key = jax.random.key(seed)
    ks = jax.random.split(key, 40)
    f32 = jnp.float32

    def nrm(k, shape, scale):
        return jax.random.normal(k, shape, f32) * scale

    dt0 = jnp.exp(jax.random.uniform(ks[16], (DEPTH, C_HEADS), f32, np.log(1e-3), np.log(1e-1)))
    return {
        'x_prompt': nrm(ks[0], (BATCH, SEQ, D_MODEL), 1.0),
        'x_sample': nrm(ks[1], (DEC_BATCH, DEC_SEQ, D_MODEL), 1.0),
        'state_hgrn': nrm(ks[2], (DEPTH, DEC_BATCH, B_HEADS, B_DK, B_DV), 0.5),
        'state_ssm': nrm(ks[3], (DEPTH, DEC_BATCH, C_HEADS, C_DSTATE, C_HEADDIM), 0.5),
        'state_conv': nrm(ks[4], (DEPTH, DEC_BATCH, C_CONV - 1, C_CONV_DIM), 1.0),
        'state_mlstm_c': nrm(ks[5], (DEPTH, DEC_BATCH, D_HEADS, D_DK, D_DV), 1.0),
        'state_mlstm_n': nrm(ks[6], (DEPTH, DEC_BATCH, D_HEADS, D_DK), 1.0),
        'state_mlstm_m': nrm(ks[7], (DEPTH, DEC_BATCH, D_HEADS), 1.0),
        'norm1_w': 1.0 + nrm(ks[8], (DEPTH, D_MODEL), 0.02),
        'w_in': nrm(ks[9], (DEPTH, D_MODEL, IN_COLS), D_MODEL ** -0.5),
        'a_ln_w': 1.0 + nrm(ks[10], (DEPTH, A_W), 0.02),
        'a_ln_b': nrm(ks[11], (DEPTH, A_W), 0.02),
        'a_ws': nrm(ks[12], (DEPTH, A_GROUPS, A_CHUNK, A_CHUNK), A_CHUNK ** -0.5),
        'a_bs': 1.0 + nrm(ks[13], (DEPTH, A_GROUPS, A_CHUNK), 0.02),
        'b_lb_logits': nrm(ks[14], (DEPTH, B_W), 1.0),
        'b_norm_w': 1.0 + nrm(ks[15], (DEPTH, B_DV), 0.02),
        'c_conv_w': nrm(ks[17], (DEPTH, C_CONV, C_CONV_DIM), C_CONV ** -0.5),
        'c_conv_b': nrm(ks[18], (DEPTH, C_CONV_DIM), 0.02),
        'c_dt_bias': dt0 + jnp.log(-jnp.expm1(-dt0)),
        'c_a_log': jnp.log(jax.random.uniform(ks[19], (DEPTH, C_HEADS), f32, 1.0, 16.0)),
        'c_d': 1.0 + nrm(ks[20], (DEPTH, C_HEADS), 0.02),
        'c_norm_w': 1.0 + nrm(ks[21], (DEPTH, C_W), 0.02),
        'd_ig_b': -2.0 + nrm(ks[22], (DEPTH, D_HEADS), 0.1),
        'd_fg_b': 3.0 + nrm(ks[23], (DEPTH, D_HEADS), 0.1),
        'd_norm_w': 1.0 + nrm(ks[24], (DEPTH, D_W), 0.02),
        'w_branch': nrm(ks[25], (DEPTH, N_BRANCH, BRANCH_W, D_MODEL), BRANCH_W ** -0.5),
        'w_out': nrm(ks[26], (DEPTH, D_MODEL, D_MODEL), D_MODEL ** -0.5),
        'norm2_w': 1.0 + nrm(ks[27], (DEPTH, D_MODEL), 0.02),
        'peer_wq': nrm(ks[28], (DEPTH, D_MODEL, PEER_HEADS * PEER_DKEY), D_MODEL ** -0.5),
        'peer_keys': nrm(ks[29], (DEPTH, PEER_HEADS, 2, PEER_NKEYS, PEER_DKEY // 2), (PEER_DKEY // 2) ** -0.5),
        'peer_u': nrm(ks[30], (DEPTH, PEER_N, D_MODEL), D_MODEL ** -0.5),
        'peer_v': nrm(ks[31], (DEPTH, PEER_N, D_MODEL), PEER_HEADS ** -0.5),
        'final_norm_w': 1.0 + nrm(ks[32], (D_MODEL,), 0.02),
    }


def reference(x_prompt, x_sample, state_hgrn, state_ssm, state_conv, state_mlstm_c, state_mlstm_n, state_mlstm_m,
              norm1_w, w_in, a_ln_w, a_ln_b, a_ws, a_bs, b_lb_logits, b_norm_w, c_conv_w, c_conv_b, c_dt_bias,
              c_a_log, c_d, c_norm_w, d_ig_b, d_fg_b, d_norm_w, w_branch, w_out, norm2_w, peer_wq, peer_keys,
              peer_u, peer_v, final_norm_w):
    p = dict(norm1_w=norm1_w, w_in=w_in, a_ln_w=a_ln_w, a_ln_b=a_ln_b, a_ws=a_ws, a_bs=a_bs,
             b_lb_logits=b_lb_logits, b_norm_w=b_norm_w, c_conv_w=c_conv_w, c_conv_b=c_conv_b,
             c_dt_bias=c_dt_bias, c_a_log=c_a_log, c_d=c_d, c_norm_w=c_norm_w, d_ig_b=d_ig_b, d_fg_b=d_fg_b,
             d_norm_w=d_norm_w, w_branch=w_branch, w_out=w_out, norm2_w=norm2_w, peer_wq=peer_wq,
             peer_keys=peer_keys, peer_u=peer_u, peer_v=peer_v, final_norm_w=final_norm_w)
    Bp = x_prompt.shape[0]
    f32 = jnp.float32
    y_prompt, (hgrn_p, ssm_p, conv_p, mc_p, mn_p, mm_p, _chunk_v_p) = run_trunk(
        x_prompt,
        jnp.zeros((DEPTH, Bp, B_HEADS, B_DK, B_DV), f32),
        jnp.zeros((DEPTH, Bp, C_HEADS, C_DSTATE, C_HEADDIM), f32),
        jnp.zeros((DEPTH, Bp, C_CONV - 1, C_CONV_DIM), x_prompt.dtype),
        jnp.zeros((DEPTH, Bp, D_HEADS, D_DK, D_DV), f32),
        jnp.zeros((DEPTH, Bp, D_HEADS, D_DK), f32),
        jnp.zeros((DEPTH, Bp, D_HEADS), f32),
        p)
    y_sample, (hgrn_s, ssm_s, conv_s, mc_s, mn_s, mm_s, chunk_v_s) = run_trunk(
        x_sample, state_hgrn, state_ssm, state_conv, state_mlstm_c, state_mlstm_n, state_mlstm_m, p)
    return (y_prompt, y_sample, hgrn_p, ssm_p, conv_p, mc_p, mn_p, mm_p,
            hgrn_s, ssm_s, conv_s, mc_s, mn_s, mm_s, chunk_v_s)
```

```python
import functools

import jax
import jax.numpy as jnp
import numpy as np
from jax import lax
from jax.experimental import pallas as pl
from jax.experimental.pallas import tpu as pltpu

f32 = jnp.float32
bf16 = jnp.bfloat16
i32 = jnp.int32
HI = lax.Precision.HIGHEST

D_MODEL = 2048
BATCH = 4
SEQ = 2048
DEPTH = 2
DEC_BATCH = 128
DEC_SEQ = 4
N_BRANCH = 4
BW = 512
A_GROUPS = 4
A_CHUNK = 128
B_HEADS = 4
LB_FLOOR = 1e-30
C_HEADDIM = 64
C_HEADS = 8
C_GROUPS = 2
C_DSTATE = 128
C_CONV = 4
C_CONV_DIM = 1024
D_HEADS = 4
D_DK = 128
NEG_BIG = -1e30
IN_SPLITS = (512, 512, 512, 512, 512, 512, 512, 1024, 8, 512, 512, 512, 512, 4, 4, 8192)
PEER_NKEYS = 128
PEER_N = PEER_NKEYS * PEER_NKEYS
PEER_HEADS = 8
PEER_TOPK = 16
EPS = 1e-6
INV_SQRT2 = 0.7071067811865476

N_PROMPT = BATCH * SEQ
N_SAMPLE = DEC_BATCH * DEC_SEQ
N_TOK = N_PROMPT + N_SAMPLE

CB_AU, CB_AV, CB_BQ, CB_BF, CB_BI, CB_BG, CB_CZ, CB_CX0, CB_CX1, CB_DQ, CB_DK, CB_DV, CB_DO = range(13)
SMALL_COL = 13 * 512
MIX_W = 14 * 512
LANE = 128
CHUNK = 128
NCHUNK = SEQ // CHUNK
SUB = 16
SB = 8
VMEM_LIMIT = 56 * 1024 * 1024


def _gelu(x):
    return 0.5 * x * (1.0 + lax.erf(x * INV_SQRT2))


def _rms(x, w):
    ms = jnp.mean(x * x, axis=-1, keepdims=True)
    return x * lax.rsqrt(ms + EPS) * w


def _tri(n):
    r = lax.broadcasted_iota(i32, (n, n), 0)
    c = lax.broadcasted_iota(i32, (n, n), 1)
    return r >= c


def _cumsum_rows_small(x, n):
    row = lax.broadcasted_iota(i32, (n, 1), 0)
    acc = jnp.zeros_like(x)
    for s in range(n):
        acc = acc + jnp.where(row >= s, x[s:s + 1, :], 0.0)
    return acc


def _row_to_col(row, eye):
    return jnp.sum(jnp.where(eye, row, 0.0), axis=-1, keepdims=True)


def _cparams(sem, vmem=VMEM_LIMIT):
    return pltpu.CompilerParams(dimension_semantics=sem, vmem_limit_bytes=vmem)


def _norm_mm_body(x_ref, nw_ref, w_ref, o_ref, xn_ref):
    @pl.when(pl.program_id(1) == 0)
    def _():
        xn_ref[...] = _rms(x_ref[...], nw_ref[...]).astype(bf16)

    o_ref[...] = jnp.dot(xn_ref[...], w_ref[...], preferred_element_type=f32).astype(o_ref.dtype)


def _norm_mm(x, nw, w, out_dtype, tm, tn):
    m, k = x.shape
    n = w.shape[1]
    return pl.pallas_call(
        _norm_mm_body,
        out_shape=jax.ShapeDtypeStruct((m, n), out_dtype),
        grid=(m // tm, n // tn),
        in_specs=[pl.BlockSpec((tm, k), lambda i, j: (i, 0)),
                  pl.BlockSpec((1, k), lambda i, j: (0, 0)),
                  pl.BlockSpec((k, tn), lambda i, j: (0, j))],
        out_specs=pl.BlockSpec((tm, tn), lambda i, j: (i, j)),
        scratch_shapes=[pltpu.VMEM((tm, k), bf16)],
        compiler_params=_cparams(("parallel", "arbitrary")),
        name="norm_mm",
    )(x, nw, w)


def _mm_res_body(a_ref, w_ref, r_ref, o_ref):
    o_ref[...] = r_ref[...] + jnp.dot(a_ref[...], w_ref[...], preferred_element_type=f32)


def _mm_res(a, w, res, tm, tn):
    m, k = a.shape
    n = w.shape[1]
    return pl.pallas_call(
        _mm_res_body,
        out_shape=jax.ShapeDtypeStruct((m, n), f32),
        grid=(m // tm, n // tn),
        in_specs=[pl.BlockSpec((tm, k), lambda i, j: (i, 0)),
                  pl.BlockSpec((k, tn), lambda i, j: (0, j)),
                  pl.BlockSpec((tm, tn), lambda i, j: (i, j))],
        out_specs=pl.BlockSpec((tm, tn), lambda i, j: (i, j)),
        compiler_params=_cparams(("parallel", "arbitrary")),
        name="mm_res",
    )(a, w, res)


def _final_norm_body(x_ref, w_ref, o_ref):
    o_ref[...] = _rms(x_ref[...], w_ref[...])


def _final_norm(x, w, tm):
    m, k = x.shape
    return pl.pallas_call(
        _final_norm_body,
        out_shape=jax.ShapeDtypeStruct((m, k), f32),
        grid=(m // tm,),
        in_specs=[pl.BlockSpec((tm, k), lambda i: (i, 0)), pl.BlockSpec((1, k), lambda i: (0, 0))],
        out_specs=pl.BlockSpec((tm, k), lambda i: (i, 0)),
        compiler_params=_cparams(("parallel",)),
        name="final_norm",
    )(x, w)


def _a_uv(au, av, lnw, lnb):
    u = _gelu(au)
    g = _gelu(av)
    xc = g - jnp.mean(g, axis=-1, keepdims=True)
    var = jnp.mean(xc * xc, axis=-1, keepdims=True)
    v = xc * lax.rsqrt(var + EPS) * lnw + lnb
    return u, v


def _b_pre(bq, bf_, lb):
    q = bq * jax.nn.sigmoid(bq)
    logf = jnp.logaddexp(jnp.log(jnp.maximum(lb, LB_FLOOR)), jnp.log1p(-lb) + jax.nn.log_sigmoid(bf_))
    kb = (1.0 - lb) * jax.nn.sigmoid(-bf_)
    return q, kb, logf


def _hgrn_chunk(q, k, v, gl, s_mat, n, eye, valid=None):
    if valid is not None:
        gl = jnp.where(valid, gl, 0.0)
        k = jnp.where(valid, k, 0.0)
    g = _cumsum_rows_small(gl, n)
    g_last = g[n - 1:n, :]
    o = jnp.dot((q * jnp.exp(g)).astype(bf16), s_mat.astype(bf16), preferred_element_type=f32)
    row = lax.broadcasted_iota(i32, (n, 1), 0)
    for s in range(n):
        m = row >= s
        d = jnp.where(m, g - g[s:s + 1, :], 0.0)
        p = jnp.where(m, q * k[s:s + 1, :] * jnp.exp(d), 0.0)
        o = o + jnp.sum(p, axis=-1, keepdims=True) * v[s:s + 1, :]
    k_dec = k * jnp.exp(g_last - g)
    upd = lax.dot_general(k_dec.astype(bf16), v.astype(bf16), (((0,), (0,)), ((), ())), preferred_element_type=f32)
    s_new = jnp.exp(_row_to_col(g_last, eye)) * s_mat + upd
    return o, s_new


def _b_post(o, bg, nw):
    return _rms(o, nw) * (bg * jax.nn.sigmoid(bg))


def _hrow(b, c):
    return b * NCHUNK + c


def _h_spec(colblk):
    return pl.BlockSpec((CHUNK, BW), lambda b, c, cb=colblk: (_hrow(b, c), cb))


_SMALL_SPEC = pl.BlockSpec((CHUNK, LANE), lambda b, c: (_hrow(b, c), SMALL_COL // LANE))
_BR_SPEC = pl.BlockSpec((CHUNK, BW), lambda b, c: (_hrow(b, c), 0))


def _full_spec(shape):
    nd = len(shape)
    return pl.BlockSpec(shape, lambda b, c, nd=nd: (0,) * nd)


def _pa_body(au_ref, av_ref, lnw_ref, lnb_ref, ws_ref, bs_ref, o_ref):
    u, v = _a_uv(au_ref[...], av_ref[...], lnw_ref[...], lnb_ref[...])
    tri = _tri(CHUNK)
    vb = v.astype(bf16)
    parts = []
    for g in range(A_GROUPS):
        w = jnp.where(tri, ws_ref[g], 0.0).astype(bf16)
        parts.append(jnp.dot(w, vb[:, g * LANE:(g + 1) * LANE], preferred_element_type=f32))
    sp = jnp.concatenate(parts, axis=-1) + bs_ref[...]
    o_ref[...] = (u * sp).astype(bf16)


def _prompt_a(h, lnw, lnb, ws, bs_full):
    return pl.pallas_call(
        _pa_body,
        out_shape=jax.ShapeDtypeStruct((N_PROMPT, BW), bf16),
        grid=(BATCH, NCHUNK),
        in_specs=[_h_spec(CB_AU), _h_spec(CB_AV), _full_spec((1, BW)), _full_spec((1, BW)),
                  _full_spec((A_GROUPS, A_CHUNK, A_CHUNK)), _full_spec((A_CHUNK, BW))],
        out_specs=_BR_SPEC,
        compiler_params=_cparams(("parallel", "parallel")),
        name="prompt_gmlp",
    )(h, h, lnw, lnb, ws, bs_full)


def _pb_body(bq_ref, bf_ref, bi_ref, bg_ref, lb_ref, nw_ref, o_ref, st_ref, s_ref, q_s, k_s, v_s, g_s, o_s):
    c = pl.program_id(1)

    @pl.when(c == 0)
    def _():
        s_ref[...] = jnp.zeros_like(s_ref)

    q, kb, logf = _b_pre(bq_ref[...], bf_ref[...], lb_ref[...])
    q_s[...] = q
    k_s[...] = kb
    v_s[...] = bi_ref[...]
    g_s[...] = logf
    eye = lax.broadcasted_iota(i32, (LANE, LANE), 0) == lax.broadcasted_iota(i32, (LANE, LANE), 1)
    for hd in range(B_HEADS):
        hs = slice(hd * LANE, (hd + 1) * LANE)

        def sub(j, carry, hs=hs, hd=hd):
            r = pl.multiple_of(j * SUB, SUB)
            o, s_new = _hgrn_chunk(q_s[pl.ds(r, SUB), hs], k_s[pl.ds(r, SUB), hs], v_s[pl.ds(r, SUB), hs],
                                   g_s[pl.ds(r, SUB), hs], s_ref[hd], SUB, eye)
            o_s[pl.ds(r, SUB), hs] = o
            s_ref[hd] = s_new
            return carry

        lax.fori_loop(0, CHUNK // SUB, sub, 0)
    bg = bg_ref[...]
    nw = nw_ref[...]
    for hd in range(B_HEADS):
        hs = slice(hd * LANE, (hd + 1) * LANE)
        o_ref[:, hs] = _b_post(o_s[:, hs], bg[:, hs], nw).astype(bf16)

    @pl.when(c == NCHUNK - 1)
    def _():
        st_ref[0] = s_ref[...]


def _prompt_b(h, lb, nw):
    return pl.pallas_call(
        _pb_body,
        out_shape=(jax.ShapeDtypeStruct((N_PROMPT, BW), bf16),
                   jax.ShapeDtypeStruct((BATCH, B_HEADS, LANE, LANE), f32)),
        grid=(BATCH, NCHUNK),
        in_specs=[_h_spec(CB_BQ), _h_spec(CB_BF), _h_spec(CB_BI), _h_spec(CB_BG),
                  _full_spec((1, BW)), _full_spec((1, LANE))],
        out_specs=(_BR_SPEC, pl.BlockSpec((1, B_HEADS, LANE, LANE), lambda b, c: (b, 0, 0, 0))),
        scratch_shapes=[pltpu.VMEM((B_HEADS, LANE, LANE), f32)] + [pltpu.VMEM((CHUNK, BW), f32)] * 5,
        compiler_params=_cparams(("parallel", "arbitrary")),
        name="prompt_hgrn",
    )(h, h, h, h, lb, nw)


def _c_conv_silu(win0, win1, win2, win3, cw_ref, cb_ref):
    y = cb_ref[...] + win0 * cw_ref[0:1, :] + win1 * cw_ref[1:2, :] + win2 * cw_ref[2:3, :] + win3 * cw_ref[3:4, :]
    return y * jax.nn.sigmoid(y)


def _c_post(yc, cz, nw):
    y = yc * (cz * jax.nn.sigmoid(cz))
    gw = BW // C_GROUPS
    parts = [_rms(y[:, g * gw:(g + 1) * gw], nw[:, g * gw:(g + 1) * gw]) for g in range(C_GROUPS)]
    return jnp.concatenate(parts, axis=-1)


def _pc_body(cz_ref, cx0_ref, cx1_ref, sm_ref, cw_ref, cb_ref, par_ref, nw_ref,
             o_ref, st_ref, cv_ref, xpad, sp_ref):
    c = pl.program_id(1)

    @pl.when(c == 0)
    def _():
        xpad[pl.ds(0, 8), :] = jnp.zeros((8, C_CONV_DIM), f32)
        sp_ref[...] = jnp.zeros_like(sp_ref)

    xpad[pl.ds(8, CHUNK), 0:BW] = cx0_ref[...]
    xpad[pl.ds(8, CHUNK), BW:2 * BW] = cx1_ref[...]
    xbc = _c_conv_silu(xpad[pl.ds(5, CHUNK), :], xpad[pl.ds(6, CHUNK), :], xpad[pl.ds(7, CHUNK), :],
                       xpad[pl.ds(8, CHUNK), :], cw_ref, cb_ref)

    @pl.when(c == NCHUNK - 1)
    def _():
        cv_ref[0] = xpad[pl.ds(CHUNK + 5, 3), :]

    xpad[pl.ds(0, 8), :] = xpad[pl.ds(CHUNK, 8), :]

    xs = xbc[:, 0:BW]
    bm = xbc[:, BW:BW + 2 * C_DSTATE]
    cm = xbc[:, BW + 2 * C_DSTATE:]
    par = par_ref[...]
    dt = jax.nn.softplus(sm_ref[...] + par[0:1, :])
    gl = dt * (-jnp.exp(par[1:2, :]))
    tri = _tri(CHUNK)
    g = jnp.dot(tri.astype(f32), gl, precision=HI, preferred_element_type=f32)
    gt = g.T
    dtt = dt.T
    lane = lax.broadcasted_iota(i32, (1, LANE), 1)
    lo = lane < C_HEADDIM
    cbs = []
    for grp in range(C_GROUPS):
        cg = cm[:, grp * C_DSTATE:(grp + 1) * C_DSTATE].astype(bf16)
        bg = bm[:, grp * C_DSTATE:(grp + 1) * C_DSTATE].astype(bf16)
        cbs.append(lax.dot_general(cg, bg, (((1,), (1,)), ((), ())), preferred_element_type=f32))
    ys = []
    for pr in range(C_HEADS // 2):
        xp = xs[:, pr * LANE:(pr + 1) * LANE]
        sp = sp_ref[pr]
        y = jnp.zeros((CHUNK, LANE), f32)
        upd = jnp.zeros((C_DSTATE, LANE), f32)
        dl = jnp.zeros((1, LANE), f32)
        cdl = jnp.zeros((1, LANE), f32)
        for sub in range(2):
            hd = 2 * pr + sub
            grp = hd // (C_HEADS // C_GROUPS)
            lm = lo if sub == 0 else jnp.logical_not(lo)
            col = g[:, hd:hd + 1]
            g_last = col[CHUNK - 1:CHUNK, :]
            dec = jnp.exp(jnp.where(tri, col - gt[hd:hd + 1, :], 0.0))
            sc = jnp.where(tri, cbs[grp] * dec * dtt[hd:hd + 1, :], 0.0)
            xm = jnp.where(lm, xp, 0.0).astype(bf16)
            cg = cm[:, grp * C_DSTATE:(grp + 1) * C_DSTATE]
            bg = bm[:, grp * C_DSTATE:(grp + 1) * C_DSTATE]
            y = y + jnp.dot(sc.astype(bf16), xm, preferred_element_type=f32)
            y = y + jnp.dot((cg * jnp.exp(col)).astype(bf16), jnp.where(lm, sp, 0.0).astype(bf16),
                            preferred_element_type=f32)
            kd = bg * (dt[:, hd:hd + 1] * jnp.exp(g_last - col))
            upd = upd + lax.dot_general(kd.astype(bf16), xm, (((0,), (0,)), ((), ())), preferred_element_type=f32)
            dl = jnp.where(lm, jnp.exp(g_last), dl)
            cdl = jnp.where(lm, par[2:3, hd:hd + 1], cdl)
        sp_ref[pr] = dl * sp + upd
        ys.append(y + cdl * xp)
    yc = jnp.concatenate(ys, axis=-1)
    o_ref[...] = _c_post(yc, cz_ref[...], nw_ref[...]).astype(bf16)

    @pl.when(c == NCHUNK - 1)
    def _():
        for pr in range(C_HEADS // 2):
            st_ref[0, 2 * pr] = sp_ref[pr][:, 0:C_HEADDIM]
            st_ref[0, 2 * pr + 1] = sp_ref[pr][:, C_HEADDIM:]


def _prompt_c(h, cw, cb, par, nw):
    return pl.pallas_call(
        _pc_body,
        out_shape=(jax.ShapeDtypeStruct((N_PROMPT, BW), bf16),
                   jax.ShapeDtypeStruct((BATCH, C_HEADS, C_DSTATE, C_HEADDIM), f32),
                   jax.ShapeDtypeStruct((BATCH, C_CONV - 1, C_CONV_DIM), f32)),
        grid=(BATCH, NCHUNK),
        in_specs=[_h_spec(CB_CZ), _h_spec(CB_CX0), _h_spec(CB_CX1), _SMALL_SPEC,
                  _full_spec((C_CONV, C_CONV_DIM)), _full_spec((1, C_CONV_DIM)), _full_spec((8, LANE)),
                  _full_spec((1, BW))],
        out_specs=(_BR_SPEC,
                   pl.BlockSpec((1, C_HEADS, C_DSTATE, C_HEADDIM), lambda b, c: (b, 0, 0, 0)),
                   pl.BlockSpec((1, C_CONV - 1, C_CONV_DIM), lambda b, c: (b, 0, 0))),
        scratch_shapes=[pltpu.VMEM((CHUNK + 8, C_CONV_DIM), f32), pltpu.VMEM((C_HEADS // 2, C_DSTATE, LANE), f32)],
        compiler_params=_cparams(("parallel", "arbitrary")),
        name="prompt_ssd",
    )(h, h, h, h, cw, cb, par, nw)


def _pd_body(dq_ref, dk_ref, dv_ref, do_ref, sm_ref, par_ref, nw_ref,
             o_ref, c_out, n_out, m_out, c_ref, n_ref, m_ref):
    c = pl.program_id(1)

    @pl.when(c == 0)
    def _():
        c_ref[...] = jnp.zeros_like(c_ref)
        n_ref[...] = jnp.zeros_like(n_ref)
        m_ref[...] = jnp.zeros_like(m_ref)

    sm = sm_ref[...] + par_ref[0:1, :]
    ls = jax.nn.log_sigmoid(sm)
    tri = _tri(CHUNK)
    bc = jnp.dot(tri.astype(f32), ls, precision=HI, preferred_element_type=f32)
    bct = bc.T
    smt = sm.T
    nw = nw_ref[...]
    for hd in range(D_HEADS):
        hs = slice(hd * LANE, (hd + 1) * LANE)
        li, lf = 8 + hd, 12 + hd
        bcol = bc[:, lf:lf + 1]
        brow = bct[lf:lf + 1, :]
        irow = smt[li:li + 1, :]
        icol = sm[:, li:li + 1]
        mprev = m_ref[hd:hd + 1, 0:1]
        dmat = jnp.where(tri, bcol - brow + irow, NEG_BIG)
        inter = bcol + mprev
        mt = jnp.maximum(inter, jnp.max(dmat, axis=-1, keepdims=True))
        w_intra = jnp.where(tri, jnp.exp(dmat - mt), 0.0)
        w_inter = jnp.exp(inter - mt)
        qh = dq_ref[:, hs] * (D_DK ** -0.5)
        kh = dk_ref[:, hs]
        vh = dv_ref[:, hs]
        qb = qh.astype(bf16)
        qk = lax.dot_general(qb, kh.astype(bf16), (((1,), (1,)), ((), ())), preferred_element_type=f32) * w_intra
        num = w_inter * jnp.dot(qb, c_ref[hd].astype(bf16), preferred_element_type=f32)
        num = num + jnp.dot(qk.astype(bf16), vh.astype(bf16), preferred_element_type=f32)
        den = w_inter * jnp.sum(qh * n_ref[hd:hd + 1, :], axis=-1, keepdims=True) + jnp.sum(qk, axis=-1, keepdims=True)
        hh = num / jnp.maximum(jnp.abs(den), jnp.exp(-mt))
        mnew = mt[CHUNK - 1:CHUNK, :]
        blast = bcol[CHUNK - 1:CHUNK, :]
        wk = jnp.exp(blast - bcol + icol - mnew)
        decay = jnp.exp(blast + mprev - mnew)
        wkk = wk * kh
        c_ref[hd] = decay * c_ref[hd] + lax.dot_general(wkk.astype(bf16), vh.astype(bf16), (((0,), (0,)), ((), ())),
                                                        preferred_element_type=f32)
        n_ref[hd:hd + 1, :] = decay * n_ref[hd:hd + 1, :] + jnp.sum(wkk, axis=0, keepdims=True)
        m_ref[hd:hd + 1, :] = jnp.broadcast_to(mnew, (1, LANE))
        o_ref[:, hs] = (jax.nn.sigmoid(do_ref[:, hs]) * _rms(hh, nw[:, hs])).astype(bf16)

    @pl.when(c == NCHUNK - 1)
    def _():
        c_out[0] = c_ref[...]
        n_out[0] = n_ref[...]
        m_out[0] = m_ref[...]


def _prompt_d(h, par, nw):
    return pl.pallas_call(
        _pd_body,
        out_shape=(jax.ShapeDtypeStruct((N_PROMPT, BW), bf16),
                   jax.ShapeDtypeStruct((BATCH, D_HEADS, LANE, LANE), f32),
                   jax.ShapeDtypeStruct((BATCH, 8, LANE), f32),
                   jax.ShapeDtypeStruct((BATCH, 8, LANE), f32)),
        grid=(BATCH, NCHUNK),
        in_specs=[_h_spec(CB_DQ), _h_spec(CB_DK), _h_spec(CB_DV), _h_spec(CB_DO), _SMALL_SPEC,
                  _full_spec((8, LANE)), _full_spec((1, BW))],
        out_specs=(_BR_SPEC,
                   pl.BlockSpec((1, D_HEADS, LANE, LANE), lambda b, c: (b, 0, 0, 0)),
                   pl.BlockSpec((1, 8, LANE), lambda b, c: (b, 0, 0)),
                   pl.BlockSpec((1, 8, LANE), lambda b, c: (b, 0, 0))),
        scratch_shapes=[pltpu.VMEM((D_HEADS, LANE, LANE), f32), pltpu.VMEM((8, LANE), f32), pltpu.VMEM((8, LANE), f32)],
        compiler_params=_cparams(("parallel", "arbitrary")),
        name="prompt_mlstm",
    )(h, h, h, h, h, par, nw)


def _sample_body(h_ref, hg_ref, ssm_ref, cv_ref, mc_ref, mn_ref, mm_ref,
                 lnw_ref, lnb_ref, ws8_ref, bs8_ref, lb_ref, bnw_ref,
                 cw_ref, cb_ref, cpar_ref, cnw_ref, dpar_ref, dnw_ref,
                 oa_ref, ob_ref, oc_ref, od_ref, chv_ref,
                 hg_out, ssm_out, cv_out, mc_out, mn_out, mm_out):
    row = lax.broadcasted_iota(i32, (8, 1), 0)
    first = row < DEC_SEQ
    tpos = row & (DEC_SEQ - 1)
    eye = lax.broadcasted_iota(i32, (LANE, LANE), 0) == lax.broadcasted_iota(i32, (LANE, LANE), 1)
    lane = lax.broadcasted_iota(i32, (1, LANE), 1)
    lo = lane < C_HEADDIM

    def tile(p, carry):
        r = pl.multiple_of(p * 8, 8)

        def col(blk, width=BW):
            return h_ref[pl.ds(r, 8), blk * BW:blk * BW + width]

        small = h_ref[pl.ds(r, 8), SMALL_COL:SMALL_COL + LANE]

        u, v = _a_uv(col(CB_AU), col(CB_AV), lnw_ref[...], lnb_ref[...])
        chv_ref[pl.ds(r, 8), :] = v
        sp = bs8_ref[...]
        for s in range(8):
            sp = sp + ws8_ref[s] * v[s:s + 1, :]
        oa_ref[pl.ds(r, 8), :] = (u * sp).astype(bf16)

        q, kb, logf = _b_pre(col(CB_BQ), col(CB_BF), lb_ref[...])
        bi = col(CB_BI)
        bg = col(CB_BG)
        for hd in range(B_HEADS):
            hs = slice(hd * LANE, (hd + 1) * LANE)
            outs = []
            for which in range(2):
                sq = 2 * p + which
                valid = first if which == 0 else jnp.logical_not(first)
                o, s_new = _hgrn_chunk(q[:, hs], kb[:, hs], bi[:, hs], logf[:, hs], hg_ref[sq, hd], 8, eye, valid)
                hg_out[sq, hd] = s_new
                outs.append(o)
            o = jnp.where(first, outs[0], outs[1])
            ob_ref[pl.ds(r, 8), hs] = _b_post(o, bg[:, hs], bnw_ref[...]).astype(bf16)

        x = h_ref[pl.ds(r, 8), CB_CX0 * BW:CB_CX0 * BW + C_CONV_DIM]
        bufs = [jnp.where(first, cv_ref[2 * p, k:k + 1, :], cv_ref[2 * p + 1, k:k + 1, :]) for k in range(3)]
        r1 = pltpu.roll(x, 1, 0)
        r2 = pltpu.roll(x, 2, 0)
        r3 = pltpu.roll(x, 3, 0)
        sh1 = jnp.where(tpos >= 1, r1, bufs[2])
        sh2 = jnp.where(tpos >= 2, r2, jnp.where(tpos == 0, bufs[1], bufs[2]))
        sh3 = jnp.where(tpos >= 3, r3, jnp.where(tpos == 0, bufs[0], jnp.where(tpos == 1, bufs[1], bufs[2])))
        xbc = _c_conv_silu(sh3, sh2, sh1, x, cw_ref, cb_ref)
        cv_out[2 * p] = pltpu.roll(x, 7, 0)[0:3, :]
        cv_out[2 * p + 1] = r3[0:3, :]
        xs = xbc[:, 0:BW]
        bm = xbc[:, BW:BW + 2 * C_DSTATE]
        cm = xbc[:, BW + 2 * C_DSTATE:]
        cpar = cpar_ref[...]
        dt = jax.nn.softplus(small + cpar[0:1, :])
        gl_all = dt * (-jnp.exp(cpar[1:2, :]))
        ys = [None] * (C_HEADS // 2)
        for which in range(2):
            sq = 2 * p + which
            valid = first if which == 0 else jnp.logical_not(first)
            g = _cumsum_rows_small(jnp.where(valid, gl_all, 0.0), 8)
            dtv = jnp.where(valid, dt, 0.0)
            dots = []
            for grp in range(C_GROUPS):
                cg = cm[:, grp * C_DSTATE:(grp + 1) * C_DSTATE]
                bgp = bm[:, grp * C_DSTATE:(grp + 1) * C_DSTATE]
                dots.append([jnp.sum(cg * bgp[s:s + 1, :], axis=-1, keepdims=True) for s in range(8)])
            for pr in range(C_HEADS // 2):
                xp = xs[:, pr * LANE:(pr + 1) * LANE]
                sp_lo = ssm_ref[sq, 2 * pr]
                sp_hi = ssm_ref[sq, 2 * pr + 1]
                spair = jnp.concatenate([sp_lo, sp_hi], axis=-1)
                y = jnp.zeros((8, LANE), f32)
                upd = jnp.zeros((C_DSTATE, LANE), f32)
                dl = jnp.zeros((1, LANE), f32)
                cdl = jnp.zeros((1, LANE), f32)
                for sub in range(2):
                    hd = 2 * pr + sub
                    grp = hd // (C_HEADS // C_GROUPS)
                    lm = lo if sub == 0 else jnp.logical_not(lo)
                    gcol = g[:, hd:hd + 1]
                    g_last = gcol[7:8, :]
                    xm = jnp.where(lm, xp, 0.0)
                    for s in range(8):
                        coef = dots[grp][s] * jnp.where(row >= s, jnp.exp(jnp.where(row >= s, gcol - gcol[s:s + 1, :], 0.0))
                                                        * dtv[s:s + 1, hd:hd + 1], 0.0)
                        y = y + coef * xm[s:s + 1, :]
                    cg = cm[:, grp * C_DSTATE:(grp + 1) * C_DSTATE]
                    bgp = bm[:, grp * C_DSTATE:(grp + 1) * C_DSTATE]
                    y = y + jnp.dot((cg * jnp.exp(gcol)).astype(bf16), jnp.where(lm, spair, 0.0).astype(bf16),
                                    preferred_element_type=f32)
                    kd = bgp * (dtv[:, hd:hd + 1] * jnp.exp(g_last - gcol))
                    upd = upd + lax.dot_general(kd.astype(bf16), xm.astype(bf16), (((0,), (0,)), ((), ())),
                                                preferred_element_type=f32)
                    dl = jnp.where(lm, jnp.exp(g_last), dl)
                    cdl = jnp.where(lm, cpar[2:3, hd:hd + 1], cdl)
                snew = dl * spair + upd
                ssm_out[sq, 2 * pr] = snew[:, 0:C_HEADDIM]
                ssm_out[sq, 2 * pr + 1] = snew[:, C_HEADDIM:]
                y = y + cdl * xp
                ys[pr] = y if which == 0 else jnp.where(first, ys[pr], y)
        yc = jnp.concatenate(ys, axis=-1)
        oc_ref[pl.ds(r, 8), :] = _c_post(yc, col(CB_CZ), cnw_ref[...]).astype(bf16)

        smd = small + dpar_ref[0:1, :]
        lsd = jax.nn.log_sigmoid(smd)
        dq = col(CB_DQ)
        dk = col(CB_DK)
        dv = col(CB_DV)
        do = col(CB_DO)
        dnw = dnw_ref[...]
        hs_out = [None] * D_HEADS
        for which in range(2):
            sq = 2 * p + which
            valid = first if which == 0 else jnp.logical_not(first)
            bcs = _cumsum_rows_small(jnp.where(valid, lsd, 0.0), 8)
            igv = jnp.where(valid, smd, NEG_BIG)
            mrow = mm_ref[pl.ds(sq, 1), :]
            mnew_row = jnp.zeros((1, D_HEADS), f32)
            hlane = lax.broadcasted_iota(i32, (1, D_HEADS), 1)
            for hd in range(D_HEADS):
                hs = slice(hd * LANE, (hd + 1) * LANE)
                li, lf = 8 + hd, 12 + hd
                bcol = bcs[:, lf:lf + 1]
                icol = igv[:, li:li + 1]
                mprev = mrow[:, hd:hd + 1]
                dcols = [jnp.where(row >= s, bcol - bcol[s:s + 1, :] + icol[s:s + 1, :], NEG_BIG) for s in range(8)]
                mx = dcols[0]
                for s in range(1, 8):
                    mx = jnp.maximum(mx, dcols[s])
                inter = bcol + mprev
                mt = jnp.maximum(inter, mx)
                w_inter = jnp.exp(inter - mt)
                qh = dq[:, hs] * (D_DK ** -0.5)
                kh = dk[:, hs]
                vh = dv[:, hs]
                cmat = mc_ref[sq, hd]
                nrow = mn_ref[sq, pl.ds(hd, 1), :]
                num = w_inter * jnp.dot(qh.astype(bf16), cmat.astype(bf16), preferred_element_type=f32)
                den = w_inter * jnp.sum(qh * nrow, axis=-1, keepdims=True)
                for s in range(8):
                    w = jnp.where(row >= s, jnp.exp(dcols[s] - mt), 0.0)
                    qk = jnp.sum(qh * kh[s:s + 1, :], axis=-1, keepdims=True) * w
                    num = num + qk * vh[s:s + 1, :]
                    den = den + qk
                hh = num / jnp.maximum(jnp.abs(den), jnp.exp(-mt))
                mnew = mt[7:8, :]
                blast = bcol[7:8, :]
                wk = jnp.exp(blast - bcol + icol - mnew)
                decay = jnp.exp(blast + mprev - mnew)
                wkk = wk * kh
                mc_out[sq, hd] = decay * cmat + lax.dot_general(wkk.astype(bf16), vh.astype(bf16), (((0,), (0,)), ((), ())),
                                                                preferred_element_type=f32)
                mn_out[sq, pl.ds(hd, 1), :] = decay * nrow + jnp.sum(wkk, axis=0, keepdims=True)
                mnew_row = jnp.where(hlane == hd, mnew, mnew_row)
                o = jax.nn.sigmoid(do[:, hs]) * _rms(hh, dnw[:, hs])
                hs_out[hd] = o if which == 0 else jnp.where(first, hs_out[hd], o)
            mm_out[pl.ds(sq, 1), :] = mnew_row
        for hd in range(D_HEADS):
            od_ref[pl.ds(r, 8), hd * LANE:(hd + 1) * LANE] = hs_out[hd].astype(bf16)
        return carry

    lax.fori_loop(0, SB // 2, tile, 0)


def _sample_mixer(h, st_hgrn, st_ssm, st_conv, st_c, st_n, st_m, pa, pb, pc, pd):
    rows = SB * DEC_SEQ
    row0 = N_PROMPT // rows

    def blk(shape):
        nd = len(shape)
        return pl.BlockSpec((SB,) + shape, lambda i, nd=nd: (i,) + (0,) * nd)

    def full(shape):
        nd = len(shape)
        return pl.BlockSpec(shape, lambda i, nd=nd: (0,) * nd)

    state_specs = [blk((B_HEADS, LANE, LANE)), blk((C_HEADS, C_DSTATE, C_HEADDIM)), blk((C_CONV - 1, C_CONV_DIM)),
                   blk((D_HEADS, LANE, LANE)), blk((D_HEADS, LANE)), blk((D_HEADS,))]
    params = list(pa) + list(pb) + list(pc) + list(pd)
    br_spec = pl.BlockSpec((rows, BW), lambda i: (i, 0))
    br_shape = jax.ShapeDtypeStruct((N_SAMPLE, BW), bf16)
    return pl.pallas_call(
        _sample_body,
        out_shape=(br_shape, br_shape, br_shape, br_shape, jax.ShapeDtypeStruct((N_SAMPLE, BW), f32),
                   jax.ShapeDtypeStruct(st_hgrn.shape, f32), jax.ShapeDtypeStruct(st_ssm.shape, f32),
                   jax.ShapeDtypeStruct(st_conv.shape, f32), jax.ShapeDtypeStruct(st_c.shape, f32),
                   jax.ShapeDtypeStruct(st_n.shape, f32), jax.ShapeDtypeStruct(st_m.shape, f32)),
        grid=(DEC_BATCH // SB,),
        in_specs=[pl.BlockSpec((rows, MIX_W), lambda i: (row0 + i, 0))] + state_specs + [full(p.shape) for p in params],
        out_specs=(br_spec, br_spec, br_spec, br_spec, br_spec) + tuple(state_specs),
        compiler_params=_cparams(("parallel",)),
        name="sample_mixer",
    )(h, st_hgrn, st_ssm, st_conv, st_c, st_n, st_m, *params)


def _merge_body(x_ref, nw_ref, ba_ref, bb_ref, bc_ref, bd_ref, wg_ref, wb_ref, o_ref, xn_ref, acc_ref):
    jc = pl.program_id(1)
    n = pl.program_id(2)

    @pl.when((jc == 0) & (n == 0))
    def _():
        xn_ref[...] = _rms(x_ref[...], nw_ref[...]).astype(bf16)

    gate = jax.nn.sigmoid(jnp.dot(xn_ref[...], wg_ref[...], preferred_element_type=f32))
    for k, br_ref in enumerate((ba_ref, bb_ref, bc_ref, bd_ref)):
        @pl.when(n == k)
        def _(br_ref=br_ref, k=k):
            contrib = gate * jnp.dot(br_ref[...], wb_ref[0], preferred_element_type=f32)
            if k == 0:
                acc_ref[...] = contrib
            else:
                acc_ref[...] = acc_ref[...] + contrib

    @pl.when(n == N_BRANCH - 1)
    def _():
        o_ref[...] = acc_ref[...].astype(bf16)


def _merge(x, nw, brs, w_gate, w_branch, tm, tn):
    m = x.shape[0]
    ncol = D_MODEL // tn
    br_spec = pl.BlockSpec((tm, BW), lambda i, jc, n: (i, 0))
    return pl.pallas_call(
        _merge_body,
        out_shape=jax.ShapeDtypeStruct((m, D_MODEL), bf16),
        grid=(m // tm, ncol, N_BRANCH),
        in_specs=[pl.BlockSpec((tm, D_MODEL), lambda i, jc, n: (i, 0)),
                  pl.BlockSpec((1, D_MODEL), lambda i, jc, n: (0, 0)),
                  br_spec, br_spec, br_spec, br_spec,
                  pl.BlockSpec((D_MODEL, tn), lambda i, jc, n: (0, n * ncol + jc)),
                  pl.BlockSpec((1, BW, tn), lambda i, jc, n: (n, 0, jc))],
        out_specs=pl.BlockSpec((tm, tn), lambda i, jc, n: (i, jc)),
        scratch_shapes=[pltpu.VMEM((tm, D_MODEL), bf16), pltpu.VMEM((tm, tn), f32)],
        compiler_params=_cparams(("parallel", "arbitrary", "arbitrary")),
        name="merge",
    )(x, nw, *brs, w_gate, w_branch)


def _top16(s, width):
    lane = lax.broadcasted_iota(i32, s.shape, 1)
    out = []
    for _ in range(PEER_TOPK):
        m = jnp.max(s, axis=-1, keepdims=True)
        am = jnp.min(jnp.where(s == m, lane, width), axis=-1, keepdims=True)
        out.append((m, am))
        s = jnp.where(lane == am, -jnp.inf, s)
    return out


def _peer_route_body(q_ref, keys_ref, ia_ref, ib_ref, gt_ref):
    tb = q_ref.shape[0]
    lane = lax.broadcasted_iota(i32, (tb, LANE), 1)
    jrow = lax.broadcasted_iota(i32, (LANE, 2 * LANE), 0)
    pcol = lax.broadcasted_iota(i32, (LANE, 2 * LANE), 1)
    ea = jnp.where((pcol >> 4) == jrow, 1.0, 0.0)
    eb = jnp.where((pcol & 15) == jrow, 1.0, 0.0)
    ia = jnp.zeros((tb, LANE), i32)
    ib = jnp.zeros((tb, LANE), i32)
    gt = jnp.zeros((tb, LANE), f32)
    for hd in range(PEER_HEADS):
        vals = []
        idxs = []
        for p in range(2):
            off = (hd * 2 + p) * LANE
            s = lax.dot_general(q_ref[:, off:off + LANE], keys_ref[hd, p], (((1,), (1,)), ((), ())),
                                preferred_element_type=f32)
            v = jnp.zeros((tb, LANE), f32)
            ix = jnp.zeros((tb, LANE), f32)
            for k, (m, am) in enumerate(_top16(s, LANE)):
                v = jnp.where(lane == k, m, v)
                ix = jnp.where(lane == k, am.astype(f32), ix)
            vals.append(v)
            idxs.append(ix)
        cand = (jnp.dot(vals[0], ea, precision=HI, preferred_element_type=f32)
                + jnp.dot(vals[1], eb, precision=HI, preferred_element_type=f32))
        cidx = (jnp.dot(idxs[0], ea, precision=HI, preferred_element_type=f32) * float(PEER_NKEYS)
                + jnp.dot(idxs[1], eb, precision=HI, preferred_element_type=f32))
        pos = lax.broadcasted_iota(i32, (tb, 2 * LANE), 1)
        fs = jnp.zeros((tb, LANE), f32)
        ex = jnp.zeros((tb, LANE), f32)
        mx = None
        for k, (m, am) in enumerate(_top16(cand, 2 * LANE)):
            if k == 0:
                mx = m
            e = jnp.max(jnp.where(pos == am, cidx, -1.0), axis=-1, keepdims=True)
            sel = lane == hd * PEER_TOPK + k
            fs = jnp.where(sel, m, fs)
            ex = jnp.where(sel, e, ex)
        inhead = (lane >= hd * PEER_TOPK) & (lane < (hd + 1) * PEER_TOPK)
        pe = jnp.where(inhead, jnp.exp(fs - mx), 0.0)
        g = pe / jnp.sum(pe, axis=-1, keepdims=True)
        exi = ex.astype(i32)
        ia = jnp.where(inhead, exi >> 7, ia)
        ib = jnp.where(inhead, exi & (PEER_NKEYS - 1), ib)
        gt = jnp.where(inhead, g, gt)
    ia_ref[...] = ia
    ib_ref[...] = ib
    gt_ref[...] = gt


def _peer_route(q, keys, tb):
    m = q.shape[0]
    spec = pl.BlockSpec((tb, LANE), lambda i: (i, 0))
    return pl.pallas_call(
        _peer_route_body,
        out_shape=(jax.ShapeDtypeStruct((m, LANE), i32), jax.ShapeDtypeStruct((m, LANE), i32),
                   jax.ShapeDtypeStruct((m, LANE), f32)),
        grid=(m // tb,),
        in_specs=[pl.BlockSpec((tb, D_MODEL), lambda i: (i, 0)),
                  pl.BlockSpec((PEER_HEADS, 2, PEER_NKEYS, LANE), lambda i: (0, 0, 0, 0))],
        out_specs=(spec, spec, spec),
        compiler_params=_cparams(("parallel",)),
        name="peer_route",
    )(q, keys)


def _peer_gates_body(ia_ref, ib_ref, gt_ref, o_ref):
    tg = ia_ref.shape[0]
    sub = lax.broadcasted_iota(i32, (LANE, LANE), 0)

    def step(t, carry):
        arow = ia_ref[pl.ds(t, 1), :]
        brow = ib_ref[pl.ds(t, 1), :]
        grow = gt_ref[pl.ds(t, 1), :]
        at = jnp.where(sub == arow, 1.0, 0.0).astype(bf16)
        bt = jnp.where(sub == brow, grow, 0.0).astype(bf16)
        gm = lax.dot_general(at, bt, (((1,), (1,)), ((), ())), preferred_element_type=f32)
        o_ref[t] = gm.astype(bf16)
        return carry

    lax.fori_loop(0, tg, step, 0)


def _peer_gates(ia, ib, gt, tg):
    m = ia.shape[0]
    spec = pl.BlockSpec((tg, LANE), lambda i: (i, 0))
    return pl.pallas_call(
        _peer_gates_body,
        out_shape=jax.ShapeDtypeStruct((m, PEER_NKEYS, PEER_NKEYS), bf16),
        grid=(m // tg,),
        in_specs=[spec, spec, spec],
        out_specs=pl.BlockSpec((tg, PEER_NKEYS, PEER_NKEYS), lambda i: (i, 0, 0)),
        compiler_params=_cparams(("parallel",)),
        name="peer_gates",
    )(ia, ib, gt)


def _peer_experts_body(x_ref, nw_ref, ut_ref, v_ref, g_ref, o_ref, xn_ref):
    @pl.when(pl.program_id(1) == 0)
    def _():
        x = x_ref[...]
        xn_ref[...] = _rms(x, nw_ref[...]).astype(bf16)
        o_ref[...] = x

    hmat = jnp.dot(xn_ref[...], ut_ref[...], preferred_element_type=f32)
    w = (_gelu(hmat) * g_ref[...].astype(f32)).astype(bf16)
    o_ref[...] += jnp.dot(w, v_ref[...], preferred_element_type=f32)


def _peer_experts(x, nw, ut, v, g, tb, eb):
    m = x.shape[0]
    return pl.pallas_call(
        _peer_experts_body,
        out_shape=jax.ShapeDtypeStruct((m, D_MODEL), f32),
        grid=(m // tb, PEER_N // eb),
        in_specs=[pl.BlockSpec((tb, D_MODEL), lambda i, j: (i, 0)),
                  pl.BlockSpec((1, D_MODEL), lambda i, j: (0, 0)),
                  pl.BlockSpec((D_MODEL, eb), lambda i, j: (0, j)),
                  pl.BlockSpec((eb, D_MODEL), lambda i, j: (j, 0)),
                  pl.BlockSpec((tb, eb), lambda i, j: (i, j))],
        out_specs=pl.BlockSpec((tb, D_MODEL), lambda i, j: (i, 0)),
        scratch_shapes=[pltpu.VMEM((tb, D_MODEL), bf16)],
        compiler_params=_cparams(("parallel", "arbitrary")),
        name="peer_experts",
    )(x, nw, ut, v, g)


def _prep_w_in(w):
    offs = np.cumsum((0,) + IN_SPLITS)
    parts = [w[:, offs[k]:offs[k + 1]] for k in range(len(IN_SPLITS))]
    (a_u, a_v, b_q, b_f, b_i, b_g, c_z, c_xbc, c_dt, d_q, d_k, d_v, d_o, d_ig, d_fg, gates) = parts
    small = jnp.concatenate([c_dt, d_ig, d_fg, jnp.zeros((D_MODEL, MIX_W - SMALL_COL - 16), w.dtype)], axis=1)
    w_mix = jnp.concatenate([a_u, a_v, b_q, b_f, b_i, b_g, c_z, c_xbc, d_q, d_k, d_v, d_o, small], axis=1)
    return w_mix.astype(bf16), gates.astype(bf16)


def _lane_row(vals, start):
    row = jnp.zeros((LANE,), f32)
    return row.at[start:start + vals.shape[0]].set(vals)


def kernel(x_prompt, x_sample, state_hgrn, state_ssm, state_conv, state_mlstm_c, state_mlstm_n, state_mlstm_m, norm1_w, w_in, a_ln_w, a_ln_b, a_ws, a_bs, b_lb_logits, b_norm_w, c_conv_w, c_conv_b, c_dt_bias, c_a_log, c_d, c_norm_w, d_ig_b, d_fg_b, d_norm_w, w_branch, w_out, norm2_w, peer_wq, peer_keys, peer_u, peer_v, final_norm_w):
    x = jnp.concatenate([x_prompt.reshape(N_PROMPT, D_MODEL), x_sample.reshape(N_SAMPLE, D_MODEL)], axis=0)
    lbs = jax.nn.softmax(b_lb_logits.astype(f32), axis=0)
    lbs = jnp.cumsum(lbs, axis=0) - lbs[0]
    zeros8 = jnp.zeros((8, LANE), f32)
    news_p = []
    news_s = []
    for l in range(DEPTH):
        w_mix, w_gate = _prep_w_in(w_in[l])
        h = _norm_mm(x, norm1_w[l][None, :], w_mix, f32, 1088, 512)

        lnw = a_ln_w[l][None, :]
        lnb = a_ln_b[l][None, :]
        lb = lbs[l][None, :]
        bnw = b_norm_w[l][None, :]
        cw = c_conv_w[l]
        cb = c_conv_b[l][None, :]
        cpar = zeros8.at[0].set(_lane_row(c_dt_bias[l], 0)).at[1].set(_lane_row(c_a_log[l], 0)).at[2].set(_lane_row(c_d[l], 0))
        cnw = c_norm_w[l][None, :]
        dpar = zeros8.at[0].set(_lane_row(d_ig_b[l], 8) + _lane_row(d_fg_b[l], 12))
        dnw = d_norm_w[l][None, :]

        bs_full = jnp.repeat(a_bs[l].T, LANE, axis=1)
        br_a = _prompt_a(h, lnw, lnb, a_ws[l], bs_full)
        br_b, hg_p = _prompt_b(h, lb, bnw)
        br_c, ssm_p, cv_p = _prompt_c(h, cw, cb, cpar, cnw)
        br_d, mc_p, mn_p, mm_p = _prompt_d(h, dpar, dnw)
        news_p.append((hg_p, ssm_p, cv_p, mc_p, mn_p[:, :D_HEADS, :], mm_p[:, :D_HEADS, 0]))

        w4 = jnp.tril(a_ws[l][:, :DEC_SEQ, :DEC_SEQ])
        w8 = jnp.zeros((A_GROUPS, 8, 8), f32).at[:, :4, :4].set(w4).at[:, 4:, 4:].set(w4)
        ws8 = jnp.repeat(jnp.transpose(w8, (2, 1, 0)), LANE, axis=2)
        bs8 = jnp.repeat(jnp.tile(a_bs[l][:, :DEC_SEQ], (1, 2)).T, LANE, axis=1)
        outs = _sample_mixer(h, state_hgrn[l], state_ssm[l], state_conv[l], state_mlstm_c[l], state_mlstm_n[l],
                             state_mlstm_m[l], (lnw, lnb, ws8, bs8), (lb, bnw), (cw, cb, cpar, cnw), (dpar, dnw))
        sa, sb, sc, sd, chv = outs[:5]
        news_s.append(tuple(outs[5:]) + (chv.reshape(DEC_BATCH, DEC_SEQ, BW),))

        brs = [jnp.concatenate([p, s], axis=0) for p, s in ((br_a, sa), (br_b, sb), (br_c, sc), (br_d, sd))]
        mixin = _merge(x, norm1_w[l][None, :], brs, w_gate, w_branch[l].astype(bf16), 544, 1024)
        x = _mm_res(mixin, w_out[l].astype(bf16), x, 1088, 512)

        q = _norm_mm(x, norm2_w[l][None, :], peer_wq[l].astype(bf16), bf16, 1088, 512)
        ia, ib, gt = _peer_route(q, peer_keys[l].astype(bf16), 256)
        g = _peer_gates(ia, ib, gt, 64).reshape(N_TOK, PEER_N)
        x = _peer_experts(x, norm2_w[l][None, :], peer_u[l].T.astype(bf16), peer_v[l].astype(bf16), g, 544, 512)

    y = _final_norm(x, final_norm_w[None, :], 544)
    y_prompt = y[:N_PROMPT].reshape(BATCH, SEQ, D_MODEL)
    y_sample = y[N_PROMPT:].reshape(DEC_BATCH, DEC_SEQ, D_MODEL)
    stack = lambda news, k: jnp.stack([n[k] for n in news], axis=0)
    return (y_prompt, y_sample) + tuple(stack(news_p, k) for k in range(6)) + tuple(stack(news_s, k) for k in range(7))
```

```python
import functools

import jax
import jax.numpy as jnp
import numpy as np
from jax import lax
from jax.experimental import pallas as pl
from jax.experimental.pallas import tpu as pltpu

f32 = jnp.float32
bf16 = jnp.bfloat16
i32 = jnp.int32
HI = lax.Precision.HIGHEST

D_MODEL = 2048
BATCH = 4
SEQ = 2048
DEPTH = 2
DEC_BATCH = 128
DEC_SEQ = 4
N_BRANCH = 4
BW = 512
A_GROUPS = 4
A_CHUNK = 128
B_HEADS = 4
LB_FLOOR = 1e-30
C_HEADDIM = 64
C_HEADS = 8
C_GROUPS = 2
C_DSTATE = 128
C_CONV = 4
C_CONV_DIM = 1024
D_HEADS = 4
D_DK = 128
NEG_BIG = -1e30
IN_SPLITS = (512, 512, 512, 512, 512, 512, 512, 1024, 8, 512, 512, 512, 512, 4, 4, 8192)
PEER_NKEYS = 128
PEER_N = PEER_NKEYS * PEER_NKEYS
PEER_HEADS = 8
PEER_TOPK = 16
EPS = 1e-6
INV_SQRT2 = 0.7071067811865476

N_PROMPT = BATCH * SEQ
N_SAMPLE = DEC_BATCH * DEC_SEQ
N_TOK = N_PROMPT + N_SAMPLE

CB_AU, CB_AV, CB_BQ, CB_BF, CB_BI, CB_BG, CB_CZ, CB_CX0, CB_CX1, CB_DQ, CB_DK, CB_DV, CB_DO = range(13)
SMALL_COL = 13 * 512
MIX_W = 14 * 512
LANE = 128
CHUNK = 128
NCHUNK = SEQ // CHUNK
SUB = 16
SB = 8
VMEM_LIMIT = 56 * 1024 * 1024


def _gelu(x):
    return 0.5 * x * (1.0 + lax.erf(x * INV_SQRT2))


def _rms(x, w):
    ms = jnp.mean(x * x, axis=-1, keepdims=True)
    return x * lax.rsqrt(ms + EPS) * w


def _tri(n):
    r = lax.broadcasted_iota(i32, (n, n), 0)
    c = lax.broadcasted_iota(i32, (n, n), 1)
    return r >= c


def _cumsum_rows_small(x, n):
    row = lax.broadcasted_iota(i32, (n, 1), 0)
    acc = jnp.zeros_like(x)
    for s in range(n):
        acc = acc + jnp.where(row >= s, x[s:s + 1, :], 0.0)
    return acc


def _row_to_col(row, eye):
    return jnp.sum(jnp.where(eye, row, 0.0), axis=-1, keepdims=True)


def _cparams(sem, vmem=VMEM_LIMIT):
    return pltpu.CompilerParams(dimension_semantics=sem, vmem_limit_bytes=vmem)


def _norm_mm_body(x_ref, nw_ref, w_ref, o_ref, xn_ref):
    @pl.when(pl.program_id(1) == 0)
    def _():
        xn_ref[...] = _rms(x_ref[...], nw_ref[...]).astype(bf16)

    o_ref[...] = jnp.dot(xn_ref[...], w_ref[...], preferred_element_type=f32).astype(o_ref.dtype)


def _norm_mm(x, nw, w, out_dtype, tm, tn):
    m, k = x.shape
    n = w.shape[1]
    return pl.pallas_call(
        _norm_mm_body,
        out_shape=jax.ShapeDtypeStruct((m, n), out_dtype),
        grid=(m // tm, n // tn),
        in_specs=[pl.BlockSpec((tm, k), lambda i, j: (i, 0)),
                  pl.BlockSpec((1, k), lambda i, j: (0, 0)),
                  pl.BlockSpec((k, tn), lambda i, j: (0, j))],
        out_specs=pl.BlockSpec((tm, tn), lambda i, j: (i, j)),
        scratch_shapes=[pltpu.VMEM((tm, k), bf16)],
        compiler_params=_cparams(("parallel", "arbitrary")),
        name="norm_mm",
    )(x, nw, w)


def _mm_res_body(a_ref, w_ref, r_ref, o_ref):
    o_ref[...] = r_ref[...] + jnp.dot(a_ref[...], w_ref[...], preferred_element_type=f32)


def _mm_res(a, w, res, tm, tn):
    m, k = a.shape
    n = w.shape[1]
    return pl.pallas_call(
        _mm_res_body,
        out_shape=jax.ShapeDtypeStruct((m, n), f32),
        grid=(m // tm, n // tn),
        in_specs=[pl.BlockSpec((tm, k), lambda i, j: (i, 0)),
                  pl.BlockSpec((k, tn), lambda i, j: (0, j)),
                  pl.BlockSpec((tm, tn), lambda i, j: (i, j))],
        out_specs=pl.BlockSpec((tm, tn), lambda i, j: (i, j)),
        compiler_params=_cparams(("parallel", "arbitrary")),
        name="mm_res",
    )(a, w, res)


def _final_norm_body(x_ref, w_ref, o_ref):
    o_ref[...] = _rms(x_ref[...], w_ref[...])


def _final_norm(x, w, tm):
    m, k = x.shape
    return pl.pallas_call(
        _final_norm_body,
        out_shape=jax.ShapeDtypeStruct((m, k), f32),
        grid=(m // tm,),
        in_specs=[pl.BlockSpec((tm, k), lambda i: (i, 0)), pl.BlockSpec((1, k), lambda i: (0, 0))],
        out_specs=pl.BlockSpec((tm, k), lambda i: (i, 0)),
        compiler_params=_cparams(("parallel",)),
        name="final_norm",
    )(x, w)


def _a_uv(au, av, lnw, lnb):
    u = _gelu(au)
    g = _gelu(av)
    xc = g - jnp.mean(g, axis=-1, keepdims=True)
    var = jnp.mean(xc * xc, axis=-1, keepdims=True)
    v = xc * lax.rsqrt(var + EPS) * lnw + lnb
    return u, v


def _b_pre(bq, bf_, lb):
    q = bq * jax.nn.sigmoid(bq)
    logf = jnp.logaddexp(jnp.log(jnp.maximum(lb, LB_FLOOR)), jnp.log1p(-lb) + jax.nn.log_sigmoid(bf_))
    kb = (1.0 - lb) * jax.nn.sigmoid(-bf_)
    return q, kb, logf


def _hgrn_chunk(q, k, v, gl, s_mat, n, eye, valid=None):
    if valid is not None:
        gl = jnp.where(valid, gl, 0.0)
        k = jnp.where(valid, k, 0.0)
    g = _cumsum_rows_small(gl, n)
    g_last = g[n - 1:n, :]
    o = jnp.dot((q * jnp.exp(g)).astype(bf16), s_mat.astype(bf16), preferred_element_type=f32)
    row = lax.broadcasted_iota(i32, (n, 1), 0)
    for s in range(n):
        m = row >= s
        d = jnp.where(m, g - g[s:s + 1, :], 0.0)
        p = jnp.where(m, q * k[s:s + 1, :] * jnp.exp(d), 0.0)
        o = o + jnp.sum(p, axis=-1, keepdims=True) * v[s:s + 1, :]
    k_dec = k * jnp.exp(g_last - g)
    upd = lax.dot_general(k_dec.astype(bf16), v.astype(bf16), (((0,), (0,)), ((), ())), preferred_element_type=f32)
    s_new = jnp.exp(_row_to_col(g_last, eye)) * s_mat + upd
    return o, s_new


def _b_post(o, bg, nw):
    return _rms(o, nw) * (bg * jax.nn.sigmoid(bg))


def _hrow(b, c):
    return b * NCHUNK + c


def _h_spec(colblk):
    return pl.BlockSpec((CHUNK, BW), lambda b, c, cb=colblk: (_hrow(b, c), cb))


_SMALL_SPEC = pl.BlockSpec((CHUNK, LANE), lambda b, c: (_hrow(b, c), SMALL_COL // LANE))
_BR_SPEC = pl.BlockSpec((CHUNK, BW), lambda b, c: (_hrow(b, c), 0))


def _full_spec(shape):
    nd = len(shape)
    return pl.BlockSpec(shape, lambda b, c, nd=nd: (0,) * nd)


def _pa_body(au_ref, av_ref, lnw_ref, lnb_ref, ws_ref, bs_ref, o_ref):
    u, v = _a_uv(au_ref[...], av_ref[...], lnw_ref[...], lnb_ref[...])
    tri = _tri(CHUNK)
    vb = v.astype(bf16)
    parts = []
    for g in range(A_GROUPS):
        w = jnp.where(tri, ws_ref[g], 0.0).astype(bf16)
        parts.append(jnp.dot(w, vb[:, g * LANE:(g + 1) * LANE], preferred_element_type=f32))
    sp = jnp.concatenate(parts, axis=-1) + bs_ref[...]
    o_ref[...] = (u * sp).astype(bf16)


def _prompt_a(h, lnw, lnb, ws, bs_full):
    return pl.pallas_call(
        _pa_body,
        out_shape=jax.ShapeDtypeStruct((N_PROMPT, BW), bf16),
        grid=(BATCH, NCHUNK),
        in_specs=[_h_spec(CB_AU), _h_spec(CB_AV), _full_spec((1, BW)), _full_spec((1, BW)),
                  _full_spec((A_GROUPS, A_CHUNK, A_CHUNK)), _full_spec((A_CHUNK, BW))],
        out_specs=_BR_SPEC,
        compiler_params=_cparams(("parallel", "parallel")),
        name="prompt_gmlp",
    )(h, h, lnw, lnb, ws, bs_full)


def _pb_body(bq_ref, bf_ref, bi_ref, bg_ref, lb_ref, nw_ref, o_ref, st_ref, s_ref, q_s, k_s, v_s, g_s, o_s):
    c = pl.program_id(1)

    @pl.when(c == 0)
    def _():
        s_ref[...] = jnp.zeros_like(s_ref)

    q, kb, logf = _b_pre(bq_ref[...], bf_ref[...], lb_ref[...])
    q_s[...] = q
    k_s[...] = kb
    v_s[...] = bi_ref[...]
    g_s[...] = logf
    eye = lax.broadcasted_iota(i32, (LANE, LANE), 0) == lax.broadcasted_iota(i32, (LANE, LANE), 1)
    for hd in range(B_HEADS):
        hs = slice(hd * LANE, (hd + 1) * LANE)

        def sub(j, carry, hs=hs, hd=hd):
            r = pl.multiple_of(j * SUB, SUB)
            o, s_new = _hgrn_chunk(q_s[pl.ds(r, SUB), hs], k_s[pl.ds(r, SUB), hs], v_s[pl.ds(r, SUB), hs],
                                   g_s[pl.ds(r, SUB), hs], s_ref[hd], SUB, eye)
            o_s[pl.ds(r, SUB), hs] = o
            s_ref[hd] = s_new
            return carry

        lax.fori_loop(0, CHUNK // SUB, sub, 0)
    bg = bg_ref[...]
    nw = nw_ref[...]
    for hd in range(B_HEADS):
        hs = slice(hd * LANE, (hd + 1) * LANE)
        o_ref[:, hs] = _b_post(o_s[:, hs], bg[:, hs], nw).astype(bf16)

    @pl.when(c == NCHUNK - 1)
    def _():
        st_ref[0] = s_ref[...]


def _prompt_b(h, lb, nw):
    return pl.pallas_call(
        _pb_body,
        out_shape=(jax.ShapeDtypeStruct((N_PROMPT, BW), bf16),
                   jax.ShapeDtypeStruct((BATCH, B_HEADS, LANE, LANE), f32)),
        grid=(BATCH, NCHUNK),
        in_specs=[_h_spec(CB_BQ), _h_spec(CB_BF), _h_spec(CB_BI), _h_spec(CB_BG),
                  _full_spec((1, BW)), _full_spec((1, LANE))],
        out_specs=(_BR_SPEC, pl.BlockSpec((1, B_HEADS, LANE, LANE), lambda b, c: (b, 0, 0, 0))),
        scratch_shapes=[pltpu.VMEM((B_HEADS, LANE, LANE), f32)] + [pltpu.VMEM((CHUNK, BW), f32)] * 5,
        compiler_params=_cparams(("parallel", "arbitrary")),
        name="prompt_hgrn",
    )(h, h, h, h, lb, nw)


def _c_conv_silu(win0, win1, win2, win3, cw_ref, cb_ref):
    y = cb_ref[...] + win0 * cw_ref[0:1, :] + win1 * cw_ref[1:2, :] + win2 * cw_ref[2:3, :] + win3 * cw_ref[3:4, :]
    return y * jax.nn.sigmoid(y)


def _c_post(yc, cz, nw):
    y = yc * (cz * jax.nn.sigmoid(cz))
    gw = BW // C_GROUPS
    parts = [_rms(y[:, g * gw:(g + 1) * gw], nw[:, g * gw:(g + 1) * gw]) for g in range(C_GROUPS)]
    return jnp.concatenate(parts, axis=-1)


def _pc_body(cz_ref, cx0_ref, cx1_ref, sm_ref, cw_ref, cb_ref, par_ref, nw_ref,
             o_ref, st_ref, cv_ref, xpad, sp_ref):
    c = pl.program_id(1)

    @pl.when(c == 0)
    def _():
        xpad[pl.ds(0, 8), :] = jnp.zeros((8, C_CONV_DIM), f32)
        sp_ref[...] = jnp.zeros_like(sp_ref)

    xpad[pl.ds(8, CHUNK), 0:BW] = cx0_ref[...]
    xpad[pl.ds(8, CHUNK), BW:2 * BW] = cx1_ref[...]
    xbc = _c_conv_silu(xpad[pl.ds(5, CHUNK), :], xpad[pl.ds(6, CHUNK), :], xpad[pl.ds(7, CHUNK), :],
                       xpad[pl.ds(8, CHUNK), :], cw_ref, cb_ref)

    @pl.when(c == NCHUNK - 1)
    def _():
        cv_ref[0] = xpad[pl.ds(CHUNK + 5, 3), :]

    xpad[pl.ds(0, 8), :] = xpad[pl.ds(CHUNK, 8), :]

    xs = xbc[:, 0:BW]
    bm = xbc[:, BW:BW + 2 * C_DSTATE]
    cm = xbc[:, BW + 2 * C_DSTATE:]
    par = par_ref[...]
    dt = jax.nn.softplus(sm_ref[...] + par[0:1, :])
    gl = dt * (-jnp.exp(par[1:2, :]))
    tri = _tri(CHUNK)
    g = jnp.dot(tri.astype(f32), gl, precision=HI, preferred_element_type=f32)
    gt = g.T
    dtt = dt.T
    lane = lax.broadcasted_iota(i32, (1, LANE), 1)
    lo = lane < C_HEADDIM
    cbs = []
    for grp in range(C_GROUPS):
        cg = cm[:, grp * C_DSTATE:(grp + 1) * C_DSTATE].astype(bf16)
        bg = bm[:, grp * C_DSTATE:(grp + 1) * C_DSTATE].astype(bf16)
        cbs.append(lax.dot_general(cg, bg, (((1,), (1,)), ((), ())), preferred_element_type=f32))
    ys = []
    for pr in range(C_HEADS // 2):
        xp = xs[:, pr * LANE:(pr + 1) * LANE]
        sp = sp_ref[pr]
        y = jnp.zeros((CHUNK, LANE), f32)
        upd = jnp.zeros((C_DSTATE, LANE), f32)
        dl = jnp.zeros((1, LANE), f32)
        cdl = jnp.zeros((1, LANE), f32)
        for sub in range(2):
            hd = 2 * pr + sub
            grp = hd // (C_HEADS // C_GROUPS)
            lm = lo if sub == 0 else jnp.logical_not(lo)
            col = g[:, hd:hd + 1]
            g_last = col[CHUNK - 1:CHUNK, :]
            dec = jnp.exp(jnp.where(tri, col - gt[hd:hd + 1, :], 0.0))
            sc = jnp.where(tri, cbs[grp] * dec * dtt[hd:hd + 1, :], 0.0)
            xm = jnp.where(lm, xp, 0.0).astype(bf16)
            cg = cm[:, grp * C_DSTATE:(grp + 1) * C_DSTATE]
            bg = bm[:, grp * C_DSTATE:(grp + 1) * C_DSTATE]
            y = y + jnp.dot(sc.astype(bf16), xm, preferred_element_type=f32)
            y = y + jnp.dot((cg * jnp.exp(col)).astype(bf16), jnp.where(lm, sp, 0.0).astype(bf16),
                            preferred_element_type=f32)
            kd = bg * (dt[:, hd:hd + 1] * jnp.exp(g_last - col))
            upd = upd + lax.dot_general(kd.astype(bf16), xm, (((0,), (0,)), ((), ())), preferred_element_type=f32)
            dl = jnp.where(lm, jnp.exp(g_last), dl)
            cdl = jnp.where(lm, par[2:3, hd:hd + 1], cdl)
        sp_ref[pr] = dl * sp + upd
        ys.append(y + cdl * xp)
    yc = jnp.concatenate(ys, axis=-1)
    o_ref[...] = _c_post(yc, cz_ref[...], nw_ref[...]).astype(bf16)

    @pl.when(c == NCHUNK - 1)
    def _():
        for pr in range(C_HEADS // 2):
            st_ref[0, 2 * pr] = sp_ref[pr][:, 0:C_HEADDIM]
            st_ref[0, 2 * pr + 1] = sp_ref[pr][:, C_HEADDIM:]


def _prompt_c(h, cw, cb, par, nw):
    return pl.pallas_call(
        _pc_body,
        out_shape=(jax.ShapeDtypeStruct((N_PROMPT, BW), bf16),
                   jax.ShapeDtypeStruct((BATCH, C_HEADS, C_DSTATE, C_HEADDIM), f32),
                   jax.ShapeDtypeStruct((BATCH, C_CONV - 1, C_CONV_DIM), f32)),
        grid=(BATCH, NCHUNK),
        in_specs=[_h_spec(CB_CZ), _h_spec(CB_CX0), _h_spec(CB_CX1), _SMALL_SPEC,
                  _full_spec((C_CONV, C_CONV_DIM)), _full_spec((1, C_CONV_DIM)), _full_spec((8, LANE)),
                  _full_spec((1, BW))],
        out_specs=(_BR_SPEC,
                   pl.BlockSpec((1, C_HEADS, C_DSTATE, C_HEADDIM), lambda b, c: (b, 0, 0, 0)),
                   pl.BlockSpec((1, C_CONV - 1, C_CONV_DIM), lambda b, c: (b, 0, 0))),
        scratch_shapes=[pltpu.VMEM((CHUNK + 8, C_CONV_DIM), f32), pltpu.VMEM((C_HEADS // 2, C_DSTATE, LANE), f32)],
        compiler_params=_cparams(("parallel", "arbitrary")),
        name="prompt_ssd",
    )(h, h, h, h, cw, cb, par, nw)


def _pd_body(dq_ref, dk_ref, dv_ref, do_ref, sm_ref, par_ref, nw_ref,
             o_ref, c_out, n_out, m_out, c_ref, n_ref, m_ref):
    c = pl.program_id(1)

    @pl.when(c == 0)
    def _():
        c_ref[...] = jnp.zeros_like(c_ref)
        n_ref[...] = jnp.zeros_like(n_ref)
        m_ref[...] = jnp.zeros_like(m_ref)

    sm = sm_ref[...] + par_ref[0:1, :]
    ls = jax.nn.log_sigmoid(sm)
    tri = _tri(CHUNK)
    bc = jnp.dot(tri.astype(f32), ls, precision=HI, preferred_element_type=f32)
    bct = bc.T
    smt = sm.T
    nw = nw_ref[...]
    for hd in range(D_HEADS):
        hs = slice(hd * LANE, (hd + 1) * LANE)
        li, lf = 8 + hd, 12 + hd
        bcol = bc[:, lf:lf + 1]
        brow = bct[lf:lf + 1, :]
        irow = smt[li:li + 1, :]
        icol = sm[:, li:li + 1]
        mprev = m_ref[hd:hd + 1, 0:1]
        dmat = jnp.where(tri, bcol - brow + irow, NEG_BIG)
        inter = bcol + mprev
        mt = jnp.maximum(inter, jnp.max(dmat, axis=-1, keepdims=True))
        w_intra = jnp.where(tri, jnp.exp(dmat - mt), 0.0)
        w_inter = jnp.exp(inter - mt)
        qh = dq_ref[:, hs] * (D_DK ** -0.5)
        kh = dk_ref[:, hs]
        vh = dv_ref[:, hs]
        qb = qh.astype(bf16)
        qk = lax.dot_general(qb, kh.astype(bf16), (((1,), (1,)), ((), ())), preferred_element_type=f32) * w_intra
        num = w_inter * jnp.dot(qb, c_ref[hd].astype(bf16), preferred_element_type=f32)
        num = num + jnp.dot(qk.astype(bf16), vh.astype(bf16), preferred_element_type=f32)
        den = w_inter * jnp.sum(qh * n_ref[hd:hd + 1, :], axis=-1, keepdims=True) + jnp.sum(qk, axis=-1, keepdims=True)
        hh = num / jnp.maximum(jnp.abs(den), jnp.exp(-mt))
        mnew = mt[CHUNK - 1:CHUNK, :]
        blast = bcol[CHUNK - 1:CHUNK, :]
        wk = jnp.exp(blast - bcol + icol - mnew)
        decay = jnp.exp(blast + mprev - mnew)
        wkk = wk * kh
        c_ref[hd] = decay * c_ref[hd] + lax.dot_general(wkk.astype(bf16), vh.astype(bf16), (((0,), (0,)), ((), ())),
                                                        preferred_element_type=f32)
        n_ref[hd:hd + 1, :] = decay * n_ref[hd:hd + 1, :] + jnp.sum(wkk, axis=0, keepdims=True)
        m_ref[hd:hd + 1, :] = jnp.broadcast_to(mnew, (1, LANE))
        o_ref[:, hs] = (jax.nn.sigmoid(do_ref[:, hs]) * _rms(hh, nw[:, hs])).astype(bf16)

    @pl.when(c == NCHUNK - 1)
    def _():
        c_out[0] = c_ref[...]
        n_out[0] = n_ref[...]
        m_out[0] = m_ref[...]


def _prompt_d(h, par, nw):
    return pl.pallas_call(
        _pd_body,
        out_shape=(jax.ShapeDtypeStruct((N_PROMPT, BW), bf16),
                   jax.ShapeDtypeStruct((BATCH, D_HEADS, LANE, LANE), f32),
                   jax.ShapeDtypeStruct((BATCH, 8, LANE), f32),
                   jax.ShapeDtypeStruct((BATCH, 8, LANE), f32)),
        grid=(BATCH, NCHUNK),
        in_specs=[_h_spec(CB_DQ), _h_spec(CB_DK), _h_spec(CB_DV), _h_spec(CB_DO), _SMALL_SPEC,
                  _full_spec((8, LANE)), _full_spec((1, BW))],
        out_specs=(_BR_SPEC,
                   pl.BlockSpec((1, D_HEADS, LANE, LANE), lambda b, c: (b, 0, 0, 0)),
                   pl.BlockSpec((1, 8, LANE), lambda b, c: (b, 0, 0)),
                   pl.BlockSpec((1, 8, LANE), lambda b, c: (b, 0, 0))),
        scratch_shapes=[pltpu.VMEM((D_HEADS, LANE, LANE), f32), pltpu.VMEM((8, LANE), f32), pltpu.VMEM((8, LANE), f32)],
        compiler_params=_cparams(("parallel", "arbitrary")),
        name="prompt_mlstm",
    )(h, h, h, h, h, par, nw)


def _sample_body(h_ref, hg_ref, ssm_ref, cv_ref, mc_ref, mn_ref, mm_ref,
                 lnw_ref, lnb_ref, ws8_ref, bs8_ref, lb_ref, bnw_ref,
                 cw_ref, cb_ref, cpar_ref, cnw_ref, dpar_ref, dnw_ref,
                 oa_ref, ob_ref, oc_ref, od_ref, chv_ref,
                 hg_out, ssm_out, cv_out, mc_out, mn_out, mm_out):
    row = lax.broadcasted_iota(i32, (8, 1), 0)
    first = row < DEC_SEQ
    tpos = row & (DEC_SEQ - 1)
    eye = lax.broadcasted_iota(i32, (LANE, LANE), 0) == lax.broadcasted_iota(i32, (LANE, LANE), 1)
    lane = lax.broadcasted_iota(i32, (1, LANE), 1)
    lo = lane < C_HEADDIM

    def tile(p, carry):
        r = pl.multiple_of(p * 8, 8)

        def col(blk, width=BW):
            return h_ref[pl.ds(r, 8), blk * BW:blk * BW + width]

        small = h_ref[pl.ds(r, 8), SMALL_COL:SMALL_COL + LANE]

        u, v = _a_uv(col(CB_AU), col(CB_AV), lnw_ref[...], lnb_ref[...])
        chv_ref[pl.ds(r, 8), :] = v
        sp = bs8_ref[...]
        for s in range(8):
            sp = sp + ws8_ref[s] * v[s:s + 1, :]
        oa_ref[pl.ds(r, 8), :] = (u * sp).astype(bf16)

        q, kb, logf = _b_pre(col(CB_BQ), col(CB_BF), lb_ref[...])
        bi = col(CB_BI)
        bg = col(CB_BG)
        for hd in range(B_HEADS):
            hs = slice(hd * LANE, (hd + 1) * LANE)
            outs = []
            for which in range(2):
                sq = 2 * p + which
                valid = first if which == 0 else jnp.logical_not(first)
                o, s_new = _hgrn_chunk(q[:, hs], kb[:, hs], bi[:, hs], logf[:, hs], hg_ref[sq, hd], 8, eye, valid)
                hg_out[sq, hd] = s_new
                outs.append(o)
            o = jnp.where(first, outs[0], outs[1])
            ob_ref[pl.ds(r, 8), hs] = _b_post(o, bg[:, hs], bnw_ref[...]).astype(bf16)

        x = h_ref[pl.ds(r, 8), CB_CX0 * BW:CB_CX0 * BW + C_CONV_DIM]
        bufs = [jnp.where(first, cv_ref[2 * p, k:k + 1, :], cv_ref[2 * p + 1, k:k + 1, :]) for k in range(3)]
        r1 = pltpu.roll(x, 1, 0)
        r2 = pltpu.roll(x, 2, 0)
        r3 = pltpu.roll(x, 3, 0)
        sh1 = jnp.where(tpos >= 1, r1, bufs[2])
        sh2 = jnp.where(tpos >= 2, r2, jnp.where(tpos == 0, bufs[1], bufs[2]))
        sh3 = jnp.where(tpos >= 3, r3, jnp.where(tpos == 0, bufs[0], jnp.where(tpos == 1, bufs[1], bufs[2])))
        xbc = _c_conv_silu(sh3, sh2, sh1, x, cw_ref, cb_ref)
        cv_out[2 * p] = pltpu.roll(x, 7, 0)[0:3, :]
        cv_out[2 * p + 1] = r3[0:3, :]
        xs = xbc[:, 0:BW]
        bm = xbc[:, BW:BW + 2 * C_DSTATE]
        cm = xbc[:, BW + 2 * C_DSTATE:]
        cpar = cpar_ref[...]
        dt = jax.nn.softplus(small + cpar[0:1, :])
        gl_all = dt * (-jnp.exp(cpar[1:2, :]))
        ys = [None] * (C_HEADS // 2)
        for which in range(2):
            sq = 2 * p + which
            valid = first if which == 0 else jnp.logical_not(first)
            g = _cumsum_rows_small(jnp.where(valid, gl_all, 0.0), 8)
            dtv = jnp.where(valid, dt, 0.0)
            dots = []
            for grp in range(C_GROUPS):
                cg = cm[:, grp * C_DSTATE:(grp + 1) * C_DSTATE]
                bgp = bm[:, grp * C_DSTATE:(grp + 1) * C_DSTATE]
                dots.append([jnp.sum(cg * bgp[s:s + 1, :], axis=-1, keepdims=True) for s in range(8)])
            for pr in range(C_HEADS // 2):
                xp = xs[:, pr * LANE:(pr + 1) * LANE]
                sp_lo = ssm_ref[sq, 2 * pr]
                sp_hi = ssm_ref[sq, 2 * pr + 1]
                spair = jnp.concatenate([sp_lo, sp_hi], axis=-1)
                y = jnp.zeros((8, LANE), f32)
                upd = jnp.zeros((C_DSTATE, LANE), f32)
                dl = jnp.zeros((1, LANE), f32)
                cdl = jnp.zeros((1, LANE), f32)
                for sub in range(2):
                    hd = 2 * pr + sub
                    grp = hd // (C_HEADS // C_GROUPS)
                    lm = lo if sub == 0 else jnp.logical_not(lo)
                    gcol = g[:, hd:hd + 1]
                    g_last = gcol[7:8, :]
                    xm = jnp.where(lm, xp, 0.0)
                    for s in range(8):
                        coef = dots[grp][s] * jnp.where(row >= s, jnp.exp(jnp.where(row >= s, gcol - gcol[s:s + 1, :], 0.0))
                                                        * dtv[s:s + 1, hd:hd + 1], 0.0)
                        y = y + coef * xm[s:s + 1, :]
                    cg = cm[:, grp * C_DSTATE:(grp + 1) * C_DSTATE]
                    bgp = bm[:, grp * C_DSTATE:(grp + 1) * C_DSTATE]
                    y = y + jnp.dot((cg * jnp.exp(gcol)).astype(bf16), jnp.where(lm, spair, 0.0).astype(bf16),
                                    preferred_element_type=f32)
                    kd = bgp * (dtv[:, hd:hd + 1] * jnp.exp(g_last - gcol))
                    upd = upd + lax.dot_general(kd.astype(bf16), xm.astype(bf16), (((0,), (0,)), ((), ())),
                                                preferred_element_type=f32)
                    dl = jnp.where(lm, jnp.exp(g_last), dl)
                    cdl = jnp.where(lm, cpar[2:3, hd:hd + 1], cdl)
                snew = dl * spair + upd
                ssm_out[sq, 2 * pr] = snew[:, 0:C_HEADDIM]
                ssm_out[sq, 2 * pr + 1] = snew[:, C_HEADDIM:]
                y = y + cdl * xp
                ys[pr] = y if which == 0 else jnp.where(first, ys[pr], y)
        yc = jnp.concatenate(ys, axis=-1)
        oc_ref[pl.ds(r, 8), :] = _c_post(yc, col(CB_CZ), cnw_ref[...]).astype(bf16)

        smd = small + dpar_ref[0:1, :]
        lsd = jax.nn.log_sigmoid(smd)
        dq = col(CB_DQ)
        dk = col(CB_DK)
        dv = col(CB_DV)
        do = col(CB_DO)
        dnw = dnw_ref[...]
        hs_out = [None] * D_HEADS
        for which in range(2):
            sq = 2 * p + which
            valid = first if which == 0 else jnp.logical_not(first)
            bcs = _cumsum_rows_small(jnp.where(valid, lsd, 0.0), 8)
            igv = jnp.where(valid, smd, NEG_BIG)
            mrow = mm_ref[pl.ds(sq, 1), :]
            mnew_row = jnp.zeros((1, D_HEADS), f32)
            hlane = lax.broadcasted_iota(i32, (1, D_HEADS), 1)
            for hd in range(D_HEADS):
                hs = slice(hd * LANE, (hd + 1) * LANE)
                li, lf = 8 + hd, 12 + hd
                bcol = bcs[:, lf:lf + 1]
                icol = igv[:, li:li + 1]
                mprev = mrow[:, hd:hd + 1]
                dcols = [jnp.where(row >= s, bcol - bcol[s:s + 1, :] + icol[s:s + 1, :], NEG_BIG) for s in range(8)]
                mx = dcols[0]
                for s in range(1, 8):
                    mx = jnp.maximum(mx, dcols[s])
                inter = bcol + mprev
                mt = jnp.maximum(inter, mx)
                w_inter = jnp.exp(inter - mt)
                qh = dq[:, hs] * (D_DK ** -0.5)
                kh = dk[:, hs]
                vh = dv[:, hs]
                cmat = mc_ref[sq, hd]
                nrow = mn_ref[sq, pl.ds(hd, 1), :]
                num = w_inter * jnp.dot(qh.astype(bf16), cmat.astype(bf16), preferred_element_type=f32)
                den = w_inter * jnp.sum(qh * nrow, axis=-1, keepdims=True)
                for s in range(8):
                    w = jnp.where(row >= s, jnp.exp(dcols[s] - mt), 0.0)
                    qk = jnp.sum(qh * kh[s:s + 1, :], axis=-1, keepdims=True) * w
                    num = num + qk * vh[s:s + 1, :]
                    den = den + qk
                hh = num / jnp.maximum(jnp.abs(den), jnp.exp(-mt))
                mnew = mt[7:8, :]
                blast = bcol[7:8, :]
                wk = jnp.exp(blast - bcol + icol - mnew)
                decay = jnp.exp(blast + mprev - mnew)
                wkk = wk * kh
                mc_out[sq, hd] = decay * cmat + lax.dot_general(wkk.astype(bf16), vh.astype(bf16), (((0,), (0,)), ((), ())),
                                                                preferred_element_type=f32)
                mn_out[sq, pl.ds(hd, 1), :] = decay * nrow + jnp.sum(wkk, axis=0, keepdims=True)
                mnew_row = jnp.where(hlane == hd, mnew, mnew_row)
                o = jax.nn.sigmoid(do[:, hs]) * _rms(hh, dnw[:, hs])
                hs_out[hd] = o if which == 0 else jnp.where(first, hs_out[hd], o)
            mm_out[pl.ds(sq, 1), :] = mnew_row
        for hd in range(D_HEADS):
            od_ref[pl.ds(r, 8), hd * LANE:(hd + 1) * LANE] = hs_out[hd].astype(bf16)
        return carry

    lax.fori_loop(0, SB // 2, tile, 0)


def _sample_mixer(h, st_hgrn, st_ssm, st_conv, st_c, st_n, st_m, pa, pb, pc, pd):
    rows = SB * DEC_SEQ
    row0 = N_PROMPT // rows

    def blk(shape):
        nd = len(shape)
        return pl.BlockSpec((SB,) + shape, lambda i, nd=nd: (i,) + (0,) * nd)

    def full(shape):
        nd = len(shape)
        return pl.BlockSpec(shape, lambda i, nd=nd: (0,) * nd)

    state_specs = [blk((B_HEADS, LANE, LANE)), blk((C_HEADS, C_DSTATE, C_HEADDIM)), blk((C_CONV - 1, C_CONV_DIM)),
                   blk((D_HEADS, LANE, LANE)), blk((D_HEADS, LANE)), blk((D_HEADS,))]
    params = list(pa) + list(pb) + list(pc) + list(pd)
    br_spec = pl.BlockSpec((rows, BW), lambda i: (i, 0))
    br_shape = jax.ShapeDtypeStruct((N_SAMPLE, BW), bf16)
    return pl.pallas_call(
        _sample_body,
        out_shape=(br_shape, br_shape, br_shape, br_shape, jax.ShapeDtypeStruct((N_SAMPLE, BW), f32),
                   jax.ShapeDtypeStruct(st_hgrn.shape, f32), jax.ShapeDtypeStruct(st_ssm.shape, f32),
                   jax.ShapeDtypeStruct(st_conv.shape, f32), jax.ShapeDtypeStruct(st_c.shape, f32),
                   jax.ShapeDtypeStruct(st_n.shape, f32), jax.ShapeDtypeStruct(st_m.shape, f32)),
        grid=(DEC_BATCH // SB,),
        in_specs=[pl.BlockSpec((rows, MIX_W), lambda i: (row0 + i, 0))] + state_specs + [full(p.shape) for p in params],
        out_specs=(br_spec, br_spec, br_spec, br_spec, br_spec) + tuple(state_specs),
        compiler_params=_cparams(("parallel",)),
        name="sample_mixer",
    )(h, st_hgrn, st_ssm, st_conv, st_c, st_n, st_m, *params)


def _merge_body(x_ref, nw_ref, ba_ref, bb_ref, bc_ref, bd_ref, wg_ref, wb_ref, o_ref, xn_ref, acc_ref):
    jc = pl.program_id(1)
    n = pl.program_id(2)

    @pl.when((jc == 0) & (n == 0))
    def _():
        xn_ref[...] = _rms(x_ref[...], nw_ref[...]).astype(bf16)

    gate = jax.nn.sigmoid(jnp.dot(xn_ref[...], wg_ref[...], preferred_element_type=f32))
    for k, br_ref in enumerate((ba_ref, bb_ref, bc_ref, bd_ref)):
        @pl.when(n == k)
        def _(br_ref=br_ref, k=k):
            contrib = gate * jnp.dot(br_ref[...], wb_ref[0], preferred_element_type=f32)
            if k == 0:
                acc_ref[...] = contrib
            else:
                acc_ref[...] = acc_ref[...] + contrib

    @pl.when(n == N_BRANCH - 1)
    def _():
        o_ref[...] = acc_ref[...].astype(bf16)


def _merge(x, nw, brs, w_gate, w_branch, tm, tn):
    m = x.shape[0]
    ncol = D_MODEL // tn
    br_spec = pl.BlockSpec((tm, BW), lambda i, jc, n: (i, 0))
    return pl.pallas_call(
        _merge_body,
        out_shape=jax.ShapeDtypeStruct((m, D_MODEL), bf16),
        grid=(m // tm, ncol, N_BRANCH),
        in_specs=[pl.BlockSpec((tm, D_MODEL), lambda i, jc, n: (i, 0)),
                  pl.BlockSpec((1, D_MODEL), lambda i, jc, n: (0, 0)),
                  br_spec, br_spec, br_spec, br_spec,
                  pl.BlockSpec((D_MODEL, tn), lambda i, jc, n: (0, n * ncol + jc)),
                  pl.BlockSpec((1, BW, tn), lambda i, jc, n: (n, 0, jc))],
        out_specs=pl.BlockSpec((tm, tn), lambda i, jc, n: (i, jc)),
        scratch_shapes=[pltpu.VMEM((tm, D_MODEL), bf16), pltpu.VMEM((tm, tn), f32)],
        compiler_params=_cparams(("parallel", "arbitrary", "arbitrary")),
        name="merge",
    )(x, nw, *brs, w_gate, w_branch)


def _top16_rows(s):
    n = s.shape[0]
    rid = lax.broadcasted_iota(i32, s.shape, 0).astype(f32)
    out = []
    for _ in range(PEER_TOPK):
        m = jnp.max(s, axis=0, keepdims=True)
        am = jnp.min(jnp.where(s == m, rid, float(n)), axis=0, keepdims=True)
        out.append((m, am))
        s = jnp.where(rid == am, -jnp.inf, s)
    return out


def _collect16(pairs, tb):
    r16 = lax.broadcasted_iota(i32, (PEER_TOPK, tb), 0)
    v = jnp.zeros((PEER_TOPK, tb), f32)
    ix = jnp.zeros((PEER_TOPK, tb), f32)
    for k, (m, am) in enumerate(pairs):
        v = jnp.where(r16 == k, m, v)
        ix = jnp.where(r16 == k, am, ix)
    return v, ix.astype(i32)


def _peer_route_body(q_ref, keys_ref, ia_ref, ib_ref, gt_ref):
    tb = q_ref.shape[0]
    vals = []
    idxs = []
    for p in range(2):
        st = lax.dot_general(keys_ref[0, p], q_ref[:, p * LANE:(p + 1) * LANE], (((1,), (1,)), ((), ())),
                             preferred_element_type=f32)
        v, ix = _collect16(_top16_rows(st), tb)
        vals.append(v)
        idxs.append(ix)
    cand = jnp.concatenate([vals[0][ka:ka + 1, :] + vals[1] for ka in range(PEER_TOPK)], axis=0)
    fs, pos = _collect16(_top16_rows(cand), tb)
    ka = pos >> 4
    kb = pos & (PEER_TOPK - 1)
    i1 = jnp.zeros((PEER_TOPK, tb), i32)
    i2 = jnp.zeros((PEER_TOPK, tb), i32)
    for j in range(PEER_TOPK):
        i1 = jnp.where(ka == j, idxs[0][j:j + 1, :], i1)
        i2 = jnp.where(kb == j, idxs[1][j:j + 1, :], i2)
    pe = jnp.exp(fs - fs[0:1, :])
    ia_ref[...] = i1
    ib_ref[...] = i2
    gt_ref[...] = pe / jnp.sum(pe, axis=0, keepdims=True)


def _peer_route(q, keys, tb):
    m = q.shape[0]
    spec = pl.BlockSpec((PEER_TOPK, tb), lambda i, hd: (hd, i))
    nslot = PEER_HEADS * PEER_TOPK
    return pl.pallas_call(
        _peer_route_body,
        out_shape=(jax.ShapeDtypeStruct((nslot, m), i32), jax.ShapeDtypeStruct((nslot, m), i32),
                   jax.ShapeDtypeStruct((nslot, m), f32)),
        grid=(m // tb, PEER_HEADS),
        in_specs=[pl.BlockSpec((tb, 2 * LANE), lambda i, hd: (i, hd)),
                  pl.BlockSpec((1, 2, PEER_NKEYS, LANE), lambda i, hd: (hd, 0, 0, 0))],
        out_specs=(spec, spec, spec),
        compiler_params=_cparams(("parallel", "parallel")),
        name="peer_route",
    )(q, keys)


GATE_UNROLL = 8


def _peer_gates_body(ia_ref, ib_ref, gt_ref, o_ref, a_s, b_s, g_s):
    tg = ia_ref.shape[1]
    a_s[...] = ia_ref[...].astype(f32).T
    b_s[...] = ib_ref[...].astype(f32).T
    g_s[...] = gt_ref[...].T
    sub = lax.broadcasted_iota(i32, (LANE, LANE), 0).astype(f32)

    def step(r, carry):
        r8 = pl.multiple_of(r * GATE_UNROLL, GATE_UNROLL)
        a8 = a_s[pl.ds(r8, GATE_UNROLL), :]
        b8 = b_s[pl.ds(r8, GATE_UNROLL), :]
        g8 = g_s[pl.ds(r8, GATE_UNROLL), :]
        for u in range(GATE_UNROLL):
            at = jnp.where(sub == a8[u:u + 1, :], 1.0, 0.0).astype(bf16)
            bt = jnp.where(sub == b8[u:u + 1, :], g8[u:u + 1, :], 0.0).astype(bf16)
            gm = lax.dot_general(at, bt, (((1,), (1,)), ((), ())), preferred_element_type=f32)
            o_ref[r8 + u] = gm.astype(bf16)
        return carry

    lax.fori_loop(0, tg // GATE_UNROLL, step, 0)


def _peer_gates(ia, ib, gt, tg):
    nslot, m = ia.shape
    spec = pl.BlockSpec((nslot, tg), lambda i: (0, i))
    return pl.pallas_call(
        _peer_gates_body,
        out_shape=jax.ShapeDtypeStruct((m, PEER_NKEYS, PEER_NKEYS), bf16),
        grid=(m // tg,),
        in_specs=[spec, spec, spec],
        out_specs=pl.BlockSpec((tg, PEER_NKEYS, PEER_NKEYS), lambda i: (i, 0, 0)),
        scratch_shapes=[pltpu.VMEM((tg, nslot), f32)] * 3,
        compiler_params=_cparams(("parallel",)),
        name="peer_gates",
    )(ia, ib, gt)


def _peer_experts_body(x_ref, nw_ref, u_ref, v_ref, g_ref, o_ref, xn_ref):
    @pl.when(pl.program_id(1) == 0)
    def _():
        x = x_ref[...]
        xn_ref[...] = _rms(x, nw_ref[...]).astype(bf16)
        o_ref[...] = x

    hmat = lax.dot_general(xn_ref[...], u_ref[...], (((1,), (1,)), ((), ())), preferred_element_type=f32)
    w = (_gelu(hmat) * g_ref[...].astype(f32)).astype(bf16)
    o_ref[...] += jnp.dot(w, v_ref[...], preferred_element_type=f32)


def _peer_experts(x, nw, u, v, g, tb, eb):
    m = x.shape[0]
    return pl.pallas_call(
        _peer_experts_body,
        out_shape=jax.ShapeDtypeStruct((m, D_MODEL), f32),
        grid=(m // tb, PEER_N // eb),
        in_specs=[pl.BlockSpec((tb, D_MODEL), lambda i, j: (i, 0)),
                  pl.BlockSpec((1, D_MODEL), lambda i, j: (0, 0)),
                  pl.BlockSpec((eb, D_MODEL), lambda i, j: (j, 0)),
                  pl.BlockSpec((eb, D_MODEL), lambda i, j: (j, 0)),
                  pl.BlockSpec((tb, eb), lambda i, j: (i, j))],
        out_specs=pl.BlockSpec((tb, D_MODEL), lambda i, j: (i, 0)),
        scratch_shapes=[pltpu.VMEM((tb, D_MODEL), bf16)],
        compiler_params=_cparams(("parallel", "arbitrary")),
        name="peer_experts",
    )(x, nw, u, v, g)


def _prep_w_in(w):
    offs = np.cumsum((0,) + IN_SPLITS)
    parts = [w[:, offs[k]:offs[k + 1]] for k in range(len(IN_SPLITS))]
    (a_u, a_v, b_q, b_f, b_i, b_g, c_z, c_xbc, c_dt, d_q, d_k, d_v, d_o, d_ig, d_fg, gates) = parts
    small = jnp.concatenate([c_dt, d_ig, d_fg, jnp.zeros((D_MODEL, MIX_W - SMALL_COL - 16), w.dtype)], axis=1)
    w_mix = jnp.concatenate([a_u, a_v, b_q, b_f, b_i, b_g, c_z, c_xbc, d_q, d_k, d_v, d_o, small], axis=1)
    return w_mix.astype(bf16), gates.astype(bf16)


def _lane_row(vals, start):
    row = jnp.zeros((LANE,), f32)
    return row.at[start:start + vals.shape[0]].set(vals)


def kernel(x_prompt, x_sample, state_hgrn, state_ssm, state_conv, state_mlstm_c, state_mlstm_n, state_mlstm_m, norm1_w, w_in, a_ln_w, a_ln_b, a_ws, a_bs, b_lb_logits, b_norm_w, c_conv_w, c_conv_b, c_dt_bias, c_a_log, c_d, c_norm_w, d_ig_b, d_fg_b, d_norm_w, w_branch, w_out, norm2_w, peer_wq, peer_keys, peer_u, peer_v, final_norm_w):
    x = jnp.concatenate([x_prompt.reshape(N_PROMPT, D_MODEL), x_sample.reshape(N_SAMPLE, D_MODEL)], axis=0)
    lbs = jax.nn.softmax(b_lb_logits.astype(f32), axis=0)
    lbs = jnp.cumsum(lbs, axis=0) - lbs[0]
    zeros8 = jnp.zeros((8, LANE), f32)
    news_p = []
    news_s = []
    for l in range(DEPTH):
        w_mix, w_gate = _prep_w_in(w_in[l])
        h = _norm_mm(x, norm1_w[l][None, :], w_mix, f32, 1088, 512)

        lnw = a_ln_w[l][None, :]
        lnb = a_ln_b[l][None, :]
        lb = lbs[l][None, :]
        bnw = b_norm_w[l][None, :]
        cw = c_conv_w[l]
        cb = c_conv_b[l][None, :]
        cpar = zeros8.at[0].set(_lane_row(c_dt_bias[l], 0)).at[1].set(_lane_row(c_a_log[l], 0)).at[2].set(_lane_row(c_d[l], 0))
        cnw = c_norm_w[l][None, :]
        dpar = zeros8.at[0].set(_lane_row(d_ig_b[l], 8) + _lane_row(d_fg_b[l], 12))
        dnw = d_norm_w[l][None, :]

        bs_full = jnp.repeat(a_bs[l].T, LANE, axis=1)
        br_a = _prompt_a(h, lnw, lnb, a_ws[l], bs_full)
        br_b, hg_p = _prompt_b(h, lb, bnw)
        br_c, ssm_p, cv_p = _prompt_c(h, cw, cb, cpar, cnw)
        br_d, mc_p, mn_p, mm_p = _prompt_d(h, dpar, dnw)
        news_p.append((hg_p, ssm_p, cv_p, mc_p, mn_p[:, :D_HEADS, :], mm_p[:, :D_HEADS, 0]))

        w4 = jnp.tril(a_ws[l][:, :DEC_SEQ, :DEC_SEQ])
        w8 = jnp.zeros((A_GROUPS, 8, 8), f32).at[:, :4, :4].set(w4).at[:, 4:, 4:].set(w4)
        ws8 = jnp.repeat(jnp.transpose(w8, (2, 1, 0)), LANE, axis=2)
        bs8 = jnp.repeat(jnp.tile(a_bs[l][:, :DEC_SEQ], (1, 2)).T, LANE, axis=1)
        outs = _sample_mixer(h, state_hgrn[l], state_ssm[l], state_conv[l], state_mlstm_c[l], state_mlstm_n[l],
                             state_mlstm_m[l], (lnw, lnb, ws8, bs8), (lb, bnw), (cw, cb, cpar, cnw), (dpar, dnw))
        sa, sb, sc, sd, chv = outs[:5]
        news_s.append(tuple(outs[5:]) + (chv.reshape(DEC_BATCH, DEC_SEQ, BW),))

        brs = [jnp.concatenate([p, s], axis=0) for p, s in ((br_a, sa), (br_b, sb), (br_c, sc), (br_d, sd))]
        mixin = _merge(x, norm1_w[l][None, :], brs, w_gate, w_branch[l].astype(bf16), 544, 1024)
        x = _mm_res(mixin, w_out[l].astype(bf16), x, 1088, 512)

        q = _norm_mm(x, norm2_w[l][None, :], peer_wq[l].astype(bf16), bf16, 1088, 512)
        ia, ib, gt = _peer_route(q, peer_keys[l].astype(bf16), 512)
        g = _peer_gates(ia, ib, gt, 128).reshape(N_TOK, PEER_N)
        x = _peer_experts(x, norm2_w[l][None, :], peer_u[l].astype(bf16), peer_v[l].astype(bf16), g, 544, 512)

    y = _final_norm(x, final_norm_w[None, :], 544)
    y_prompt = y[:N_PROMPT].reshape(BATCH, SEQ, D_MODEL)
    y_sample = y[N_PROMPT:].reshape(DEC_BATCH, DEC_SEQ, D_MODEL)
    stack = lambda news, k: jnp.stack([n[k] for n in news], axis=0)
    return (y_prompt, y_sample) + tuple(stack(news_p, k) for k in range(6)) + tuple(stack(news_s, k) for k in range(7))
```

```python
import functools

import jax
import jax.numpy as jnp
import numpy as np
from jax import lax
from jax.experimental import pallas as pl
from jax.experimental.pallas import tpu as pltpu

f32 = jnp.float32
bf16 = jnp.bfloat16
i32 = jnp.int32
HI = lax.Precision.HIGHEST

D_MODEL = 2048
BATCH = 4
SEQ = 2048
DEPTH = 2
DEC_BATCH = 128
DEC_SEQ = 4
N_BRANCH = 4
BW = 512
A_GROUPS = 4
A_CHUNK = 128
B_HEADS = 4
LB_FLOOR = 1e-30
C_HEADDIM = 64
C_HEADS = 8
C_GROUPS = 2
C_DSTATE = 128
C_CONV = 4
C_CONV_DIM = 1024
D_HEADS = 4
D_DK = 128
NEG_BIG = -1e30
IN_SPLITS = (512, 512, 512, 512, 512, 512, 512, 1024, 8, 512, 512, 512, 512, 4, 4, 8192)
PEER_NKEYS = 128
PEER_N = PEER_NKEYS * PEER_NKEYS
PEER_HEADS = 8
PEER_TOPK = 16
EPS = 1e-6
INV_SQRT2 = 0.7071067811865476

N_PROMPT = BATCH * SEQ
N_SAMPLE = DEC_BATCH * DEC_SEQ
N_TOK = N_PROMPT + N_SAMPLE

CB_AU, CB_AV, CB_BQ, CB_BF, CB_BI, CB_BG, CB_CZ, CB_CX0, CB_CX1, CB_DQ, CB_DK, CB_DV, CB_DO = range(13)
SMALL_COL = 13 * 512
MIX_W = 14 * 512
LANE = 128
CHUNK = 128
NCHUNK = SEQ // CHUNK
SUB = 16
SB = 8
VMEM_LIMIT = 56 * 1024 * 1024


def _gelu(x):
    return 0.5 * x * (1.0 + lax.erf(x * INV_SQRT2))


def _rms(x, w):
    ms = jnp.mean(x * x, axis=-1, keepdims=True)
    return x * lax.rsqrt(ms + EPS) * w


def _tri(n):
    r = lax.broadcasted_iota(i32, (n, n), 0)
    c = lax.broadcasted_iota(i32, (n, n), 1)
    return r >= c


def _cumsum_rows_small(x, n):
    row = lax.broadcasted_iota(i32, (n, 1), 0)
    acc = jnp.zeros_like(x)
    for s in range(n):
        acc = acc + jnp.where(row >= s, x[s:s + 1, :], 0.0)
    return acc


def _row_to_col(row, eye):
    return jnp.sum(jnp.where(eye, row, 0.0), axis=-1, keepdims=True)


def _cparams(sem, vmem=VMEM_LIMIT):
    return pltpu.CompilerParams(dimension_semantics=sem, vmem_limit_bytes=vmem)


def _norm_mm_body(x_ref, nw_ref, w_ref, o_ref, xn_ref):
    @pl.when(pl.program_id(1) == 0)
    def _():
        xn_ref[...] = _rms(x_ref[...], nw_ref[...]).astype(bf16)

    o_ref[...] = jnp.dot(xn_ref[...], w_ref[...], preferred_element_type=f32).astype(o_ref.dtype)


def _norm_mm(x, nw, w, out_dtype, tm, tn):
    m, k = x.shape
    n = w.shape[1]
    return pl.pallas_call(
        _norm_mm_body,
        out_shape=jax.ShapeDtypeStruct((m, n), out_dtype),
        grid=(m // tm, n // tn),
        in_specs=[pl.BlockSpec((tm, k), lambda i, j: (i, 0)),
                  pl.BlockSpec((1, k), lambda i, j: (0, 0)),
                  pl.BlockSpec((k, tn), lambda i, j: (0, j))],
        out_specs=pl.BlockSpec((tm, tn), lambda i, j: (i, j)),
        scratch_shapes=[pltpu.VMEM((tm, k), bf16)],
        compiler_params=_cparams(("parallel", "arbitrary")),
        name="norm_mm",
    )(x, nw, w)


def _mm_res_body(a_ref, w_ref, r_ref, o_ref):
    o_ref[...] = r_ref[...] + jnp.dot(a_ref[...], w_ref[...], preferred_element_type=f32)


def _mm_res(a, w, res, tm, tn):
    m, k = a.shape
    n = w.shape[1]
    return pl.pallas_call(
        _mm_res_body,
        out_shape=jax.ShapeDtypeStruct((m, n), f32),
        grid=(m // tm, n // tn),
        in_specs=[pl.BlockSpec((tm, k), lambda i, j: (i, 0)),
                  pl.BlockSpec((k, tn), lambda i, j: (0, j)),
                  pl.BlockSpec((tm, tn), lambda i, j: (i, j))],
        out_specs=pl.BlockSpec((tm, tn), lambda i, j: (i, j)),
        compiler_params=_cparams(("parallel", "arbitrary")),
        name="mm_res",
    )(a, w, res)


def _final_norm_body(x_ref, w_ref, op_ref, os_ref):
    y = _rms(x_ref[...], w_ref[...])
    i = pl.program_id(0)

    @pl.when(i < N_PROMPT // N_SAMPLE)
    def _():
        op_ref[...] = y

    @pl.when(i == N_PROMPT // N_SAMPLE)
    def _():
        os_ref[...] = y


def _final_norm(x, w):
    m, k = x.shape
    tm = N_SAMPLE
    last_p = N_PROMPT // tm - 1
    return pl.pallas_call(
        _final_norm_body,
        out_shape=(jax.ShapeDtypeStruct((N_PROMPT, k), f32), jax.ShapeDtypeStruct((N_SAMPLE, k), f32)),
        grid=(m // tm,),
        in_specs=[pl.BlockSpec((tm, k), lambda i: (i, 0)), pl.BlockSpec((1, k), lambda i: (0, 0))],
        out_specs=(pl.BlockSpec((tm, k), lambda i: (jnp.minimum(i, last_p), 0)),
                   pl.BlockSpec((tm, k), lambda i: (0, 0))),
        compiler_params=_cparams(("arbitrary",)),
        name="final_norm",
    )(x, w)


def _a_uv(au, av, lnw, lnb):
    u = _gelu(au)
    g = _gelu(av)
    xc = g - jnp.mean(g, axis=-1, keepdims=True)
    var = jnp.mean(xc * xc, axis=-1, keepdims=True)
    v = xc * lax.rsqrt(var + EPS) * lnw + lnb
    return u, v


def _b_pre(bq, bf_, lb):
    q = bq * jax.nn.sigmoid(bq)
    logf = jnp.logaddexp(jnp.log(jnp.maximum(lb, LB_FLOOR)), jnp.log1p(-lb) + jax.nn.log_sigmoid(bf_))
    kb = (1.0 - lb) * jax.nn.sigmoid(-bf_)
    return q, kb, logf


def _hgrn_chunk(q, k, v, gl, s_mat, n, eye, valid=None):
    if valid is not None:
        gl = jnp.where(valid, gl, 0.0)
        k = jnp.where(valid, k, 0.0)
    g = _cumsum_rows_small(gl, n)
    g_last = g[n - 1:n, :]
    o = jnp.dot((q * jnp.exp(g)).astype(bf16), s_mat.astype(bf16), preferred_element_type=f32)
    row = lax.broadcasted_iota(i32, (n, 1), 0)
    for s in range(n):
        m = row >= s
        d = jnp.where(m, g - g[s:s + 1, :], 0.0)
        p = jnp.where(m, q * k[s:s + 1, :] * jnp.exp(d), 0.0)
        o = o + jnp.sum(p, axis=-1, keepdims=True) * v[s:s + 1, :]
    k_dec = k * jnp.exp(g_last - g)
    upd = lax.dot_general(k_dec.astype(bf16), v.astype(bf16), (((0,), (0,)), ((), ())), preferred_element_type=f32)
    s_new = jnp.exp(_row_to_col(g_last, eye)) * s_mat + upd
    return o, s_new


def _b_post(o, bg, nw):
    return _rms(o, nw) * (bg * jax.nn.sigmoid(bg))


def _hrow(b, c):
    return b * NCHUNK + c


def _h_spec(colblk):
    return pl.BlockSpec((CHUNK, BW), lambda b, c, cb=colblk: (_hrow(b, c), cb))


_SMALL_SPEC = pl.BlockSpec((CHUNK, LANE), lambda b, c: (_hrow(b, c), SMALL_COL // LANE))
_BR_SPEC = pl.BlockSpec((CHUNK, BW), lambda b, c: (_hrow(b, c), 0))


def _full_spec(shape):
    nd = len(shape)
    return pl.BlockSpec(shape, lambda b, c, nd=nd: (0,) * nd)


def _pa_body(au_ref, av_ref, lnw_ref, lnb_ref, ws_ref, bs_ref, o_ref):
    u, v = _a_uv(au_ref[...], av_ref[...], lnw_ref[...], lnb_ref[...])
    tri = _tri(CHUNK)
    vb = v.astype(bf16)
    parts = []
    for g in range(A_GROUPS):
        w = jnp.where(tri, ws_ref[g], 0.0).astype(bf16)
        parts.append(jnp.dot(w, vb[:, g * LANE:(g + 1) * LANE], preferred_element_type=f32))
    sp = jnp.concatenate(parts, axis=-1) + bs_ref[...]
    o_ref[...] = (u * sp).astype(bf16)


def _prompt_a(h, lnw, lnb, ws, bs_full):
    return pl.pallas_call(
        _pa_body,
        out_shape=jax.ShapeDtypeStruct((N_PROMPT, BW), bf16),
        grid=(BATCH, NCHUNK),
        in_specs=[_h_spec(CB_AU), _h_spec(CB_AV), _full_spec((1, BW)), _full_spec((1, BW)),
                  _full_spec((A_GROUPS, A_CHUNK, A_CHUNK)), _full_spec((A_CHUNK, BW))],
        out_specs=_BR_SPEC,
        compiler_params=_cparams(("parallel", "parallel")),
        name="prompt_gmlp",
    )(h, h, lnw, lnb, ws, bs_full)


def _pb_body(bq_ref, bf_ref, bi_ref, bg_ref, lb_ref, nw_ref, o_ref, st_ref, s_ref, q_s, k_s, v_s, g_s, o_s):
    c = pl.program_id(1)

    @pl.when(c == 0)
    def _():
        s_ref[...] = jnp.zeros_like(s_ref)

    q, kb, logf = _b_pre(bq_ref[...], bf_ref[...], lb_ref[...])
    q_s[...] = q
    k_s[...] = kb
    v_s[...] = bi_ref[...]
    g_s[...] = logf
    eye = lax.broadcasted_iota(i32, (LANE, LANE), 0) == lax.broadcasted_iota(i32, (LANE, LANE), 1)
    def sub(j, carry):
        r = pl.multiple_of(j * SUB, SUB)
        for hd in range(B_HEADS):
            hs = slice(hd * LANE, (hd + 1) * LANE)
            o, s_new = _hgrn_chunk(q_s[pl.ds(r, SUB), hs], k_s[pl.ds(r, SUB), hs], v_s[pl.ds(r, SUB), hs],
                                   g_s[pl.ds(r, SUB), hs], s_ref[hd], SUB, eye)
            o_s[pl.ds(r, SUB), hs] = o
            s_ref[hd] = s_new
        return carry

    lax.fori_loop(0, CHUNK // SUB, sub, 0)
    bg = bg_ref[...]
    nw = nw_ref[...]
    for hd in range(B_HEADS):
        hs = slice(hd * LANE, (hd + 1) * LANE)
        o_ref[:, hs] = _b_post(o_s[:, hs], bg[:, hs], nw).astype(bf16)

    @pl.when(c == NCHUNK - 1)
    def _():
        st_ref[0] = s_ref[...]


def _prompt_b(h, lb, nw):
    return pl.pallas_call(
        _pb_body,
        out_shape=(jax.ShapeDtypeStruct((N_PROMPT, BW), bf16),
                   jax.ShapeDtypeStruct((BATCH, B_HEADS, LANE, LANE), f32)),
        grid=(BATCH, NCHUNK),
        in_specs=[_h_spec(CB_BQ), _h_spec(CB_BF), _h_spec(CB_BI), _h_spec(CB_BG),
                  _full_spec((1, BW)), _full_spec((1, LANE))],
        out_specs=(_BR_SPEC, pl.BlockSpec((1, B_HEADS, LANE, LANE), lambda b, c: (b, 0, 0, 0))),
        scratch_shapes=[pltpu.VMEM((B_HEADS, LANE, LANE), f32)] + [pltpu.VMEM((CHUNK, BW), f32)] * 5,
        compiler_params=_cparams(("parallel", "arbitrary")),
        name="prompt_hgrn",
    )(h, h, h, h, lb, nw)


def _c_conv_silu(win0, win1, win2, win3, cw_ref, cb_ref):
    y = cb_ref[...] + win0 * cw_ref[0:1, :] + win1 * cw_ref[1:2, :] + win2 * cw_ref[2:3, :] + win3 * cw_ref[3:4, :]
    return y * jax.nn.sigmoid(y)


def _c_post(yc, cz, nw):
    y = yc * (cz * jax.nn.sigmoid(cz))
    gw = BW // C_GROUPS
    parts = [_rms(y[:, g * gw:(g + 1) * gw], nw[:, g * gw:(g + 1) * gw]) for g in range(C_GROUPS)]
    return jnp.concatenate(parts, axis=-1)


def _pc_body(cz_ref, cx0_ref, cx1_ref, sm_ref, cw_ref, cb_ref, par_ref, nw_ref,
             o_ref, st_ref, cv_ref, xpad, sp_ref):
    c = pl.program_id(1)

    @pl.when(c == 0)
    def _():
        xpad[pl.ds(0, 8), :] = jnp.zeros((8, C_CONV_DIM), f32)
        sp_ref[...] = jnp.zeros_like(sp_ref)

    xpad[pl.ds(8, CHUNK), 0:BW] = cx0_ref[...]
    xpad[pl.ds(8, CHUNK), BW:2 * BW] = cx1_ref[...]
    xbc = _c_conv_silu(xpad[pl.ds(5, CHUNK), :], xpad[pl.ds(6, CHUNK), :], xpad[pl.ds(7, CHUNK), :],
                       xpad[pl.ds(8, CHUNK), :], cw_ref, cb_ref)

    @pl.when(c == NCHUNK - 1)
    def _():
        cv_ref[0] = xpad[pl.ds(CHUNK + 5, 3), :]

    xpad[pl.ds(0, 8), :] = xpad[pl.ds(CHUNK, 8), :]

    xs = xbc[:, 0:BW]
    bm = xbc[:, BW:BW + 2 * C_DSTATE]
    cm = xbc[:, BW + 2 * C_DSTATE:]
    par = par_ref[...]
    dt = jax.nn.softplus(sm_ref[...] + par[0:1, :])
    gl = dt * (-jnp.exp(par[1:2, :]))
    tri = _tri(CHUNK)
    g = jnp.dot(tri.astype(f32), gl, precision=HI, preferred_element_type=f32)
    gt = g.T
    dtt = dt.T
    lane = lax.broadcasted_iota(i32, (1, LANE), 1)
    lo = lane < C_HEADDIM
    cbs = []
    for grp in range(C_GROUPS):
        cg = cm[:, grp * C_DSTATE:(grp + 1) * C_DSTATE].astype(bf16)
        bg = bm[:, grp * C_DSTATE:(grp + 1) * C_DSTATE].astype(bf16)
        cbs.append(lax.dot_general(cg, bg, (((1,), (1,)), ((), ())), preferred_element_type=f32))
    ys = []
    for pr in range(C_HEADS // 2):
        xp = xs[:, pr * LANE:(pr + 1) * LANE]
        sp = sp_ref[pr]
        y = jnp.zeros((CHUNK, LANE), f32)
        upd = jnp.zeros((C_DSTATE, LANE), f32)
        dl = jnp.zeros((1, LANE), f32)
        cdl = jnp.zeros((1, LANE), f32)
        for sub in range(2):
            hd = 2 * pr + sub
            grp = hd // (C_HEADS // C_GROUPS)
            lm = lo if sub == 0 else jnp.logical_not(lo)
            col = g[:, hd:hd + 1]
            g_last = col[CHUNK - 1:CHUNK, :]
            dec = jnp.exp(jnp.where(tri, col - gt[hd:hd + 1, :], 0.0))
            sc = jnp.where(tri, cbs[grp] * dec * dtt[hd:hd + 1, :], 0.0)
            xm = jnp.where(lm, xp, 0.0).astype(bf16)
            cg = cm[:, grp * C_DSTATE:(grp + 1) * C_DSTATE]
            bg = bm[:, grp * C_DSTATE:(grp + 1) * C_DSTATE]
            y = y + jnp.dot(sc.astype(bf16), xm, preferred_element_type=f32)
            y = y + jnp.dot((cg * jnp.exp(col)).astype(bf16), jnp.where(lm, sp, 0.0).astype(bf16),
                            preferred_element_type=f32)
            kd = bg * (dt[:, hd:hd + 1] * jnp.exp(g_last - col))
            upd = upd + lax.dot_general(kd.astype(bf16), xm, (((0,), (0,)), ((), ())), preferred_element_type=f32)
            dl = jnp.where(lm, jnp.exp(g_last), dl)
            cdl = jnp.where(lm, par[2:3, hd:hd + 1], cdl)
        sp_ref[pr] = dl * sp + upd
        ys.append(y + cdl * xp)
    yc = jnp.concatenate(ys, axis=-1)
    o_ref[...] = _c_post(yc, cz_ref[...], nw_ref[...]).astype(bf16)

    @pl.when(c == NCHUNK - 1)
    def _():
        for pr in range(C_HEADS // 2):
            st_ref[0, 2 * pr] = sp_ref[pr][:, 0:C_HEADDIM]
            st_ref[0, 2 * pr + 1] = sp_ref[pr][:, C_HEADDIM:]


def _prompt_c(h, cw, cb, par, nw):
    return pl.pallas_call(
        _pc_body,
        out_shape=(jax.ShapeDtypeStruct((N_PROMPT, BW), bf16),
                   jax.ShapeDtypeStruct((BATCH, C_HEADS, C_DSTATE, C_HEADDIM), f32),
                   jax.ShapeDtypeStruct((BATCH, C_CONV - 1, C_CONV_DIM), f32)),
        grid=(BATCH, NCHUNK),
        in_specs=[_h_spec(CB_CZ), _h_spec(CB_CX0), _h_spec(CB_CX1), _SMALL_SPEC,
                  _full_spec((C_CONV, C_CONV_DIM)), _full_spec((1, C_CONV_DIM)), _full_spec((8, LANE)),
                  _full_spec((1, BW))],
        out_specs=(_BR_SPEC,
                   pl.BlockSpec((1, C_HEADS, C_DSTATE, C_HEADDIM), lambda b, c: (b, 0, 0, 0)),
                   pl.BlockSpec((1, C_CONV - 1, C_CONV_DIM), lambda b, c: (b, 0, 0))),
        scratch_shapes=[pltpu.VMEM((CHUNK + 8, C_CONV_DIM), f32), pltpu.VMEM((C_HEADS // 2, C_DSTATE, LANE), f32)],
        compiler_params=_cparams(("parallel", "arbitrary")),
        name="prompt_ssd",
    )(h, h, h, h, cw, cb, par, nw)


def _pd_body(dq_ref, dk_ref, dv_ref, do_ref, sm_ref, par_ref, nw_ref,
             o_ref, c_out, n_out, m_out, c_ref, n_ref, m_ref):
    c = pl.program_id(1)

    @pl.when(c == 0)
    def _():
        c_ref[...] = jnp.zeros_like(c_ref)
        n_ref[...] = jnp.zeros_like(n_ref)
        m_ref[...] = jnp.zeros_like(m_ref)

    sm = sm_ref[...] + par_ref[0:1, :]
    ls = jax.nn.log_sigmoid(sm)
    tri = _tri(CHUNK)
    bc = jnp.dot(tri.astype(f32), ls, precision=HI, preferred_element_type=f32)
    bct = bc.T
    smt = sm.T
    nw = nw_ref[...]
    for hd in range(D_HEADS):
        hs = slice(hd * LANE, (hd + 1) * LANE)
        li, lf = 8 + hd, 12 + hd
        bcol = bc[:, lf:lf + 1]
        brow = bct[lf:lf + 1, :]
        irow = smt[li:li + 1, :]
        icol = sm[:, li:li + 1]
        mprev = m_ref[hd:hd + 1, 0:1]
        dmat = jnp.where(tri, bcol - brow + irow, NEG_BIG)
        inter = bcol + mprev
        mt = jnp.maximum(inter, jnp.max(dmat, axis=-1, keepdims=True))
        w_intra = jnp.where(tri, jnp.exp(dmat - mt), 0.0)
        w_inter = jnp.exp(inter - mt)
        qh = dq_ref[:, hs] * (D_DK ** -0.5)
        kh = dk_ref[:, hs]
        vh = dv_ref[:, hs]
        qb = qh.astype(bf16)
        qk = lax.dot_general(qb, kh.astype(bf16), (((1,), (1,)), ((), ())), preferred_element_type=f32) * w_intra
        num = w_inter * jnp.dot(qb, c_ref[hd].astype(bf16), preferred_element_type=f32)
        num = num + jnp.dot(qk.astype(bf16), vh.astype(bf16), preferred_element_type=f32)
        den = w_inter * jnp.sum(qh * n_ref[hd:hd + 1, :], axis=-1, keepdims=True) + jnp.sum(qk, axis=-1, keepdims=True)
        hh = num / jnp.maximum(jnp.abs(den), jnp.exp(-mt))
        mnew = mt[CHUNK - 1:CHUNK, :]
        blast = bcol[CHUNK - 1:CHUNK, :]
        wk = jnp.exp(blast - bcol + icol - mnew)
        decay = jnp.exp(blast + mprev - mnew)
        wkk = wk * kh
        c_ref[hd] = decay * c_ref[hd] + lax.dot_general(wkk.astype(bf16), vh.astype(bf16), (((0,), (0,)), ((), ())),
                                                        preferred_element_type=f32)
        n_ref[hd:hd + 1, :] = decay * n_ref[hd:hd + 1, :] + jnp.sum(wkk, axis=0, keepdims=True)
        m_ref[hd:hd + 1, :] = jnp.broadcast_to(mnew, (1, LANE))
        o_ref[:, hs] = (jax.nn.sigmoid(do_ref[:, hs]) * _rms(hh, nw[:, hs])).astype(bf16)

    @pl.when(c == NCHUNK - 1)
    def _():
        c_out[0] = c_ref[...]
        n_out[0] = n_ref[...]
        m_out[0] = m_ref[...]


def _prompt_d(h, par, nw):
    return pl.pallas_call(
        _pd_body,
        out_shape=(jax.ShapeDtypeStruct((N_PROMPT, BW), bf16),
                   jax.ShapeDtypeStruct((BATCH, D_HEADS, LANE, LANE), f32),
                   jax.ShapeDtypeStruct((BATCH, 8, LANE), f32),
                   jax.ShapeDtypeStruct((BATCH, 8, LANE), f32)),
        grid=(BATCH, NCHUNK),
        in_specs=[_h_spec(CB_DQ), _h_spec(CB_DK), _h_spec(CB_DV), _h_spec(CB_DO), _SMALL_SPEC,
                  _full_spec((8, LANE)), _full_spec((1, BW))],
        out_specs=(_BR_SPEC,
                   pl.BlockSpec((1, D_HEADS, LANE, LANE), lambda b, c: (b, 0, 0, 0)),
                   pl.BlockSpec((1, 8, LANE), lambda b, c: (b, 0, 0)),
                   pl.BlockSpec((1, 8, LANE), lambda b, c: (b, 0, 0))),
        scratch_shapes=[pltpu.VMEM((D_HEADS, LANE, LANE), f32), pltpu.VMEM((8, LANE), f32), pltpu.VMEM((8, LANE), f32)],
        compiler_params=_cparams(("parallel", "arbitrary")),
        name="prompt_mlstm",
    )(h, h, h, h, h, par, nw)


def _sample_body(h_ref, hg_ref, ssm_ref, cv_ref, mc_ref, mn_ref, mm_ref,
                 lnw_ref, lnb_ref, ws8_ref, bs8_ref, lb_ref, bnw_ref,
                 cw_ref, cb_ref, cpar_ref, cnw_ref, dpar_ref, dnw_ref,
                 oa_ref, ob_ref, oc_ref, od_ref, chv_ref,
                 hg_out, ssm_out, cv_out, mc_out, mn_out, mm_out):
    row = lax.broadcasted_iota(i32, (8, 1), 0)
    first = row < DEC_SEQ
    tpos = row & (DEC_SEQ - 1)
    eye = lax.broadcasted_iota(i32, (LANE, LANE), 0) == lax.broadcasted_iota(i32, (LANE, LANE), 1)
    lane = lax.broadcasted_iota(i32, (1, LANE), 1)
    lo = lane < C_HEADDIM

    def tile(p, carry):
        r = pl.multiple_of(p * 8, 8)

        def col(blk, width=BW):
            return h_ref[pl.ds(r, 8), blk * BW:blk * BW + width]

        small = h_ref[pl.ds(r, 8), SMALL_COL:SMALL_COL + LANE]

        u, v = _a_uv(col(CB_AU), col(CB_AV), lnw_ref[...], lnb_ref[...])
        chv_ref[pl.ds(r, 8), :] = v
        sp = bs8_ref[...]
        for s in range(8):
            sp = sp + ws8_ref[s] * v[s:s + 1, :]
        oa_ref[pl.ds(r, 8), :] = (u * sp).astype(bf16)

        q, kb, logf = _b_pre(col(CB_BQ), col(CB_BF), lb_ref[...])
        bi = col(CB_BI)
        bg = col(CB_BG)
        for hd in range(B_HEADS):
            hs = slice(hd * LANE, (hd + 1) * LANE)
            outs = []
            for which in range(2):
                sq = 2 * p + which
                valid = first if which == 0 else jnp.logical_not(first)
                o, s_new = _hgrn_chunk(q[:, hs], kb[:, hs], bi[:, hs], logf[:, hs], hg_ref[sq, hd], 8, eye, valid)
                hg_out[sq, hd] = s_new
                outs.append(o)
            o = jnp.where(first, outs[0], outs[1])
            ob_ref[pl.ds(r, 8), hs] = _b_post(o, bg[:, hs], bnw_ref[...]).astype(bf16)

        x = h_ref[pl.ds(r, 8), CB_CX0 * BW:CB_CX0 * BW + C_CONV_DIM]
        bufs = [jnp.where(first, cv_ref[2 * p, k:k + 1, :], cv_ref[2 * p + 1, k:k + 1, :]) for k in range(3)]
        r1 = pltpu.roll(x, 1, 0)
        r2 = pltpu.roll(x, 2, 0)
        r3 = pltpu.roll(x, 3, 0)
        sh1 = jnp.where(tpos >= 1, r1, bufs[2])
        sh2 = jnp.where(tpos >= 2, r2, jnp.where(tpos == 0, bufs[1], bufs[2]))
        sh3 = jnp.where(tpos >= 3, r3, jnp.where(tpos == 0, bufs[0], jnp.where(tpos == 1, bufs[1], bufs[2])))
        xbc = _c_conv_silu(sh3, sh2, sh1, x, cw_ref, cb_ref)
        cv_out[2 * p] = pltpu.roll(x, 7, 0)[0:3, :]
        cv_out[2 * p + 1] = r3[0:3, :]
        xs = xbc[:, 0:BW]
        bm = xbc[:, BW:BW + 2 * C_DSTATE]
        cm = xbc[:, BW + 2 * C_DSTATE:]
        cpar = cpar_ref[...]
        dt = jax.nn.softplus(small + cpar[0:1, :])
        gl_all = dt * (-jnp.exp(cpar[1:2, :]))
        ys = [None] * (C_HEADS // 2)
        for which in range(2):
            sq = 2 * p + which
            valid = first if which == 0 else jnp.logical_not(first)
            g = _cumsum_rows_small(jnp.where(valid, gl_all, 0.0), 8)
            dtv = jnp.where(valid, dt, 0.0)
            dots = []
            for grp in range(C_GROUPS):
                cg = cm[:, grp * C_DSTATE:(grp + 1) * C_DSTATE]
                bgp = bm[:, grp * C_DSTATE:(grp + 1) * C_DSTATE]
                dots.append([jnp.sum(cg * bgp[s:s + 1, :], axis=-1, keepdims=True) for s in range(8)])
            for pr in range(C_HEADS // 2):
                xp = xs[:, pr * LANE:(pr + 1) * LANE]
                sp_lo = ssm_ref[sq, 2 * pr]
                sp_hi = ssm_ref[sq, 2 * pr + 1]
                spair = jnp.concatenate([sp_lo, sp_hi], axis=-1)
                y = jnp.zeros((8, LANE), f32)
                upd = jnp.zeros((C_DSTATE, LANE), f32)
                dl = jnp.zeros((1, LANE), f32)
                cdl = jnp.zeros((1, LANE), f32)
                for sub in range(2):
                    hd = 2 * pr + sub
                    grp = hd // (C_HEADS // C_GROUPS)
                    lm = lo if sub == 0 else jnp.logical_not(lo)
                    gcol = g[:, hd:hd + 1]
                    g_last = gcol[7:8, :]
                    xm = jnp.where(lm, xp, 0.0)
                    for s in range(8):
                        coef = dots[grp][s] * jnp.where(row >= s, jnp.exp(jnp.where(row >= s, gcol - gcol[s:s + 1, :], 0.0))
                                                        * dtv[s:s + 1, hd:hd + 1], 0.0)
                        y = y + coef * xm[s:s + 1, :]
                    cg = cm[:, grp * C_DSTATE:(grp + 1) * C_DSTATE]
                    bgp = bm[:, grp * C_DSTATE:(grp + 1) * C_DSTATE]
                    y = y + jnp.dot((cg * jnp.exp(gcol)).astype(bf16), jnp.where(lm, spair, 0.0).astype(bf16),
                                    preferred_element_type=f32)
                    kd = bgp * (dtv[:, hd:hd + 1] * jnp.exp(g_last - gcol))
                    upd = upd + lax.dot_general(kd.astype(bf16), xm.astype(bf16), (((0,), (0,)), ((), ())),
                                                preferred_element_type=f32)
                    dl = jnp.where(lm, jnp.exp(g_last), dl)
                    cdl = jnp.where(lm, cpar[2:3, hd:hd + 1], cdl)
                snew = dl * spair + upd
                ssm_out[sq, 2 * pr] = snew[:, 0:C_HEADDIM]
                ssm_out[sq, 2 * pr + 1] = snew[:, C_HEADDIM:]
                y = y + cdl * xp
                ys[pr] = y if which == 0 else jnp.where(first, ys[pr], y)
        yc = jnp.concatenate(ys, axis=-1)
        oc_ref[pl.ds(r, 8), :] = _c_post(yc, col(CB_CZ), cnw_ref[...]).astype(bf16)

        smd = small + dpar_ref[0:1, :]
        lsd = jax.nn.log_sigmoid(smd)
        dq = col(CB_DQ)
        dk = col(CB_DK)
        dv = col(CB_DV)
        do = col(CB_DO)
        dnw = dnw_ref[...]
        hs_out = [None] * D_HEADS
        for which in range(2):
            sq = 2 * p + which
            valid = first if which == 0 else jnp.logical_not(first)
            bcs = _cumsum_rows_small(jnp.where(valid, lsd, 0.0), 8)
            igv = jnp.where(valid, smd, NEG_BIG)
            mrow = mm_ref[pl.ds(sq, 1), :]
            mnew_row = jnp.zeros((1, D_HEADS), f32)
            hlane = lax.broadcasted_iota(i32, (1, D_HEADS), 1)
            for hd in range(D_HEADS):
                hs = slice(hd * LANE, (hd + 1) * LANE)
                li, lf = 8 + hd, 12 + hd
                bcol = bcs[:, lf:lf + 1]
                icol = igv[:, li:li + 1]
                mprev = mrow[:, hd:hd + 1]
                dcols = [jnp.where(row >= s, bcol - bcol[s:s + 1, :] + icol[s:s + 1, :], NEG_BIG) for s in range(8)]
                mx = dcols[0]
                for s in range(1, 8):
                    mx = jnp.maximum(mx, dcols[s])
                inter = bcol + mprev
                mt = jnp.maximum(inter, mx)
                w_inter = jnp.exp(inter - mt)
                qh = dq[:, hs] * (D_DK ** -0.5)
                kh = dk[:, hs]
                vh = dv[:, hs]
                cmat = mc_ref[sq, hd]
                nrow = mn_ref[sq, pl.ds(hd, 1), :]
                num = w_inter * jnp.dot(qh.astype(bf16), cmat.astype(bf16), preferred_element_type=f32)
                den = w_inter * jnp.sum(qh * nrow, axis=-1, keepdims=True)
                for s in range(8):
                    w = jnp.where(row >= s, jnp.exp(dcols[s] - mt), 0.0)
                    qk = jnp.sum(qh * kh[s:s + 1, :], axis=-1, keepdims=True) * w
                    num = num + qk * vh[s:s + 1, :]
                    den = den + qk
                hh = num / jnp.maximum(jnp.abs(den), jnp.exp(-mt))
                mnew = mt[7:8, :]
                blast = bcol[7:8, :]
                wk = jnp.exp(blast - bcol + icol - mnew)
                decay = jnp.exp(blast + mprev - mnew)
                wkk = wk * kh
                mc_out[sq, hd] = decay * cmat + lax.dot_general(wkk.astype(bf16), vh.astype(bf16), (((0,), (0,)), ((), ())),
                                                                preferred_element_type=f32)
                mn_out[sq, pl.ds(hd, 1), :] = decay * nrow + jnp.sum(wkk, axis=0, keepdims=True)
                mnew_row = jnp.where(hlane == hd, mnew, mnew_row)
                o = jax.nn.sigmoid(do[:, hs]) * _rms(hh, dnw[:, hs])
                hs_out[hd] = o if which == 0 else jnp.where(first, hs_out[hd], o)
            mm_out[pl.ds(sq, 1), :] = mnew_row
        for hd in range(D_HEADS):
            od_ref[pl.ds(r, 8), hd * LANE:(hd + 1) * LANE] = hs_out[hd].astype(bf16)
        return carry

    lax.fori_loop(0, SB // 2, tile, 0)


_N_SAMPLE_IN = 19


def _sample_body_stacked(*refs):
    _sample_body(*refs[:_N_SAMPLE_IN], *refs[_N_SAMPLE_IN + 3:])


def _sample_mixer(h, st_hgrn, st_ssm, st_conv, st_c, st_n, st_m, pa, pb, pc, pd, layer, stacked):
    rows = SB * DEC_SEQ
    row0 = N_PROMPT // rows

    def blk(shape):
        nd = len(shape)
        return pl.BlockSpec((SB,) + shape, lambda i, nd=nd: (i,) + (0,) * nd)

    def blk_stacked(shape):
        nd = len(shape)
        return pl.BlockSpec((None, SB) + shape, lambda i, nd=nd: (layer, i) + (0,) * nd)

    def full(shape):
        nd = len(shape)
        return pl.BlockSpec(shape, lambda i, nd=nd: (0,) * nd)

    big = ((B_HEADS, LANE, LANE), (C_HEADS, C_DSTATE, C_HEADDIM), (D_HEADS, LANE, LANE))
    in_state_specs = [blk_stacked(big[0]), blk_stacked(big[1]), blk_stacked((C_CONV - 1, C_CONV_DIM)),
                      blk_stacked(big[2]), blk_stacked((D_HEADS, LANE)), blk_stacked((D_HEADS,))]
    out_state_specs = [blk_stacked(big[0]), blk_stacked(big[1]), blk((C_CONV - 1, C_CONV_DIM)), blk_stacked(big[2]),
                       blk((D_HEADS, LANE)), blk((D_HEADS,))]
    params = list(pa) + list(pb) + list(pc) + list(pd)
    br_spec = pl.BlockSpec((rows, BW), lambda i: (i, 0))
    br_shape = jax.ShapeDtypeStruct((N_SAMPLE, BW), bf16)
    stacked_shape = lambda a: jax.ShapeDtypeStruct(a.shape, f32)
    per_layer = lambda a: jax.ShapeDtypeStruct(a.shape[1:], f32)
    inputs = [h, st_hgrn, st_ssm, st_conv, st_c, st_n, st_m] + params
    in_specs = [pl.BlockSpec((rows, MIX_W), lambda i: (row0 + i, 0))] + in_state_specs + [full(p.shape) for p in params]
    assert len(inputs) == _N_SAMPLE_IN
    body, aliases = _sample_body, {}
    if stacked is not None:
        body = _sample_body_stacked
        inputs = inputs + list(stacked)
        in_specs = in_specs + [pl.BlockSpec(memory_space=pl.ANY)] * 3
        aliases = {_N_SAMPLE_IN: 5, _N_SAMPLE_IN + 1: 6, _N_SAMPLE_IN + 2: 8}
    return pl.pallas_call(
        body,
        out_shape=(br_shape, br_shape, br_shape, br_shape, jax.ShapeDtypeStruct((N_SAMPLE, BW), f32),
                   stacked_shape(st_hgrn), stacked_shape(st_ssm),
                   per_layer(st_conv), stacked_shape(st_c), per_layer(st_n), per_layer(st_m)),
        grid=(DEC_BATCH // SB,),
        in_specs=in_specs,
        out_specs=(br_spec, br_spec, br_spec, br_spec, br_spec) + tuple(out_state_specs),
        input_output_aliases=aliases,
        compiler_params=_cparams(("parallel",)),
        name="sample_mixer",
    )(*inputs)


def _merge_body(x_ref, nw_ref, ba_ref, bb_ref, bc_ref, bd_ref, wg_ref, wb_ref, o_ref, xn_ref, acc_ref):
    jc = pl.program_id(1)
    n = pl.program_id(2)

    @pl.when((jc == 0) & (n == 0))
    def _():
        xn_ref[...] = _rms(x_ref[...], nw_ref[...]).astype(bf16)

    gate = jax.nn.sigmoid(jnp.dot(xn_ref[...], wg_ref[...], preferred_element_type=f32))
    for k, br_ref in enumerate((ba_ref, bb_ref, bc_ref, bd_ref)):
        @pl.when(n == k)
        def _(br_ref=br_ref, k=k):
            contrib = gate * jnp.dot(br_ref[...], wb_ref[0], preferred_element_type=f32)
            if k == 0:
                acc_ref[...] = contrib
            else:
                acc_ref[...] = acc_ref[...] + contrib

    @pl.when(n == N_BRANCH - 1)
    def _():
        o_ref[...] = acc_ref[...].astype(bf16)


def _merge(x, nw, brs, w_gate, w_branch, tm, tn):
    m = x.shape[0]
    ncol = D_MODEL // tn
    br_spec = pl.BlockSpec((tm, BW), lambda i, jc, n: (i, 0), pipeline_mode=pl.Buffered(1))
    return pl.pallas_call(
        _merge_body,
        out_shape=jax.ShapeDtypeStruct((m, D_MODEL), bf16),
        grid=(m // tm, ncol, N_BRANCH),
        in_specs=[pl.BlockSpec((tm, D_MODEL), lambda i, jc, n: (i, 0), pipeline_mode=pl.Buffered(1)),
                  pl.BlockSpec((1, D_MODEL), lambda i, jc, n: (0, 0)),
                  br_spec, br_spec, br_spec, br_spec,
                  pl.BlockSpec((D_MODEL, tn), lambda i, jc, n: (0, n * ncol + jc)),
                  pl.BlockSpec((1, BW, tn), lambda i, jc, n: (n, 0, jc))],
        out_specs=pl.BlockSpec((tm, tn), lambda i, jc, n: (i, jc)),
        scratch_shapes=[pltpu.VMEM((tm, D_MODEL), bf16), pltpu.VMEM((tm, tn), f32)],
        compiler_params=_cparams(("parallel", "arbitrary", "arbitrary")),
        name="merge",
    )(x, nw, *brs, w_gate, w_branch)


def _top16_rows(s):
    n = s.shape[0]
    rid = lax.broadcasted_iota(i32, s.shape, 0).astype(f32)
    out = []
    for _ in range(PEER_TOPK):
        m = jnp.max(s, axis=0, keepdims=True)
        am = jnp.min(jnp.where(s == m, rid, float(n)), axis=0, keepdims=True)
        out.append((m, am))
        s = jnp.where(rid == am, -jnp.inf, s)
    return out


def _collect16(pairs, tb):
    r16 = lax.broadcasted_iota(i32, (PEER_TOPK, tb), 0)
    v = jnp.zeros((PEER_TOPK, tb), f32)
    ix = jnp.zeros((PEER_TOPK, tb), f32)
    for k, (m, am) in enumerate(pairs):
        v = jnp.where(r16 == k, m, v)
        ix = jnp.where(r16 == k, am, ix)
    return v, ix.astype(i32)


def _peer_route_body(q_ref, keys_ref, ia_ref, ib_ref, gt_ref):
    tb = q_ref.shape[0]
    vals = []
    idxs = []
    for p in range(2):
        st = lax.dot_general(keys_ref[0, p], q_ref[:, p * LANE:(p + 1) * LANE], (((1,), (1,)), ((), ())),
                             preferred_element_type=f32)
        v, ix = _collect16(_top16_rows(st), tb)
        vals.append(v)
        idxs.append(ix)
    cand = jnp.concatenate([vals[0][ka:ka + 1, :] + vals[1] for ka in range(PEER_TOPK)], axis=0)
    fs, pos = _collect16(_top16_rows(cand), tb)
    ka = pos >> 4
    kb = pos & (PEER_TOPK - 1)
    i1 = jnp.zeros((PEER_TOPK, tb), i32)
    i2 = jnp.zeros((PEER_TOPK, tb), i32)
    for j in range(PEER_TOPK):
        i1 = jnp.where(ka == j, idxs[0][j:j + 1, :], i1)
        i2 = jnp.where(kb == j, idxs[1][j:j + 1, :], i2)
    pe = jnp.exp(fs - fs[0:1, :])
    ia_ref[...] = i1
    ib_ref[...] = i2
    gt_ref[...] = pe / jnp.sum(pe, axis=0, keepdims=True)


def _peer_route(q, keys, tb):
    m = q.shape[0]
    spec = pl.BlockSpec((PEER_TOPK, tb), lambda i, hd: (hd, i))
    nslot = PEER_HEADS * PEER_TOPK
    return pl.pallas_call(
        _peer_route_body,
        out_shape=(jax.ShapeDtypeStruct((nslot, m), i32), jax.ShapeDtypeStruct((nslot, m), i32),
                   jax.ShapeDtypeStruct((nslot, m), f32)),
        grid=(m // tb, PEER_HEADS),
        in_specs=[pl.BlockSpec((tb, 2 * LANE), lambda i, hd: (i, hd)),
                  pl.BlockSpec((1, 2, PEER_NKEYS, LANE), lambda i, hd: (hd, 0, 0, 0))],
        out_specs=(spec, spec, spec),
        compiler_params=_cparams(("parallel", "parallel")),
        name="peer_route",
    )(q, keys)


GATE_UNROLL = 8


def _peer_gates_body(ia_ref, ib_ref, gt_ref, o_ref, a_s, b_s, g_s):
    tg = ia_ref.shape[1]
    a_s[...] = ia_ref[...].astype(f32).T
    b_s[...] = ib_ref[...].astype(f32).T
    g_s[...] = gt_ref[...].T
    sub = lax.broadcasted_iota(i32, (LANE, LANE), 0).astype(f32)

    def step(r, carry):
        r8 = pl.multiple_of(r * GATE_UNROLL, GATE_UNROLL)
        a8 = a_s[pl.ds(r8, GATE_UNROLL), :]
        b8 = b_s[pl.ds(r8, GATE_UNROLL), :]
        g8 = g_s[pl.ds(r8, GATE_UNROLL), :]
        for u in range(GATE_UNROLL):
            at = jnp.where(sub == a8[u:u + 1, :], 1.0, 0.0).astype(bf16)
            bt = jnp.where(sub == b8[u:u + 1, :], g8[u:u + 1, :], 0.0).astype(bf16)
            gm = lax.dot_general(at, bt, (((1,), (1,)), ((), ())), preferred_element_type=f32)
            o_ref[r8 + u] = gm.astype(bf16)
        return carry

    lax.fori_loop(0, tg // GATE_UNROLL, step, 0)


def _peer_gates(ia, ib, gt, tg):
    nslot, m = ia.shape
    spec = pl.BlockSpec((nslot, tg), lambda i: (0, i))
    return pl.pallas_call(
        _peer_gates_body,
        out_shape=jax.ShapeDtypeStruct((m, PEER_NKEYS, PEER_NKEYS), bf16),
        grid=(m // tg,),
        in_specs=[spec, spec, spec],
        out_specs=pl.BlockSpec((tg, PEER_NKEYS, PEER_NKEYS), lambda i: (i, 0, 0)),
        scratch_shapes=[pltpu.VMEM((tg, nslot), f32)] * 3,
        compiler_params=_cparams(("parallel",)),
        name="peer_gates",
    )(ia, ib, gt)


def _peer_experts_body(x_ref, nw_ref, u_ref, v_ref, g_ref, o_ref, xn_ref):
    @pl.when(pl.program_id(1) == 0)
    def _():
        x = x_ref[...]
        xn_ref[...] = _rms(x, nw_ref[...]).astype(bf16)
        o_ref[...] = x

    hmat = lax.dot_general(xn_ref[...], u_ref[...], (((1,), (1,)), ((), ())), preferred_element_type=f32)
    w = (_gelu(hmat) * g_ref[...].astype(f32)).astype(bf16)
    o_ref[...] += jnp.dot(w, v_ref[...], preferred_element_type=f32)


def _peer_experts(x, nw, u, v, g, tb, eb):
    m = x.shape[0]
    return pl.pallas_call(
        _peer_experts_body,
        out_shape=jax.ShapeDtypeStruct((m, D_MODEL), f32),
        grid=(m // tb, PEER_N // eb),
        in_specs=[pl.BlockSpec((tb, D_MODEL), lambda i, j: (i, 0), pipeline_mode=pl.Buffered(1)),
                  pl.BlockSpec((1, D_MODEL), lambda i, j: (0, 0)),
                  pl.BlockSpec((eb, D_MODEL), lambda i, j: (j, 0)),
                  pl.BlockSpec((eb, D_MODEL), lambda i, j: (j, 0)),
                  pl.BlockSpec((tb, eb), lambda i, j: (i, j))],
        out_specs=pl.BlockSpec((tb, D_MODEL), lambda i, j: (i, 0), pipeline_mode=pl.Buffered(1)),
        scratch_shapes=[pltpu.VMEM((tb, D_MODEL), bf16)],
        compiler_params=_cparams(("parallel", "arbitrary")),
        name="peer_experts",
    )(x, nw, u, v, g)


_IN_OFFS = tuple(int(v) for v in np.cumsum((0,) + IN_SPLITS))
_O_CDT, _O_DQ, _O_DIG, _O_GATES, _O_END = _IN_OFFS[8], _IN_OFFS[9], _IN_OFFS[13], _IN_OFFS[15], _IN_OFFS[16]
PREP_ROWS = 128


def _prep_w_in_body(w_ref, om_ref, og_ref):
    rows = w_ref.shape[0]
    om_ref[:, 0:_O_CDT] = w_ref[:, 0:_O_CDT].astype(bf16)
    om_ref[:, _O_CDT:SMALL_COL] = w_ref[:, _O_DQ:_O_DIG].astype(bf16)
    small = jnp.concatenate([w_ref[:, _O_CDT:_O_DQ], w_ref[:, _O_DIG:_O_GATES],
                             jnp.zeros((rows, LANE - 16), f32)], axis=-1)
    om_ref[:, SMALL_COL:SMALL_COL + LANE] = small.astype(bf16)
    om_ref[:, SMALL_COL + LANE:MIX_W] = jnp.zeros((rows, MIX_W - SMALL_COL - LANE), bf16)
    og_ref[...] = w_ref[:, _O_GATES:_O_END].astype(bf16)


def _prep_w_in(w, layer):
    _, k, n = w.shape
    ng = _O_END - _O_GATES
    return pl.pallas_call(
        _prep_w_in_body,
        out_shape=(jax.ShapeDtypeStruct((k, MIX_W), bf16), jax.ShapeDtypeStruct((k, ng), bf16)),
        grid=(k // PREP_ROWS,),
        in_specs=[pl.BlockSpec((None, PREP_ROWS, n), lambda i: (layer, i, 0))],
        out_specs=(pl.BlockSpec((PREP_ROWS, MIX_W), lambda i: (i, 0)), pl.BlockSpec((PREP_ROWS, ng), lambda i: (i, 0))),
        compiler_params=_cparams(("parallel",)),
        name="prep_w_in",
    )(w)


def _lane_row(vals, start):
    row = jnp.zeros((LANE,), f32)
    return row.at[start:start + vals.shape[0]].set(vals)


def kernel(x_prompt, x_sample, state_hgrn, state_ssm, state_conv, state_mlstm_c, state_mlstm_n, state_mlstm_m, norm1_w, w_in, a_ln_w, a_ln_b, a_ws, a_bs, b_lb_logits, b_norm_w, c_conv_w, c_conv_b, c_dt_bias, c_a_log, c_d, c_norm_w, d_ig_b, d_fg_b, d_norm_w, w_branch, w_out, norm2_w, peer_wq, peer_keys, peer_u, peer_v, final_norm_w):
    x = jnp.concatenate([x_prompt.reshape(N_PROMPT, D_MODEL), x_sample.reshape(N_SAMPLE, D_MODEL)], axis=0)
    lbs = jax.nn.softmax(b_lb_logits.astype(f32), axis=0)
    lbs = jnp.cumsum(lbs, axis=0) - lbs[0]
    zeros8 = jnp.zeros((8, LANE), f32)
    news_p = []
    news_s = []
    stacked_s = None
    for l in range(DEPTH):
        w_mix, w_gate = _prep_w_in(w_in, l)
        h = _norm_mm(x, norm1_w[l][None, :], w_mix, f32, 1088, 512)

        lnw = a_ln_w[l][None, :]
        lnb = a_ln_b[l][None, :]
        lb = lbs[l][None, :]
        bnw = b_norm_w[l][None, :]
        cw = c_conv_w[l]
        cb = c_conv_b[l][None, :]
        cpar = zeros8.at[0].set(_lane_row(c_dt_bias[l], 0)).at[1].set(_lane_row(c_a_log[l], 0)).at[2].set(_lane_row(c_d[l], 0))
        cnw = c_norm_w[l][None, :]
        dpar = zeros8.at[0].set(_lane_row(d_ig_b[l], 8) + _lane_row(d_fg_b[l], 12))
        dnw = d_norm_w[l][None, :]

        bs_full = jnp.repeat(a_bs[l].T, LANE, axis=1)
        br_a = _prompt_a(h, lnw, lnb, a_ws[l], bs_full)
        br_b, hg_p = _prompt_b(h, lb, bnw)
        br_c, ssm_p, cv_p = _prompt_c(h, cw, cb, cpar, cnw)
        br_d, mc_p, mn_p, mm_p = _prompt_d(h, dpar, dnw)
        news_p.append((hg_p, ssm_p, cv_p, mc_p, mn_p[:, :D_HEADS, :], mm_p[:, :D_HEADS, 0]))

        w4 = jnp.tril(a_ws[l][:, :DEC_SEQ, :DEC_SEQ])
        w8 = jnp.zeros((A_GROUPS, 8, 8), f32).at[:, :4, :4].set(w4).at[:, 4:, 4:].set(w4)
        ws8 = jnp.repeat(jnp.transpose(w8, (2, 1, 0)), LANE, axis=2)
        bs8 = jnp.repeat(jnp.tile(a_bs[l][:, :DEC_SEQ], (1, 2)).T, LANE, axis=1)
        outs = _sample_mixer(h, state_hgrn, state_ssm, state_conv, state_mlstm_c, state_mlstm_n, state_mlstm_m,
                             (lnw, lnb, ws8, bs8), (lb, bnw), (cw, cb, cpar, cnw), (dpar, dnw), l, stacked_s)
        sa, sb, sc, sd, chv = outs[:5]
        stacked_s = (outs[5], outs[6], outs[8])
        news_s.append((outs[7], outs[9], outs[10], chv.reshape(DEC_BATCH, DEC_SEQ, BW)))

        brs = [jnp.concatenate([p, s], axis=0) for p, s in ((br_a, sa), (br_b, sb), (br_c, sc), (br_d, sd))]
        mixin = _merge(x, norm1_w[l][None, :], brs, w_gate, w_branch[l].astype(bf16), 1088, 1024)
        x = _mm_res(mixin, w_out[l].astype(bf16), x, 1088, 512)

        q = _norm_mm(x, norm2_w[l][None, :], peer_wq[l].astype(bf16), bf16, 1088, 512)
        ia, ib, gt = _peer_route(q, peer_keys[l].astype(bf16), 512)
        g = _peer_gates(ia, ib, gt, 128).reshape(N_TOK, PEER_N)
        x = _peer_experts(x, norm2_w[l][None, :], peer_u[l].astype(bf16), peer_v[l].astype(bf16), g, 1088, 512)

    y_p, y_s = _final_norm(x, final_norm_w[None, :])
    y_prompt = y_p.reshape(BATCH, SEQ, D_MODEL)
    y_sample = y_s.reshape(DEC_BATCH, DEC_SEQ, D_MODEL)
    stack = lambda news, k: jnp.stack([n[k] for n in news], axis=0)
    hgrn_s, ssm_s, mc_s = stacked_s
    conv_s, mn_s, mm_s, chunk_v_s = (stack(news_s, k) for k in range(4))
    return ((y_prompt, y_sample) + tuple(stack(news_p, k) for k in range(6))
            + (hgrn_s, ssm_s, conv_s, mc_s, mn_s, mm_s, chunk_v_s))
```

```python
import functools

import jax
import jax.numpy as jnp
import numpy as np
from jax import lax
from jax.experimental import pallas as pl
from jax.experimental.pallas import tpu as pltpu

f32 = jnp.float32
bf16 = jnp.bfloat16
i32 = jnp.int32
HI = lax.Precision.HIGHEST

D_MODEL = 2048
BATCH = 4
SEQ = 2048
DEPTH = 2
DEC_BATCH = 128
DEC_SEQ = 4
N_BRANCH = 4
BW = 512
A_GROUPS = 4
A_CHUNK = 128
B_HEADS = 4
LB_FLOOR = 1e-30
C_HEADDIM = 64
C_HEADS = 8
C_GROUPS = 2
C_DSTATE = 128
C_CONV = 4
C_CONV_DIM = 1024
D_HEADS = 4
D_DK = 128
NEG_BIG = -1e30
IN_SPLITS = (512, 512, 512, 512, 512, 512, 512, 1024, 8, 512, 512, 512, 512, 4, 4, 8192)
PEER_NKEYS = 128
PEER_N = PEER_NKEYS * PEER_NKEYS
PEER_HEADS = 8
PEER_TOPK = 16
EPS = 1e-6
INV_SQRT2 = 0.7071067811865476

N_PROMPT = BATCH * SEQ
N_SAMPLE = DEC_BATCH * DEC_SEQ
N_TOK = N_PROMPT + N_SAMPLE

CB_AU, CB_AV, CB_BQ, CB_BF, CB_BI, CB_BG, CB_CZ, CB_CX0, CB_CX1, CB_DQ, CB_DK, CB_DV, CB_DO = range(13)
SMALL_COL = 13 * 512
MIX_W = 14 * 512
LANE = 128
CHUNK = 128
NCHUNK = SEQ // CHUNK
SUB = 16
SB = 8
VMEM_LIMIT = 56 * 1024 * 1024


def _gelu(x):
    return 0.5 * x * (1.0 + lax.erf(x * INV_SQRT2))


def _rms(x, w):
    ms = jnp.mean(x * x, axis=-1, keepdims=True)
    return x * lax.rsqrt(ms + EPS) * w


def _tri(n):
    r = lax.broadcasted_iota(i32, (n, n), 0)
    c = lax.broadcasted_iota(i32, (n, n), 1)
    return r >= c


def _cumsum_rows_small(x, n):
    row = lax.broadcasted_iota(i32, (n, 1), 0)
    acc = jnp.zeros_like(x)
    for s in range(n):
        acc = acc + jnp.where(row >= s, x[s:s + 1, :], 0.0)
    return acc


def _row_to_col(row, eye):
    return jnp.sum(jnp.where(eye, row, 0.0), axis=-1, keepdims=True)


def _cparams(sem, vmem=VMEM_LIMIT):
    return pltpu.CompilerParams(dimension_semantics=sem, vmem_limit_bytes=vmem)


def _norm_mm_body(x_ref, nw_ref, w_ref, o_ref, xn_ref):
    @pl.when(pl.program_id(1) == 0)
    def _():
        xn_ref[...] = _rms(x_ref[...], nw_ref[...]).astype(bf16)

    o_ref[...] = jnp.dot(xn_ref[...], w_ref[...], preferred_element_type=f32).astype(o_ref.dtype)


def _w_spec(w, layer, block, index_map):
    if layer is None:
        return pl.BlockSpec(block, index_map)
    return pl.BlockSpec((None,) + block, lambda *g: (layer,) + index_map(*g))


def _norm_mm(x, nw, w, out_dtype, tm, tn, layer=None):
    m, k = x.shape
    n = w.shape[-1]
    return pl.pallas_call(
        _norm_mm_body,
        out_shape=jax.ShapeDtypeStruct((m, n), out_dtype),
        grid=(m // tm, n // tn),
        in_specs=[pl.BlockSpec((tm, k), lambda i, j: (i, 0)),
                  pl.BlockSpec((1, k), lambda i, j: (0, 0)),
                  _w_spec(w, layer, (k, tn), lambda i, j: (0, j))],
        out_specs=pl.BlockSpec((tm, tn), lambda i, j: (i, j)),
        scratch_shapes=[pltpu.VMEM((tm, k), bf16)],
        compiler_params=_cparams(("parallel", "arbitrary")),
        name="norm_mm",
    )(x, nw, w)


def _mm_res_body(a_ref, w_ref, r_ref, o_ref):
    o_ref[...] = r_ref[...] + jnp.dot(a_ref[...], w_ref[...], preferred_element_type=f32)


def _mm_res(a, w, res, tm, tn, layer=None):
    m, k = a.shape
    n = w.shape[-1]
    return pl.pallas_call(
        _mm_res_body,
        out_shape=jax.ShapeDtypeStruct((m, n), f32),
        grid=(m // tm, n // tn),
        in_specs=[pl.BlockSpec((tm, k), lambda i, j: (i, 0)),
                  _w_spec(w, layer, (k, tn), lambda i, j: (0, j)),
                  pl.BlockSpec((tm, tn), lambda i, j: (i, j))],
        out_specs=pl.BlockSpec((tm, tn), lambda i, j: (i, j)),
        compiler_params=_cparams(("parallel", "arbitrary")),
        name="mm_res",
    )(a, w, res)


def _final_norm_body(x_ref, w_ref, op_ref, os_ref):
    y = _rms(x_ref[...], w_ref[...])
    i = pl.program_id(0)

    @pl.when(i < N_PROMPT // N_SAMPLE)
    def _():
        op_ref[...] = y

    @pl.when(i == N_PROMPT // N_SAMPLE)
    def _():
        os_ref[...] = y


def _final_norm(x, w):
    m, k = x.shape
    tm = N_SAMPLE
    last_p = N_PROMPT // tm - 1
    return pl.pallas_call(
        _final_norm_body,
        out_shape=(jax.ShapeDtypeStruct((N_PROMPT, k), f32), jax.ShapeDtypeStruct((N_SAMPLE, k), f32)),
        grid=(m // tm,),
        in_specs=[pl.BlockSpec((tm, k), lambda i: (i, 0)), pl.BlockSpec((1, k), lambda i: (0, 0))],
        out_specs=(pl.BlockSpec((tm, k), lambda i: (jnp.minimum(i, last_p), 0)),
                   pl.BlockSpec((tm, k), lambda i: (0, 0))),
        compiler_params=_cparams(("arbitrary",)),
        name="final_norm",
    )(x, w)


def _a_uv(au, av, lnw, lnb):
    u = _gelu(au)
    g = _gelu(av)
    xc = g - jnp.mean(g, axis=-1, keepdims=True)
    var = jnp.mean(xc * xc, axis=-1, keepdims=True)
    v = xc * lax.rsqrt(var + EPS) * lnw + lnb
    return u, v


def _b_pre(bq, bf_, lb):
    q = bq * jax.nn.sigmoid(bq)
    logf = jnp.logaddexp(jnp.log(jnp.maximum(lb, LB_FLOOR)), jnp.log1p(-lb) + jax.nn.log_sigmoid(bf_))
    kb = (1.0 - lb) * jax.nn.sigmoid(-bf_)
    return q, kb, logf


def _hgrn_chunk(q, k, v, gl, s_mat, n, eye, valid=None):
    if valid is not None:
        gl = jnp.where(valid, gl, 0.0)
        k = jnp.where(valid, k, 0.0)
    g = _cumsum_rows_small(gl, n)
    g_last = g[n - 1:n, :]
    o = jnp.dot((q * jnp.exp(g)).astype(bf16), s_mat.astype(bf16), preferred_element_type=f32)
    row = lax.broadcasted_iota(i32, (n, 1), 0)
    for s in range(n):
        m = row >= s
        d = jnp.where(m, g - g[s:s + 1, :], 0.0)
        p = jnp.where(m, q * k[s:s + 1, :] * jnp.exp(d), 0.0)
        o = o + jnp.sum(p, axis=-1, keepdims=True) * v[s:s + 1, :]
    k_dec = k * jnp.exp(g_last - g)
    upd = lax.dot_general(k_dec.astype(bf16), v.astype(bf16), (((0,), (0,)), ((), ())), preferred_element_type=f32)
    s_new = jnp.exp(_row_to_col(g_last, eye)) * s_mat + upd
    return o, s_new


def _b_post(o, bg, nw):
    return _rms(o, nw) * (bg * jax.nn.sigmoid(bg))


def _hrow(b, c):
    return b * NCHUNK + c


def _h_spec(colblk):
    return pl.BlockSpec((CHUNK, BW), lambda b, c, cb=colblk: (_hrow(b, c), cb))


_SMALL_SPEC = pl.BlockSpec((CHUNK, LANE), lambda b, c: (_hrow(b, c), SMALL_COL // LANE))
_BR_SPEC = pl.BlockSpec((CHUNK, BW), lambda b, c: (_hrow(b, c), 0))


def _full_spec(shape):
    nd = len(shape)
    return pl.BlockSpec(shape, lambda b, c, nd=nd: (0,) * nd)


def _pa_body(au_ref, av_ref, lnw_ref, lnb_ref, ws_ref, bs_ref, o_ref):
    u, v = _a_uv(au_ref[...], av_ref[...], lnw_ref[...], lnb_ref[...])
    tri = _tri(CHUNK)
    vb = v.astype(bf16)
    parts = []
    for g in range(A_GROUPS):
        w = jnp.where(tri, ws_ref[g], 0.0).astype(bf16)
        parts.append(jnp.dot(w, vb[:, g * LANE:(g + 1) * LANE], preferred_element_type=f32))
    sp = jnp.concatenate(parts, axis=-1) + bs_ref[...]
    o_ref[...] = (u * sp).astype(bf16)


def _prompt_a(h, lnw, lnb, ws, bs_full):
    return pl.pallas_call(
        _pa_body,
        out_shape=jax.ShapeDtypeStruct((N_TOK, BW), bf16),
        grid=(BATCH, NCHUNK),
        in_specs=[_h_spec(CB_AU), _h_spec(CB_AV), _full_spec((1, BW)), _full_spec((1, BW)),
                  _full_spec((A_GROUPS, A_CHUNK, A_CHUNK)), _full_spec((A_CHUNK, BW))],
        out_specs=_BR_SPEC,
        compiler_params=_cparams(("parallel", "parallel")),
        name="prompt_gmlp",
    )(h, h, lnw, lnb, ws, bs_full)


def _pb_body(bq_ref, bf_ref, bi_ref, bg_ref, lb_ref, nw_ref, o_ref, st_ref, s_ref, q_s, k_s, v_s, g_s, o_s):
    c = pl.program_id(1)

    @pl.when(c == 0)
    def _():
        s_ref[...] = jnp.zeros_like(s_ref)

    q, kb, logf = _b_pre(bq_ref[...], bf_ref[...], lb_ref[...])
    q_s[...] = q
    k_s[...] = kb
    v_s[...] = bi_ref[...]
    g_s[...] = logf
    eye = lax.broadcasted_iota(i32, (LANE, LANE), 0) == lax.broadcasted_iota(i32, (LANE, LANE), 1)
    def sub(j, carry):
        r = pl.multiple_of(j * SUB, SUB)
        for hd in range(B_HEADS):
            hs = slice(hd * LANE, (hd + 1) * LANE)
            o, s_new = _hgrn_chunk(q_s[pl.ds(r, SUB), hs], k_s[pl.ds(r, SUB), hs], v_s[pl.ds(r, SUB), hs],
                                   g_s[pl.ds(r, SUB), hs], s_ref[hd], SUB, eye)
            o_s[pl.ds(r, SUB), hs] = o
            s_ref[hd] = s_new
        return carry

    lax.fori_loop(0, CHUNK // SUB, sub, 0)
    bg = bg_ref[...]
    nw = nw_ref[...]
    for hd in range(B_HEADS):
        hs = slice(hd * LANE, (hd + 1) * LANE)
        o_ref[:, hs] = _b_post(o_s[:, hs], bg[:, hs], nw).astype(bf16)

    @pl.when(c == NCHUNK - 1)
    def _():
        st_ref[0] = s_ref[...]


def _prompt_b(h, lb, nw):
    return pl.pallas_call(
        _pb_body,
        out_shape=(jax.ShapeDtypeStruct((N_TOK, BW), bf16),
                   jax.ShapeDtypeStruct((BATCH, B_HEADS, LANE, LANE), f32)),
        grid=(BATCH, NCHUNK),
        in_specs=[_h_spec(CB_BQ), _h_spec(CB_BF), _h_spec(CB_BI), _h_spec(CB_BG),
                  _full_spec((1, BW)), _full_spec((1, LANE))],
        out_specs=(_BR_SPEC, pl.BlockSpec((1, B_HEADS, LANE, LANE), lambda b, c: (b, 0, 0, 0))),
        scratch_shapes=[pltpu.VMEM((B_HEADS, LANE, LANE), f32)] + [pltpu.VMEM((CHUNK, BW), f32)] * 5,
        compiler_params=_cparams(("parallel", "arbitrary")),
        name="prompt_hgrn",
    )(h, h, h, h, lb, nw)


def _c_conv_silu(win0, win1, win2, win3, cw_ref, cb_ref):
    y = cb_ref[...] + win0 * cw_ref[0:1, :] + win1 * cw_ref[1:2, :] + win2 * cw_ref[2:3, :] + win3 * cw_ref[3:4, :]
    return y * jax.nn.sigmoid(y)


def _c_post(yc, cz, nw):
    y = yc * (cz * jax.nn.sigmoid(cz))
    gw = BW // C_GROUPS
    parts = [_rms(y[:, g * gw:(g + 1) * gw], nw[:, g * gw:(g + 1) * gw]) for g in range(C_GROUPS)]
    return jnp.concatenate(parts, axis=-1)


def _pc_body(cz_ref, cx0_ref, cx1_ref, sm_ref, cw_ref, cb_ref, par_ref, nw_ref,
             o_ref, st_ref, cv_ref, xpad, sp_ref):
    c = pl.program_id(1)

    @pl.when(c == 0)
    def _():
        xpad[pl.ds(0, 8), :] = jnp.zeros((8, C_CONV_DIM), f32)
        sp_ref[...] = jnp.zeros_like(sp_ref)

    xpad[pl.ds(8, CHUNK), 0:BW] = cx0_ref[...]
    xpad[pl.ds(8, CHUNK), BW:2 * BW] = cx1_ref[...]
    xbc = _c_conv_silu(xpad[pl.ds(5, CHUNK), :], xpad[pl.ds(6, CHUNK), :], xpad[pl.ds(7, CHUNK), :],
                       xpad[pl.ds(8, CHUNK), :], cw_ref, cb_ref)

    @pl.when(c == NCHUNK - 1)
    def _():
        cv_ref[0] = xpad[pl.ds(CHUNK + 5, 3), :]

    xpad[pl.ds(0, 8), :] = xpad[pl.ds(CHUNK, 8), :]

    xs = xbc[:, 0:BW]
    bm = xbc[:, BW:BW + 2 * C_DSTATE]
    cm = xbc[:, BW + 2 * C_DSTATE:]
    par = par_ref[...]
    dt = jax.nn.softplus(sm_ref[...] + par[0:1, :])
    gl = dt * (-jnp.exp(par[1:2, :]))
    tri = _tri(CHUNK)
    g = jnp.dot(tri.astype(f32), gl, precision=HI, preferred_element_type=f32)
    gt = g.T
    dtt = dt.T
    lane = lax.broadcasted_iota(i32, (1, LANE), 1)
    lo = lane < C_HEADDIM
    cbs = []
    for grp in range(C_GROUPS):
        cg = cm[:, grp * C_DSTATE:(grp + 1) * C_DSTATE].astype(bf16)
        bg = bm[:, grp * C_DSTATE:(grp + 1) * C_DSTATE].astype(bf16)
        cbs.append(lax.dot_general(cg, bg, (((1,), (1,)), ((), ())), preferred_element_type=f32))
    ys = []
    for pr in range(C_HEADS // 2):
        xp = xs[:, pr * LANE:(pr + 1) * LANE]
        sp = sp_ref[pr]
        y = jnp.zeros((CHUNK, LANE), f32)
        upd = jnp.zeros((C_DSTATE, LANE), f32)
        dl = jnp.zeros((1, LANE), f32)
        cdl = jnp.zeros((1, LANE), f32)
        for sub in range(2):
            hd = 2 * pr + sub
            grp = hd // (C_HEADS // C_GROUPS)
            lm = lo if sub == 0 else jnp.logical_not(lo)
            col = g[:, hd:hd + 1]
            g_last = col[CHUNK - 1:CHUNK, :]
            dec = jnp.exp(jnp.where(tri, col - gt[hd:hd + 1, :], 0.0))
            sc = jnp.where(tri, cbs[grp] * dec * dtt[hd:hd + 1, :], 0.0)
            xm = jnp.where(lm, xp, 0.0).astype(bf16)
            cg = cm[:, grp * C_DSTATE:(grp + 1) * C_DSTATE]
            bg = bm[:, grp * C_DSTATE:(grp + 1) * C_DSTATE]
            y = y + jnp.dot(sc.astype(bf16), xm, preferred_element_type=f32)
            y = y + jnp.dot((cg * jnp.exp(col)).astype(bf16), jnp.where(lm, sp, 0.0).astype(bf16),
                            preferred_element_type=f32)
            kd = bg * (dt[:, hd:hd + 1] * jnp.exp(g_last - col))
            upd = upd + lax.dot_general(kd.astype(bf16), xm, (((0,), (0,)), ((), ())), preferred_element_type=f32)
            dl = jnp.where(lm, jnp.exp(g_last), dl)
            cdl = jnp.where(lm, par[2:3, hd:hd + 1], cdl)
        sp_ref[pr] = dl * sp + upd
        ys.append(y + cdl * xp)
    yc = jnp.concatenate(ys, axis=-1)
    o_ref[...] = _c_post(yc, cz_ref[...], nw_ref[...]).astype(bf16)

    @pl.when(c == NCHUNK - 1)
    def _():
        for pr in range(C_HEADS // 2):
            st_ref[0, 2 * pr] = sp_ref[pr][:, 0:C_HEADDIM]
            st_ref[0, 2 * pr + 1] = sp_ref[pr][:, C_HEADDIM:]


def _prompt_c(h, cw, cb, par, nw):
    return pl.pallas_call(
        _pc_body,
        out_shape=(jax.ShapeDtypeStruct((N_TOK, BW), bf16),
                   jax.ShapeDtypeStruct((BATCH, C_HEADS, C_DSTATE, C_HEADDIM), f32),
                   jax.ShapeDtypeStruct((BATCH, C_CONV - 1, C_CONV_DIM), f32)),
        grid=(BATCH, NCHUNK),
        in_specs=[_h_spec(CB_CZ), _h_spec(CB_CX0), _h_spec(CB_CX1), _SMALL_SPEC,
                  _full_spec((C_CONV, C_CONV_DIM)), _full_spec((1, C_CONV_DIM)), _full_spec((8, LANE)),
                  _full_spec((1, BW))],
        out_specs=(_BR_SPEC,
                   pl.BlockSpec((1, C_HEADS, C_DSTATE, C_HEADDIM), lambda b, c: (b, 0, 0, 0)),
                   pl.BlockSpec((1, C_CONV - 1, C_CONV_DIM), lambda b, c: (b, 0, 0))),
        scratch_shapes=[pltpu.VMEM((CHUNK + 8, C_CONV_DIM), f32), pltpu.VMEM((C_HEADS // 2, C_DSTATE, LANE), f32)],
        compiler_params=_cparams(("parallel", "arbitrary")),
        name="prompt_ssd",
    )(h, h, h, h, cw, cb, par, nw)


def _pd_body(dq_ref, dk_ref, dv_ref, do_ref, sm_ref, par_ref, nw_ref,
             o_ref, c_out, n_out, m_out, c_ref, n_ref, m_ref):
    c = pl.program_id(1)

    @pl.when(c == 0)
    def _():
        c_ref[...] = jnp.zeros_like(c_ref)
        n_ref[...] = jnp.zeros_like(n_ref)
        m_ref[...] = jnp.zeros_like(m_ref)

    sm = sm_ref[...] + par_ref[0:1, :]
    ls = jax.nn.log_sigmoid(sm)
    tri = _tri(CHUNK)
    bc = jnp.dot(tri.astype(f32), ls, precision=HI, preferred_element_type=f32)
    bct = bc.T
    smt = sm.T
    nw = nw_ref[...]
    for hd in range(D_HEADS):
        hs = slice(hd * LANE, (hd + 1) * LANE)
        li, lf = 8 + hd, 12 + hd
        bcol = bc[:, lf:lf + 1]
        brow = bct[lf:lf + 1, :]
        irow = smt[li:li + 1, :]
        icol = sm[:, li:li + 1]
        mprev = m_ref[hd:hd + 1, 0:1]
        dmat = jnp.where(tri, bcol - brow + irow, NEG_BIG)
        inter = bcol + mprev
        mt = jnp.maximum(inter, jnp.max(dmat, axis=-1, keepdims=True))
        w_intra = jnp.where(tri, jnp.exp(dmat - mt), 0.0)
        w_inter = jnp.exp(inter - mt)
        qh = dq_ref[:, hs] * (D_DK ** -0.5)
        kh = dk_ref[:, hs]
        vh = dv_ref[:, hs]
        qb = qh.astype(bf16)
        qk = lax.dot_general(qb, kh.astype(bf16), (((1,), (1,)), ((), ())), preferred_element_type=f32) * w_intra
        num = w_inter * jnp.dot(qb, c_ref[hd].astype(bf16), preferred_element_type=f32)
        num = num + jnp.dot(qk.astype(bf16), vh.astype(bf16), preferred_element_type=f32)
        den = w_inter * jnp.sum(qh * n_ref[hd:hd + 1, :], axis=-1, keepdims=True) + jnp.sum(qk, axis=-1, keepdims=True)
        hh = num / jnp.maximum(jnp.abs(den), jnp.exp(-mt))
        mnew = mt[CHUNK - 1:CHUNK, :]
        blast = bcol[CHUNK - 1:CHUNK, :]
        wk = jnp.exp(blast - bcol + icol - mnew)
        decay = jnp.exp(blast + mprev - mnew)
        wkk = wk * kh
        c_ref[hd] = decay * c_ref[hd] + lax.dot_general(wkk.astype(bf16), vh.astype(bf16), (((0,), (0,)), ((), ())),
                                                        preferred_element_type=f32)
        n_ref[hd:hd + 1, :] = decay * n_ref[hd:hd + 1, :] + jnp.sum(wkk, axis=0, keepdims=True)
        m_ref[hd:hd + 1, :] = jnp.broadcast_to(mnew, (1, LANE))
        o_ref[:, hs] = (jax.nn.sigmoid(do_ref[:, hs]) * _rms(hh, nw[:, hs])).astype(bf16)

    @pl.when(c == NCHUNK - 1)
    def _():
        c_out[0] = c_ref[...]
        n_out[0] = n_ref[...]
        m_out[0] = m_ref[...]


def _prompt_d(h, par, nw):
    return pl.pallas_call(
        _pd_body,
        out_shape=(jax.ShapeDtypeStruct((N_TOK, BW), bf16),
                   jax.ShapeDtypeStruct((BATCH, D_HEADS, LANE, LANE), f32),
                   jax.ShapeDtypeStruct((BATCH, 8, LANE), f32),
                   jax.ShapeDtypeStruct((BATCH, 8, LANE), f32)),
        grid=(BATCH, NCHUNK),
        in_specs=[_h_spec(CB_DQ), _h_spec(CB_DK), _h_spec(CB_DV), _h_spec(CB_DO), _SMALL_SPEC,
                  _full_spec((8, LANE)), _full_spec((1, BW))],
        out_specs=(_BR_SPEC,
                   pl.BlockSpec((1, D_HEADS, LANE, LANE), lambda b, c: (b, 0, 0, 0)),
                   pl.BlockSpec((1, 8, LANE), lambda b, c: (b, 0, 0)),
                   pl.BlockSpec((1, 8, LANE), lambda b, c: (b, 0, 0))),
        scratch_shapes=[pltpu.VMEM((D_HEADS, LANE, LANE), f32), pltpu.VMEM((8, LANE), f32), pltpu.VMEM((8, LANE), f32)],
        compiler_params=_cparams(("parallel", "arbitrary")),
        name="prompt_mlstm",
    )(h, h, h, h, h, par, nw)


def _sample_body(h_ref, hg_ref, ssm_ref, cv_ref, mc_ref, mn_ref, mm_ref,
                 lnw_ref, lnb_ref, ws8_ref, bs8_ref, lb_ref, bnw_ref,
                 cw_ref, cb_ref, cpar_ref, cnw_ref, dpar_ref, dnw_ref,
                 oa_ref, ob_ref, oc_ref, od_ref, chv_ref,
                 hg_out, ssm_out, cv_out, mc_out, mn_out, mm_out):
    row = lax.broadcasted_iota(i32, (8, 1), 0)
    first = row < DEC_SEQ
    tpos = row & (DEC_SEQ - 1)
    eye = lax.broadcasted_iota(i32, (LANE, LANE), 0) == lax.broadcasted_iota(i32, (LANE, LANE), 1)
    lane = lax.broadcasted_iota(i32, (1, LANE), 1)
    lo = lane < C_HEADDIM

    def tile(p, carry):
        r = pl.multiple_of(p * 8, 8)

        def col(blk, width=BW):
            return h_ref[pl.ds(r, 8), blk * BW:blk * BW + width]

        small = h_ref[pl.ds(r, 8), SMALL_COL:SMALL_COL + LANE]

        u, v = _a_uv(col(CB_AU), col(CB_AV), lnw_ref[...], lnb_ref[...])
        chv_ref[pl.ds(r, 8), :] = v
        sp = bs8_ref[...]
        for s in range(8):
            sp = sp + ws8_ref[s] * v[s:s + 1, :]
        oa_ref[pl.ds(r, 8), :] = (u * sp).astype(bf16)

        q, kb, logf = _b_pre(col(CB_BQ), col(CB_BF), lb_ref[...])
        bi = col(CB_BI)
        bg = col(CB_BG)
        for hd in range(B_HEADS):
            hs = slice(hd * LANE, (hd + 1) * LANE)
            outs = []
            for which in range(2):
                sq = 2 * p + which
                valid = first if which == 0 else jnp.logical_not(first)
                o, s_new = _hgrn_chunk(q[:, hs], kb[:, hs], bi[:, hs], logf[:, hs], hg_ref[sq, hd], 8, eye, valid)
                hg_out[sq, hd] = s_new
                outs.append(o)
            o = jnp.where(first, outs[0], outs[1])
            ob_ref[pl.ds(r, 8), hs] = _b_post(o, bg[:, hs], bnw_ref[...]).astype(bf16)

        x = h_ref[pl.ds(r, 8), CB_CX0 * BW:CB_CX0 * BW + C_CONV_DIM]
        bufs = [jnp.where(first, cv_ref[2 * p, k:k + 1, :], cv_ref[2 * p + 1, k:k + 1, :]) for k in range(3)]
        r1 = pltpu.roll(x, 1, 0)
        r2 = pltpu.roll(x, 2, 0)
        r3 = pltpu.roll(x, 3, 0)
        sh1 = jnp.where(tpos >= 1, r1, bufs[2])
        sh2 = jnp.where(tpos >= 2, r2, jnp.where(tpos == 0, bufs[1], bufs[2]))
        sh3 = jnp.where(tpos >= 3, r3, jnp.where(tpos == 0, bufs[0], jnp.where(tpos == 1, bufs[1], bufs[2])))
        xbc = _c_conv_silu(sh3, sh2, sh1, x, cw_ref, cb_ref)
        cv_out[2 * p] = pltpu.roll(x, 7, 0)[0:3, :]
        cv_out[2 * p + 1] = r3[0:3, :]
        xs = xbc[:, 0:BW]
        bm = xbc[:, BW:BW + 2 * C_DSTATE]
        cm = xbc[:, BW + 2 * C_DSTATE:]
        cpar = cpar_ref[...]
        dt = jax.nn.softplus(small + cpar[0:1, :])
        gl_all = dt * (-jnp.exp(cpar[1:2, :]))
        ys = [None] * (C_HEADS // 2)
        for which in range(2):
            sq = 2 * p + which
            valid = first if which == 0 else jnp.logical_not(first)
            g = _cumsum_rows_small(jnp.where(valid, gl_all, 0.0), 8)
            dtv = jnp.where(valid, dt, 0.0)
            dots = []
            for grp in range(C_GROUPS):
                cg = cm[:, grp * C_DSTATE:(grp + 1) * C_DSTATE]
                bgp = bm[:, grp * C_DSTATE:(grp + 1) * C_DSTATE]
                dots.append([jnp.sum(cg * bgp[s:s + 1, :], axis=-1, keepdims=True) for s in range(8)])
            for pr in range(C_HEADS // 2):
                xp = xs[:, pr * LANE:(pr + 1) * LANE]
                sp_lo = ssm_ref[sq, 2 * pr]
                sp_hi = ssm_ref[sq, 2 * pr + 1]
                spair = jnp.concatenate([sp_lo, sp_hi], axis=-1)
                y = jnp.zeros((8, LANE), f32)
                upd = jnp.zeros((C_DSTATE, LANE), f32)
                dl = jnp.zeros((1, LANE), f32)
                cdl = jnp.zeros((1, LANE), f32)
                for sub in range(2):
                    hd = 2 * pr + sub
                    grp = hd // (C_HEADS // C_GROUPS)
                    lm = lo if sub == 0 else jnp.logical_not(lo)
                    gcol = g[:, hd:hd + 1]
                    g_last = gcol[7:8, :]
                    xm = jnp.where(lm, xp, 0.0)
                    for s in range(8):
                        coef = dots[grp][s] * jnp.where(row >= s, jnp.exp(jnp.where(row >= s, gcol - gcol[s:s + 1, :], 0.0))
                                                        * dtv[s:s + 1, hd:hd + 1], 0.0)
                        y = y + coef * xm[s:s + 1, :]
                    cg = cm[:, grp * C_DSTATE:(grp + 1) * C_DSTATE]
                    bgp = bm[:, grp * C_DSTATE:(grp + 1) * C_DSTATE]
                    y = y + jnp.dot((cg * jnp.exp(gcol)).astype(bf16), jnp.where(lm, spair, 0.0).astype(bf16),
                                    preferred_element_type=f32)
                    kd = bgp * (dtv[:, hd:hd + 1] * jnp.exp(g_last - gcol))
                    upd = upd + lax.dot_general(kd.astype(bf16), xm.astype(bf16), (((0,), (0,)), ((), ())),
                                                preferred_element_type=f32)
                    dl = jnp.where(lm, jnp.exp(g_last), dl)
                    cdl = jnp.where(lm, cpar[2:3, hd:hd + 1], cdl)
                snew = dl * spair + upd
                ssm_out[sq, 2 * pr] = snew[:, 0:C_HEADDIM]
                ssm_out[sq, 2 * pr + 1] = snew[:, C_HEADDIM:]
                y = y + cdl * xp
                ys[pr] = y if which == 0 else jnp.where(first, ys[pr], y)
        yc = jnp.concatenate(ys, axis=-1)
        oc_ref[pl.ds(r, 8), :] = _c_post(yc, col(CB_CZ), cnw_ref[...]).astype(bf16)

        smd = small + dpar_ref[0:1, :]
        lsd = jax.nn.log_sigmoid(smd)
        dq = col(CB_DQ)
        dk = col(CB_DK)
        dv = col(CB_DV)
        do = col(CB_DO)
        dnw = dnw_ref[...]
        hs_out = [None] * D_HEADS
        for which in range(2):
            sq = 2 * p + which
            valid = first if which == 0 else jnp.logical_not(first)
            bcs = _cumsum_rows_small(jnp.where(valid, lsd, 0.0), 8)
            igv = jnp.where(valid, smd, NEG_BIG)
            mrow = mm_ref[pl.ds(sq, 1), :]
            mnew_row = jnp.zeros((1, D_HEADS), f32)
            hlane = lax.broadcasted_iota(i32, (1, D_HEADS), 1)
            for hd in range(D_HEADS):
                hs = slice(hd * LANE, (hd + 1) * LANE)
                li, lf = 8 + hd, 12 + hd
                bcol = bcs[:, lf:lf + 1]
                icol = igv[:, li:li + 1]
                mprev = mrow[:, hd:hd + 1]
                dcols = [jnp.where(row >= s, bcol - bcol[s:s + 1, :] + icol[s:s + 1, :], NEG_BIG) for s in range(8)]
                mx = dcols[0]
                for s in range(1, 8):
                    mx = jnp.maximum(mx, dcols[s])
                inter = bcol + mprev
                mt = jnp.maximum(inter, mx)
                w_inter = jnp.exp(inter - mt)
                qh = dq[:, hs] * (D_DK ** -0.5)
                kh = dk[:, hs]
                vh = dv[:, hs]
                cmat = mc_ref[sq, hd]
                nrow = mn_ref[sq, pl.ds(hd, 1), :]
                num = w_inter * jnp.dot(qh.astype(bf16), cmat.astype(bf16), preferred_element_type=f32)
                den = w_inter * jnp.sum(qh * nrow, axis=-1, keepdims=True)
                for s in range(8):
                    w = jnp.where(row >= s, jnp.exp(dcols[s] - mt), 0.0)
                    qk = jnp.sum(qh * kh[s:s + 1, :], axis=-1, keepdims=True) * w
                    num = num + qk * vh[s:s + 1, :]
                    den = den + qk
                hh = num / jnp.maximum(jnp.abs(den), jnp.exp(-mt))
                mnew = mt[7:8, :]
                blast = bcol[7:8, :]
                wk = jnp.exp(blast - bcol + icol - mnew)
                decay = jnp.exp(blast + mprev - mnew)
                wkk = wk * kh
                mc_out[sq, hd] = decay * cmat + lax.dot_general(wkk.astype(bf16), vh.astype(bf16), (((0,), (0,)), ((), ())),
                                                                preferred_element_type=f32)
                mn_out[sq, pl.ds(hd, 1), :] = decay * nrow + jnp.sum(wkk, axis=0, keepdims=True)
                mnew_row = jnp.where(hlane == hd, mnew, mnew_row)
                o = jax.nn.sigmoid(do[:, hs]) * _rms(hh, dnw[:, hs])
                hs_out[hd] = o if which == 0 else jnp.where(first, hs_out[hd], o)
            mm_out[pl.ds(sq, 1), :] = mnew_row
        for hd in range(D_HEADS):
            od_ref[pl.ds(r, 8), hd * LANE:(hd + 1) * LANE] = hs_out[hd].astype(bf16)
        return carry

    lax.fori_loop(0, SB // 2, tile, 0)


_N_SAMPLE_IN = 19


def _sample_body_aliased(n_alias, *refs):
    _sample_body(*refs[:_N_SAMPLE_IN], *refs[_N_SAMPLE_IN + n_alias:])


def _sample_mixer(h, st_hgrn, st_ssm, st_conv, st_c, st_n, st_m, pa, pb, pc, pd, layer, brs, stacked):
    rows = SB * DEC_SEQ
    row0 = N_PROMPT // rows

    def blk(shape):
        nd = len(shape)
        return pl.BlockSpec((SB,) + shape, lambda i, nd=nd: (i,) + (0,) * nd)

    def blk_stacked(shape):
        nd = len(shape)
        return pl.BlockSpec((None, SB) + shape, lambda i, nd=nd: (layer, i) + (0,) * nd)

    def full(shape):
        nd = len(shape)
        return pl.BlockSpec(shape, lambda i, nd=nd: (0,) * nd)

    big = ((B_HEADS, LANE, LANE), (C_HEADS, C_DSTATE, C_HEADDIM), (D_HEADS, LANE, LANE))
    in_state_specs = [blk_stacked(big[0]), blk_stacked(big[1]), blk_stacked((C_CONV - 1, C_CONV_DIM)),
                      blk_stacked(big[2]), blk_stacked((D_HEADS, LANE)), blk_stacked((D_HEADS,))]
    out_state_specs = [blk_stacked(big[0]), blk_stacked(big[1]), blk((C_CONV - 1, C_CONV_DIM)), blk_stacked(big[2]),
                       blk((D_HEADS, LANE)), blk((D_HEADS,))]
    params = list(pa) + list(pb) + list(pc) + list(pd)
    br_spec = pl.BlockSpec((rows, BW), lambda i: (row0 + i, 0))
    br_shape = jax.ShapeDtypeStruct((N_TOK, BW), bf16)
    stacked_shape = lambda a: jax.ShapeDtypeStruct(a.shape, f32)
    per_layer = lambda a: jax.ShapeDtypeStruct(a.shape[1:], f32)
    inputs = [h, st_hgrn, st_ssm, st_conv, st_c, st_n, st_m] + params
    in_specs = [pl.BlockSpec((rows, MIX_W), lambda i: (row0 + i, 0))] + in_state_specs + [full(p.shape) for p in params]
    assert len(inputs) == _N_SAMPLE_IN
    alias_in = list(brs) + (list(stacked) if stacked is not None else [])
    alias_out = [0, 1, 2, 3] + ([5, 6, 8] if stacked is not None else [])
    aliases = {_N_SAMPLE_IN + k: o for k, o in enumerate(alias_out)}
    return pl.pallas_call(
        functools.partial(_sample_body_aliased, len(alias_in)),
        out_shape=(br_shape, br_shape, br_shape, br_shape, jax.ShapeDtypeStruct((N_SAMPLE, BW), f32),
                   stacked_shape(st_hgrn), stacked_shape(st_ssm),
                   per_layer(st_conv), stacked_shape(st_c), per_layer(st_n), per_layer(st_m)),
        grid=(DEC_BATCH // SB,),
        in_specs=in_specs + [pl.BlockSpec(memory_space=pl.ANY)] * len(alias_in),
        out_specs=(br_spec, br_spec, br_spec, br_spec, pl.BlockSpec((rows, BW), lambda i: (i, 0)))
        + tuple(out_state_specs),
        input_output_aliases=aliases,
        compiler_params=_cparams(("parallel",)),
        name="sample_mixer",
    )(*inputs, *alias_in)


def _merge_body(x_ref, nw_ref, ba_ref, bb_ref, bc_ref, bd_ref, wg_ref, wb_ref, o_ref, xn_ref, acc_ref):
    jc = pl.program_id(1)
    n = pl.program_id(2)

    @pl.when((jc == 0) & (n == 0))
    def _():
        xn_ref[...] = _rms(x_ref[...], nw_ref[...]).astype(bf16)

    gate = jax.nn.sigmoid(jnp.dot(xn_ref[...], wg_ref[...], preferred_element_type=f32))
    for k, br_ref in enumerate((ba_ref, bb_ref, bc_ref, bd_ref)):
        @pl.when(n == k)
        def _(br_ref=br_ref, k=k):
            contrib = gate * jnp.dot(br_ref[...], wb_ref[0], preferred_element_type=f32)
            if k == 0:
                acc_ref[...] = contrib
            else:
                acc_ref[...] = acc_ref[...] + contrib

    @pl.when(n == N_BRANCH - 1)
    def _():
        o_ref[...] = acc_ref[...].astype(bf16)


def _merge(x, nw, brs, w_gate, w_branch, tm, tn, layer):
    m = x.shape[0]
    ncol = D_MODEL // tn
    br_spec = pl.BlockSpec((tm, BW), lambda i, jc, n: (i, 0), pipeline_mode=pl.Buffered(1))
    return pl.pallas_call(
        _merge_body,
        out_shape=jax.ShapeDtypeStruct((m, D_MODEL), bf16),
        grid=(m // tm, ncol, N_BRANCH),
        in_specs=[pl.BlockSpec((tm, D_MODEL), lambda i, jc, n: (i, 0), pipeline_mode=pl.Buffered(1)),
                  pl.BlockSpec((1, D_MODEL), lambda i, jc, n: (0, 0)),
                  br_spec, br_spec, br_spec, br_spec,
                  pl.BlockSpec((D_MODEL, tn), lambda i, jc, n: (0, n * ncol + jc)),
                  pl.BlockSpec((None, 1, BW, tn), lambda i, jc, n: (layer, n, 0, jc))],
        out_specs=pl.BlockSpec((tm, tn), lambda i, jc, n: (i, jc)),
        scratch_shapes=[pltpu.VMEM((tm, D_MODEL), bf16), pltpu.VMEM((tm, tn), f32)],
        compiler_params=_cparams(("parallel", "arbitrary", "arbitrary")),
        name="merge",
    )(x, nw, *brs, w_gate, w_branch)


def _top16_rows(s, rid=None):
    if rid is None:
        rid = lax.broadcasted_iota(i32, s.shape, 0).astype(f32)
    out = []
    for _ in range(PEER_TOPK):
        m = jnp.max(s, axis=0, keepdims=True)
        am = jnp.min(jnp.where(s == m, rid, float(PEER_N)), axis=0, keepdims=True)
        out.append((m, am))
        s = jnp.where(rid == am, -jnp.inf, s)
    return out


def _collect16(pairs, tb):
    r16 = lax.broadcasted_iota(i32, (PEER_TOPK, tb), 0)
    v = jnp.zeros((PEER_TOPK, tb), f32)
    ix = jnp.zeros((PEER_TOPK, tb), f32)
    for k, (m, am) in enumerate(pairs):
        v = jnp.where(r16 == k, m, v)
        ix = jnp.where(r16 == k, am, ix)
    return v, ix.astype(i32)


def _peer_route_body(q_ref, keys_ref, ia_ref, ib_ref, gt_ref):
    tb = q_ref.shape[0]
    vals = []
    idxs = []
    for p in range(2):
        st = lax.dot_general(keys_ref[0, p], q_ref[:, p * LANE:(p + 1) * LANE], (((1,), (1,)), ((), ())),
                             preferred_element_type=f32)
        v, ix = _collect16(_top16_rows(st), tb)
        vals.append(v)
        idxs.append(ix)
    va, vb = vals
    half = PEER_TOPK // 2
    cand = jnp.concatenate([va[0:1, :] + vb] + [va[k:k + 1, :] + vb[0:half, :] for k in range(1, half)]
                           + [va[half:, :] + vb[0:1, :]], axis=0)
    r = lax.broadcasted_iota(i32, cand.shape, 0)
    r2 = r - PEER_TOPK
    mid = PEER_TOPK * (1 + (r2 >> 3)) + (r2 & (half - 1))
    pid = jnp.where(r < PEER_TOPK, r, jnp.where(r < PEER_TOPK + half * (half - 1), mid, (r - 8 * half) * PEER_TOPK))
    fs, pos = _collect16(_top16_rows(cand, pid.astype(f32)), tb)
    ka = pos >> 4
    kb = pos & (PEER_TOPK - 1)
    i1 = jnp.zeros((PEER_TOPK, tb), i32)
    i2 = jnp.zeros((PEER_TOPK, tb), i32)
    for j in range(PEER_TOPK):
        i1 = jnp.where(ka == j, idxs[0][j:j + 1, :], i1)
        i2 = jnp.where(kb == j, idxs[1][j:j + 1, :], i2)
    pe = jnp.exp(fs - fs[0:1, :])
    ia_ref[...] = i1
    ib_ref[...] = i2
    gt_ref[...] = pe / jnp.sum(pe, axis=0, keepdims=True)


def _peer_route(q, keys, tb, layer):
    m = q.shape[0]
    spec = pl.BlockSpec((PEER_TOPK, tb), lambda i, hd: (hd, i))
    nslot = PEER_HEADS * PEER_TOPK
    return pl.pallas_call(
        _peer_route_body,
        out_shape=(jax.ShapeDtypeStruct((nslot, m), i32), jax.ShapeDtypeStruct((nslot, m), i32),
                   jax.ShapeDtypeStruct((nslot, m), f32)),
        grid=(m // tb, PEER_HEADS),
        in_specs=[pl.BlockSpec((tb, 2 * LANE), lambda i, hd: (i, hd)),
                  pl.BlockSpec((None, 1, 2, PEER_NKEYS, LANE), lambda i, hd: (layer, hd, 0, 0, 0))],
        out_specs=(spec, spec, spec),
        compiler_params=_cparams(("parallel", "parallel")),
        name="peer_route",
    )(q, keys)


GATE_UNROLL = 8


def _peer_gates_body(ia_ref, ib_ref, gt_ref, o_ref, a_s, b_s, g_s, gm_s):
    tg = ia_ref.shape[1]
    a_s[...] = ia_ref[...].astype(f32).T
    b_s[...] = ib_ref[...].astype(f32).T
    g_s[...] = gt_ref[...].T
    sub = lax.broadcasted_iota(i32, (LANE, LANE), 0).astype(f32)

    def step(r, carry):
        r8 = pl.multiple_of(r * GATE_UNROLL, GATE_UNROLL)
        a8 = a_s[pl.ds(r8, GATE_UNROLL), :]
        b8 = b_s[pl.ds(r8, GATE_UNROLL), :]
        g8 = g_s[pl.ds(r8, GATE_UNROLL), :]
        for u in range(GATE_UNROLL):
            at = jnp.where(sub == a8[u:u + 1, :], 1.0, 0.0).astype(bf16)
            bt = jnp.where(sub == b8[u:u + 1, :], g8[u:u + 1, :], 0.0).astype(bf16)
            gm = lax.dot_general(at, bt, (((1,), (1,)), ((), ())), preferred_element_type=f32)
            gm_s[pl.ds(pl.multiple_of((r8 + u) * PEER_NKEYS, PEER_NKEYS), PEER_NKEYS), :] = gm
        return carry

    lax.fori_loop(0, tg // GATE_UNROLL, step, 0)

    def relayout(i1, carry):
        blk = gm_s[pl.ds(i1, tg, stride=PEER_NKEYS), :]
        o_ref[:, pl.ds(pl.multiple_of(i1 * PEER_NKEYS, PEER_NKEYS), PEER_NKEYS)] = blk.astype(bf16)
        return carry

    lax.fori_loop(0, PEER_NKEYS, relayout, 0)


def _peer_gates(ia, ib, gt, tg):
    nslot, m = ia.shape
    spec = pl.BlockSpec((nslot, tg), lambda i: (0, i))
    return pl.pallas_call(
        _peer_gates_body,
        out_shape=jax.ShapeDtypeStruct((m, PEER_N), bf16),
        grid=(m // tg,),
        in_specs=[spec, spec, spec],
        out_specs=pl.BlockSpec((tg, PEER_N), lambda i: (i, 0)),
        scratch_shapes=[pltpu.VMEM((tg, nslot), f32)] * 3 + [pltpu.VMEM((tg * PEER_NKEYS, PEER_NKEYS), f32)],
        compiler_params=_cparams(("parallel",)),
        name="peer_gates",
    )(ia, ib, gt)


def _peer_experts_body(x_ref, nw_ref, u_ref, v_ref, g_ref, o_ref, xn_ref):
    @pl.when(pl.program_id(1) == 0)
    def _():
        x = x_ref[...]
        xn_ref[...] = _rms(x, nw_ref[...]).astype(bf16)
        o_ref[...] = x

    hmat = lax.dot_general(xn_ref[...], u_ref[...], (((1,), (1,)), ((), ())), preferred_element_type=f32)
    w = (_gelu(hmat) * g_ref[...].astype(f32)).astype(bf16)
    o_ref[...] += jnp.dot(w, v_ref[...], preferred_element_type=f32)


def _peer_experts(x, nw, u, v, g, tb, eb, layer):
    m = x.shape[0]
    return pl.pallas_call(
        _peer_experts_body,
        out_shape=jax.ShapeDtypeStruct((m, D_MODEL), f32),
        grid=(m // tb, PEER_N // eb),
        in_specs=[pl.BlockSpec((tb, D_MODEL), lambda i, j: (i, 0), pipeline_mode=pl.Buffered(1)),
                  pl.BlockSpec((1, D_MODEL), lambda i, j: (0, 0)),
                  pl.BlockSpec((None, eb, D_MODEL), lambda i, j: (layer, j, 0)),
                  pl.BlockSpec((None, eb, D_MODEL), lambda i, j: (layer, j, 0)),
                  pl.BlockSpec((tb, eb), lambda i, j: (i, j))],
        out_specs=pl.BlockSpec((tb, D_MODEL), lambda i, j: (i, 0), pipeline_mode=pl.Buffered(1)),
        scratch_shapes=[pltpu.VMEM((tb, D_MODEL), bf16)],
        compiler_params=_cparams(("parallel", "arbitrary")),
        name="peer_experts",
    )(x, nw, u, v, g)


_IN_OFFS = tuple(int(v) for v in np.cumsum((0,) + IN_SPLITS))
_O_CDT, _O_DQ, _O_DIG, _O_GATES, _O_END = _IN_OFFS[8], _IN_OFFS[9], _IN_OFFS[13], _IN_OFFS[15], _IN_OFFS[16]
PREP_ROWS = 128


def _prep_w_in_body(w_ref, om_ref, og_ref):
    rows = w_ref.shape[0]
    om_ref[:, 0:_O_CDT] = w_ref[:, 0:_O_CDT].astype(bf16)
    om_ref[:, _O_CDT:SMALL_COL] = w_ref[:, _O_DQ:_O_DIG].astype(bf16)
    small = jnp.concatenate([w_ref[:, _O_CDT:_O_DQ], w_ref[:, _O_DIG:_O_GATES],
                             jnp.zeros((rows, LANE - 16), w_ref.dtype)], axis=-1)
    om_ref[:, SMALL_COL:SMALL_COL + LANE] = small.astype(bf16)
    om_ref[:, SMALL_COL + LANE:MIX_W] = jnp.zeros((rows, MIX_W - SMALL_COL - LANE), bf16)
    og_ref[...] = w_ref[:, _O_GATES:_O_END].astype(bf16)


def _prep_w_in(w, layer):
    _, k, n = w.shape
    ng = _O_END - _O_GATES
    return pl.pallas_call(
        _prep_w_in_body,
        out_shape=(jax.ShapeDtypeStruct((k, MIX_W), bf16), jax.ShapeDtypeStruct((k, ng), bf16)),
        grid=(k // PREP_ROWS,),
        in_specs=[pl.BlockSpec((None, PREP_ROWS, n), lambda i: (layer, i, 0))],
        out_specs=(pl.BlockSpec((PREP_ROWS, MIX_W), lambda i: (i, 0)), pl.BlockSpec((PREP_ROWS, ng), lambda i: (i, 0))),
        compiler_params=_cparams(("parallel",)),
        name="prep_w_in",
    )(w)


def _lane_row(vals, start):
    row = jnp.zeros((LANE,), f32)
    return row.at[start:start + vals.shape[0]].set(vals)


def kernel(x_prompt, x_sample, state_hgrn, state_ssm, state_conv, state_mlstm_c, state_mlstm_n, state_mlstm_m, norm1_w, w_in, a_ln_w, a_ln_b, a_ws, a_bs, b_lb_logits, b_norm_w, c_conv_w, c_conv_b, c_dt_bias, c_a_log, c_d, c_norm_w, d_ig_b, d_fg_b, d_norm_w, w_branch, w_out, norm2_w, peer_wq, peer_keys, peer_u, peer_v, final_norm_w):
    x = jnp.concatenate([x_prompt.reshape(N_PROMPT, D_MODEL), x_sample.reshape(N_SAMPLE, D_MODEL)], axis=0)
    lbs = jax.nn.softmax(b_lb_logits.astype(f32), axis=0)
    lbs = jnp.cumsum(lbs, axis=0) - lbs[0]
    zeros8 = jnp.zeros((8, LANE), f32)
    news_p = []
    news_s = []
    stacked_s = None
    w_in_b, w_branch_b, w_out_b, wq_b = (w.astype(bf16) for w in (w_in, w_branch, w_out, peer_wq))
    keys_b, u_b, v_b = (w.astype(bf16) for w in (peer_keys, peer_u, peer_v))
    for l in range(DEPTH):
        w_mix, w_gate = _prep_w_in(w_in_b, l)
        h = _norm_mm(x, norm1_w[l][None, :], w_mix, f32, 1088, 512)

        lnw = a_ln_w[l][None, :]
        lnb = a_ln_b[l][None, :]
        lb = lbs[l][None, :]
        bnw = b_norm_w[l][None, :]
        cw = c_conv_w[l]
        cb = c_conv_b[l][None, :]
        cpar = zeros8.at[0].set(_lane_row(c_dt_bias[l], 0)).at[1].set(_lane_row(c_a_log[l], 0)).at[2].set(_lane_row(c_d[l], 0))
        cnw = c_norm_w[l][None, :]
        dpar = zeros8.at[0].set(_lane_row(d_ig_b[l], 8) + _lane_row(d_fg_b[l], 12))
        dnw = d_norm_w[l][None, :]

        bs_full = jnp.repeat(a_bs[l].T, LANE, axis=1)
        br_a = _prompt_a(h, lnw, lnb, a_ws[l], bs_full)
        br_b, hg_p = _prompt_b(h, lb, bnw)
        br_c, ssm_p, cv_p = _prompt_c(h, cw, cb, cpar, cnw)
        br_d, mc_p, mn_p, mm_p = _prompt_d(h, dpar, dnw)
        news_p.append((hg_p, ssm_p, cv_p, mc_p, mn_p[:, :D_HEADS, :], mm_p[:, :D_HEADS, 0]))

        w4 = jnp.tril(a_ws[l][:, :DEC_SEQ, :DEC_SEQ])
        w8 = jnp.zeros((A_GROUPS, 8, 8), f32).at[:, :4, :4].set(w4).at[:, 4:, 4:].set(w4)
        ws8 = jnp.repeat(jnp.transpose(w8, (2, 1, 0)), LANE, axis=2)
        bs8 = jnp.repeat(jnp.tile(a_bs[l][:, :DEC_SEQ], (1, 2)).T, LANE, axis=1)
        outs = _sample_mixer(h, state_hgrn, state_ssm, state_conv, state_mlstm_c, state_mlstm_n, state_mlstm_m,
                             (lnw, lnb, ws8, bs8), (lb, bnw), (cw, cb, cpar, cnw), (dpar, dnw), l,
                             (br_a, br_b, br_c, br_d), stacked_s)
        brs = outs[:4]
        stacked_s = (outs[5], outs[6], outs[8])
        news_s.append((outs[7], outs[9], outs[10], outs[4].reshape(DEC_BATCH, DEC_SEQ, BW)))

        mixin = _merge(x, norm1_w[l][None, :], brs, w_gate, w_branch_b, 1088, 1024, l)
        x = _mm_res(mixin, w_out_b, x, 1088, 512, l)

        q = _norm_mm(x, norm2_w[l][None, :], wq_b, bf16, 1088, 512, l)
        ia, ib, gt = _peer_route(q, keys_b, 512, l)
        g = _peer_gates(ia, ib, gt, 128)
        x = _peer_experts(x, norm2_w[l][None, :], u_b, v_b, g, 1088, 512, l)

    y_p, y_s = _final_norm(x, final_norm_w[None, :])
    y_prompt = y_p.reshape(BATCH, SEQ, D_MODEL)
    y_sample = y_s.reshape(DEC_BATCH, DEC_SEQ, D_MODEL)
    stack = lambda news, k: jnp.stack([n[k] for n in news], axis=0)
    hgrn_s, ssm_s, mc_s = stacked_s
    conv_s, mn_s, mm_s, chunk_v_s = (stack(news_s, k) for k in range(4))
    return ((y_prompt, y_sample) + tuple(stack(news_p, k) for k in range(6))
            + (hgrn_s, ssm_s, conv_s, mc_s, mn_s, mm_s, chunk_v_s))
```

```python
import functools

import jax
import jax.numpy as jnp
import numpy as np
from jax import lax
from jax.experimental import pallas as pl
from jax.experimental.pallas import tpu as pltpu

f32 = jnp.float32
bf16 = jnp.bfloat16
i32 = jnp.int32
HI = lax.Precision.HIGHEST

D_MODEL = 2048
BATCH = 4
SEQ = 2048
DEPTH = 2
DEC_BATCH = 128
DEC_SEQ = 4
N_BRANCH = 4
BW = 512
A_GROUPS = 4
A_CHUNK = 128
B_HEADS = 4
LB_FLOOR = 1e-30
C_HEADDIM = 64
C_HEADS = 8
C_GROUPS = 2
C_DSTATE = 128
C_CONV = 4
C_CONV_DIM = 1024
D_HEADS = 4
D_DK = 128
NEG_BIG = -1e30
IN_SPLITS = (512, 512, 512, 512, 512, 512, 512, 1024, 8, 512, 512, 512, 512, 4, 4, 8192)
PEER_NKEYS = 128
PEER_N = PEER_NKEYS * PEER_NKEYS
PEER_HEADS = 8
PEER_TOPK = 16
EPS = 1e-6
INV_SQRT2 = 0.7071067811865476

N_PROMPT = BATCH * SEQ
N_SAMPLE = DEC_BATCH * DEC_SEQ
N_TOK = N_PROMPT + N_SAMPLE

CB_AU, CB_AV, CB_BQ, CB_BF, CB_BI, CB_BG, CB_CZ, CB_CX0, CB_CX1, CB_DQ, CB_DK, CB_DV, CB_DO = range(13)
SMALL_COL = 13 * 512
MIX_W = 14 * 512
LANE = 128
CHUNK = 128
NCHUNK = SEQ // CHUNK
SUB = 16
SB = 8
VMEM_LIMIT = 56 * 1024 * 1024


def _gelu(x):
    return 0.5 * x * (1.0 + lax.erf(x * INV_SQRT2))


def _rms(x, w):
    ms = jnp.mean(x * x, axis=-1, keepdims=True)
    return x * lax.rsqrt(ms + EPS) * w


def _tri(n):
    r = lax.broadcasted_iota(i32, (n, n), 0)
    c = lax.broadcasted_iota(i32, (n, n), 1)
    return r >= c


def _cumsum_rows_small(x, n):
    row = lax.broadcasted_iota(i32, (n, 1), 0)
    acc = jnp.zeros_like(x)
    for s in range(n):
        acc = acc + jnp.where(row >= s, x[s:s + 1, :], 0.0)
    return acc


def _row_to_col(row, eye):
    return jnp.sum(jnp.where(eye, row, 0.0), axis=-1, keepdims=True)


def _cparams(sem, vmem=VMEM_LIMIT):
    return pltpu.CompilerParams(dimension_semantics=sem, vmem_limit_bytes=vmem)


def _norm_mm_body(x_ref, nw_ref, w_ref, o_ref, xn_ref):
    @pl.when(pl.program_id(1) == 0)
    def _():
        xn_ref[...] = _rms(x_ref[...], nw_ref[...]).astype(bf16)

    o_ref[...] = jnp.dot(xn_ref[...], w_ref[...], preferred_element_type=f32).astype(o_ref.dtype)


def _w_spec(w, layer, block, index_map):
    if layer is None:
        return pl.BlockSpec(block, index_map)
    return pl.BlockSpec((None,) + block, lambda *g: (layer,) + index_map(*g))


def _norm_mm(x, nw, w, out_dtype, tm, tn, layer=None):
    m, k = x.shape
    n = w.shape[-1]
    return pl.pallas_call(
        _norm_mm_body,
        out_shape=jax.ShapeDtypeStruct((m, n), out_dtype),
        grid=(m // tm, n // tn),
        in_specs=[pl.BlockSpec((tm, k), lambda i, j: (i, 0), pipeline_mode=pl.Buffered(1)),
                  pl.BlockSpec((1, k), lambda i, j: (0, 0)),
                  _w_spec(w, layer, (k, tn), lambda i, j: (0, j))],
        out_specs=pl.BlockSpec((tm, tn), lambda i, j: (i, j)),
        scratch_shapes=[pltpu.VMEM((tm, k), bf16)],
        compiler_params=_cparams(("parallel", "arbitrary")),
        name="norm_mm",
    )(x, nw, w)


def _mm_res_body(a_ref, w_ref, r_ref, o_ref):
    o_ref[...] = r_ref[...] + jnp.dot(a_ref[...], w_ref[...], preferred_element_type=f32)


def _mm_res(a, w, res, tm, tn, layer=None):
    m, k = a.shape
    n = w.shape[-1]
    return pl.pallas_call(
        _mm_res_body,
        out_shape=jax.ShapeDtypeStruct((m, n), f32),
        grid=(m // tm, n // tn),
        in_specs=[pl.BlockSpec((tm, k), lambda i, j: (i, 0)),
                  _w_spec(w, layer, (k, tn), lambda i, j: (0, j)),
                  pl.BlockSpec((tm, tn), lambda i, j: (i, j))],
        out_specs=pl.BlockSpec((tm, tn), lambda i, j: (i, j)),
        compiler_params=_cparams(("parallel", "arbitrary")),
        name="mm_res",
    )(a, w, res)


def _final_norm_body(x_ref, w_ref, op_ref, os_ref):
    y = _rms(x_ref[...], w_ref[...])
    i = pl.program_id(0)

    @pl.when(i < N_PROMPT // N_SAMPLE)
    def _():
        op_ref[...] = y

    @pl.when(i == N_PROMPT // N_SAMPLE)
    def _():
        os_ref[...] = y


def _final_norm(x, w):
    m, k = x.shape
    tm = N_SAMPLE
    last_p = N_PROMPT // tm - 1
    return pl.pallas_call(
        _final_norm_body,
        out_shape=(jax.ShapeDtypeStruct((N_PROMPT, k), f32), jax.ShapeDtypeStruct((N_SAMPLE, k), f32)),
        grid=(m // tm,),
        in_specs=[pl.BlockSpec((tm, k), lambda i: (i, 0)), pl.BlockSpec((1, k), lambda i: (0, 0))],
        out_specs=(pl.BlockSpec((tm, k), lambda i: (jnp.minimum(i, last_p), 0)),
                   pl.BlockSpec((tm, k), lambda i: (0, 0))),
        compiler_params=_cparams(("arbitrary",)),
        name="final_norm",
    )(x, w)


def _a_uv(au, av, lnw, lnb):
    u = _gelu(au)
    g = _gelu(av)
    xc = g - jnp.mean(g, axis=-1, keepdims=True)
    var = jnp.mean(xc * xc, axis=-1, keepdims=True)
    v = xc * lax.rsqrt(var + EPS) * lnw + lnb
    return u, v


def _b_pre(bq, bf_, lb):
    q = bq * jax.nn.sigmoid(bq)
    logf = jnp.logaddexp(jnp.log(jnp.maximum(lb, LB_FLOOR)), jnp.log1p(-lb) + jax.nn.log_sigmoid(bf_))
    kb = (1.0 - lb) * jax.nn.sigmoid(-bf_)
    return q, kb, logf


def _hgrn_chunk(q, k, v, gl, s_mat, n, eye, valid=None):
    if valid is not None:
        gl = jnp.where(valid, gl, 0.0)
        k = jnp.where(valid, k, 0.0)
    g = _cumsum_rows_small(gl, n)
    g_last = g[n - 1:n, :]
    o = jnp.dot((q * jnp.exp(g)).astype(bf16), s_mat.astype(bf16), preferred_element_type=f32)
    row = lax.broadcasted_iota(i32, (n, 1), 0)
    for s in range(n):
        m = row >= s
        d = jnp.where(m, g - g[s:s + 1, :], 0.0)
        p = jnp.where(m, q * k[s:s + 1, :] * jnp.exp(d), 0.0)
        o = o + jnp.sum(p, axis=-1, keepdims=True) * v[s:s + 1, :]
    k_dec = k * jnp.exp(g_last - g)
    upd = lax.dot_general(k_dec.astype(bf16), v.astype(bf16), (((0,), (0,)), ((), ())), preferred_element_type=f32)
    s_new = jnp.exp(_row_to_col(g_last, eye)) * s_mat + upd
    return o, s_new


def _b_post(o, bg, nw):
    return _rms(o, nw) * (bg * jax.nn.sigmoid(bg))


def _hrow(b, c):
    return b * NCHUNK + c


def _h_spec(colblk):
    return pl.BlockSpec((CHUNK, BW), lambda b, c, cb=colblk: (_hrow(b, c), cb))


_SMALL_SPEC = pl.BlockSpec((CHUNK, LANE), lambda b, c: (_hrow(b, c), SMALL_COL // LANE))
_BR_SPEC = pl.BlockSpec((CHUNK, BW), lambda b, c: (_hrow(b, c), 0))


def _full_spec(shape):
    nd = len(shape)
    return pl.BlockSpec(shape, lambda b, c, nd=nd: (0,) * nd)


def _pa_body(au_ref, av_ref, lnw_ref, lnb_ref, ws_ref, bs_ref, o_ref):
    u, v = _a_uv(au_ref[...], av_ref[...], lnw_ref[...], lnb_ref[...])
    tri = _tri(CHUNK)
    vb = v.astype(bf16)
    parts = []
    for g in range(A_GROUPS):
        w = jnp.where(tri, ws_ref[g], 0.0).astype(bf16)
        parts.append(jnp.dot(w, vb[:, g * LANE:(g + 1) * LANE], preferred_element_type=f32))
    sp = jnp.concatenate(parts, axis=-1) + bs_ref[...]
    o_ref[...] = (u * sp).astype(bf16)


def _prompt_a(h, lnw, lnb, ws, bs_full):
    return pl.pallas_call(
        _pa_body,
        out_shape=jax.ShapeDtypeStruct((N_TOK, BW), bf16),
        grid=(BATCH, NCHUNK),
        in_specs=[_h_spec(CB_AU), _h_spec(CB_AV), _full_spec((1, BW)), _full_spec((1, BW)),
                  _full_spec((A_GROUPS, A_CHUNK, A_CHUNK)), _full_spec((A_CHUNK, BW))],
        out_specs=_BR_SPEC,
        compiler_params=_cparams(("parallel", "parallel")),
        name="prompt_gmlp",
    )(h, h, lnw, lnb, ws, bs_full)


def _pb_body(bq_ref, bf_ref, bi_ref, bg_ref, lb_ref, nw_ref, o_ref, st_ref, s_ref, q_s, k_s, v_s, g_s, o_s):
    c = pl.program_id(1)

    @pl.when(c == 0)
    def _():
        s_ref[...] = jnp.zeros_like(s_ref)

    q, kb, logf = _b_pre(bq_ref[...], bf_ref[...], lb_ref[...])
    q_s[...] = q
    k_s[...] = kb
    v_s[...] = bi_ref[...]
    g_s[...] = logf
    eye = lax.broadcasted_iota(i32, (LANE, LANE), 0) == lax.broadcasted_iota(i32, (LANE, LANE), 1)
    def sub(j, carry):
        r = pl.multiple_of(j * SUB, SUB)
        for hd in range(B_HEADS):
            hs = slice(hd * LANE, (hd + 1) * LANE)
            o, s_new = _hgrn_chunk(q_s[pl.ds(r, SUB), hs], k_s[pl.ds(r, SUB), hs], v_s[pl.ds(r, SUB), hs],
                                   g_s[pl.ds(r, SUB), hs], s_ref[hd], SUB, eye)
            o_s[pl.ds(r, SUB), hs] = o
            s_ref[hd] = s_new
        return carry

    lax.fori_loop(0, CHUNK // SUB, sub, 0)
    bg = bg_ref[...]
    nw = nw_ref[...]
    for hd in range(B_HEADS):
        hs = slice(hd * LANE, (hd + 1) * LANE)
        o_ref[:, hs] = _b_post(o_s[:, hs], bg[:, hs], nw).astype(bf16)

    @pl.when(c == NCHUNK - 1)
    def _():
        st_ref[0] = s_ref[...]


def _prompt_b(h, lb, nw):
    return pl.pallas_call(
        _pb_body,
        out_shape=(jax.ShapeDtypeStruct((N_TOK, BW), bf16),
                   jax.ShapeDtypeStruct((BATCH, B_HEADS, LANE, LANE), f32)),
        grid=(BATCH, NCHUNK),
        in_specs=[_h_spec(CB_BQ), _h_spec(CB_BF), _h_spec(CB_BI), _h_spec(CB_BG),
                  _full_spec((1, BW)), _full_spec((1, LANE))],
        out_specs=(_BR_SPEC, pl.BlockSpec((1, B_HEADS, LANE, LANE), lambda b, c: (b, 0, 0, 0))),
        scratch_shapes=[pltpu.VMEM((B_HEADS, LANE, LANE), f32)] + [pltpu.VMEM((CHUNK, BW), f32)] * 5,
        compiler_params=_cparams(("parallel", "arbitrary")),
        name="prompt_hgrn",
    )(h, h, h, h, lb, nw)


def _c_conv_silu(win0, win1, win2, win3, cw_ref, cb_ref):
    y = cb_ref[...] + win0 * cw_ref[0:1, :] + win1 * cw_ref[1:2, :] + win2 * cw_ref[2:3, :] + win3 * cw_ref[3:4, :]
    return y * jax.nn.sigmoid(y)


def _c_post(yc, cz, nw):
    y = yc * (cz * jax.nn.sigmoid(cz))
    gw = BW // C_GROUPS
    parts = [_rms(y[:, g * gw:(g + 1) * gw], nw[:, g * gw:(g + 1) * gw]) for g in range(C_GROUPS)]
    return jnp.concatenate(parts, axis=-1)


def _pc_body(cz_ref, cx0_ref, cx1_ref, sm_ref, cw_ref, cb_ref, par_ref, nw_ref,
             o_ref, st_ref, cv_ref, xpad, sp_ref):
    c = pl.program_id(1)

    @pl.when(c == 0)
    def _():
        xpad[pl.ds(0, 8), :] = jnp.zeros((8, C_CONV_DIM), f32)
        sp_ref[...] = jnp.zeros_like(sp_ref)

    xpad[pl.ds(8, CHUNK), 0:BW] = cx0_ref[...]
    xpad[pl.ds(8, CHUNK), BW:2 * BW] = cx1_ref[...]
    xbc = _c_conv_silu(xpad[pl.ds(5, CHUNK), :], xpad[pl.ds(6, CHUNK), :], xpad[pl.ds(7, CHUNK), :],
                       xpad[pl.ds(8, CHUNK), :], cw_ref, cb_ref)

    @pl.when(c == NCHUNK - 1)
    def _():
        cv_ref[0] = xpad[pl.ds(CHUNK + 5, 3), :]

    xpad[pl.ds(0, 8), :] = xpad[pl.ds(CHUNK, 8), :]

    xs = xbc[:, 0:BW]
    bm = xbc[:, BW:BW + 2 * C_DSTATE]
    cm = xbc[:, BW + 2 * C_DSTATE:]
    par = par_ref[...]
    dt = jax.nn.softplus(sm_ref[...] + par[0:1, :])
    gl = dt * (-jnp.exp(par[1:2, :]))
    tri = _tri(CHUNK)
    g = jnp.dot(tri.astype(f32), gl, precision=HI, preferred_element_type=f32)
    gt = g.T
    dtt = dt.T
    lane = lax.broadcasted_iota(i32, (1, LANE), 1)
    lo = lane < C_HEADDIM
    cbs = []
    for grp in range(C_GROUPS):
        cg = cm[:, grp * C_DSTATE:(grp + 1) * C_DSTATE].astype(bf16)
        bg = bm[:, grp * C_DSTATE:(grp + 1) * C_DSTATE].astype(bf16)
        cbs.append(lax.dot_general(cg, bg, (((1,), (1,)), ((), ())), preferred_element_type=f32))
    ys = []
    for pr in range(C_HEADS // 2):
        xp = xs[:, pr * LANE:(pr + 1) * LANE]
        sp = sp_ref[pr]
        y = jnp.zeros((CHUNK, LANE), f32)
        upd = jnp.zeros((C_DSTATE, LANE), f32)
        dl = jnp.zeros((1, LANE), f32)
        cdl = jnp.zeros((1, LANE), f32)
        for sub in range(2):
            hd = 2 * pr + sub
            grp = hd // (C_HEADS // C_GROUPS)
            lm = lo if sub == 0 else jnp.logical_not(lo)
            col = g[:, hd:hd + 1]
            g_last = col[CHUNK - 1:CHUNK, :]
            dec = jnp.exp(jnp.where(tri, col - gt[hd:hd + 1, :], 0.0))
            sc = jnp.where(tri, cbs[grp] * dec * dtt[hd:hd + 1, :], 0.0)
            xm = jnp.where(lm, xp, 0.0).astype(bf16)
            cg = cm[:, grp * C_DSTATE:(grp + 1) * C_DSTATE]
            bg = bm[:, grp * C_DSTATE:(grp + 1) * C_DSTATE]
            y = y + jnp.dot(sc.astype(bf16), xm, preferred_element_type=f32)
            y = y + jnp.dot((cg * jnp.exp(col)).astype(bf16), jnp.where(lm, sp, 0.0).astype(bf16),
                            preferred_element_type=f32)
            kd = bg * (dt[:, hd:hd + 1] * jnp.exp(g_last - col))
            upd = upd + lax.dot_general(kd.astype(bf16), xm, (((0,), (0,)), ((), ())), preferred_element_type=f32)
            dl = jnp.where(lm, jnp.exp(g_last), dl)
            cdl = jnp.where(lm, par[2:3, hd:hd + 1], cdl)
        sp_ref[pr] = dl * sp + upd
        ys.append(y + cdl * xp)
    yc = jnp.concatenate(ys, axis=-1)
    o_ref[...] = _c_post(yc, cz_ref[...], nw_ref[...]).astype(bf16)

    @pl.when(c == NCHUNK - 1)
    def _():
        for pr in range(C_HEADS // 2):
            st_ref[0, 2 * pr] = sp_ref[pr][:, 0:C_HEADDIM]
            st_ref[0, 2 * pr + 1] = sp_ref[pr][:, C_HEADDIM:]


def _prompt_c(h, cw, cb, par, nw):
    return pl.pallas_call(
        _pc_body,
        out_shape=(jax.ShapeDtypeStruct((N_TOK, BW), bf16),
                   jax.ShapeDtypeStruct((BATCH, C_HEADS, C_DSTATE, C_HEADDIM), f32),
                   jax.ShapeDtypeStruct((BATCH, C_CONV - 1, C_CONV_DIM), f32)),
        grid=(BATCH, NCHUNK),
        in_specs=[_h_spec(CB_CZ), _h_spec(CB_CX0), _h_spec(CB_CX1), _SMALL_SPEC,
                  _full_spec((C_CONV, C_CONV_DIM)), _full_spec((1, C_CONV_DIM)), _full_spec((8, LANE)),
                  _full_spec((1, BW))],
        out_specs=(_BR_SPEC,
                   pl.BlockSpec((1, C_HEADS, C_DSTATE, C_HEADDIM), lambda b, c: (b, 0, 0, 0)),
                   pl.BlockSpec((1, C_CONV - 1, C_CONV_DIM), lambda b, c: (b, 0, 0))),
        scratch_shapes=[pltpu.VMEM((CHUNK + 8, C_CONV_DIM), f32), pltpu.VMEM((C_HEADS // 2, C_DSTATE, LANE), f32)],
        compiler_params=_cparams(("parallel", "arbitrary")),
        name="prompt_ssd",
    )(h, h, h, h, cw, cb, par, nw)


def _pd_body(dq_ref, dk_ref, dv_ref, do_ref, sm_ref, par_ref, nw_ref,
             o_ref, c_out, n_out, m_out, c_ref, n_ref, m_ref):
    c = pl.program_id(1)

    @pl.when(c == 0)
    def _():
        c_ref[...] = jnp.zeros_like(c_ref)
        n_ref[...] = jnp.zeros_like(n_ref)
        m_ref[...] = jnp.zeros_like(m_ref)

    sm = sm_ref[...] + par_ref[0:1, :]
    ls = jax.nn.log_sigmoid(sm)
    tri = _tri(CHUNK)
    bc = jnp.dot(tri.astype(f32), ls, precision=HI, preferred_element_type=f32)
    bct = bc.T
    smt = sm.T
    nw = nw_ref[...]
    for hd in range(D_HEADS):
        hs = slice(hd * LANE, (hd + 1) * LANE)
        li, lf = 8 + hd, 12 + hd
        bcol = bc[:, lf:lf + 1]
        brow = bct[lf:lf + 1, :]
        irow = smt[li:li + 1, :]
        icol = sm[:, li:li + 1]
        mprev = m_ref[hd:hd + 1, 0:1]
        dmat = jnp.where(tri, bcol - brow + irow, NEG_BIG)
        inter = bcol + mprev
        mt = jnp.maximum(inter, jnp.max(dmat, axis=-1, keepdims=True))
        w_intra = jnp.where(tri, jnp.exp(dmat - mt), 0.0)
        w_inter = jnp.exp(inter - mt)
        qh = dq_ref[:, hs] * (D_DK ** -0.5)
        kh = dk_ref[:, hs]
        vh = dv_ref[:, hs]
        qb = qh.astype(bf16)
        qk = lax.dot_general(qb, kh.astype(bf16), (((1,), (1,)), ((), ())), preferred_element_type=f32) * w_intra
        num = w_inter * jnp.dot(qb, c_ref[hd].astype(bf16), preferred_element_type=f32)
        num = num + jnp.dot(qk.astype(bf16), vh.astype(bf16), preferred_element_type=f32)
        den = w_inter * jnp.sum(qh * n_ref[hd:hd + 1, :], axis=-1, keepdims=True) + jnp.sum(qk, axis=-1, keepdims=True)
        hh = num / jnp.maximum(jnp.abs(den), jnp.exp(-mt))
        mnew = mt[CHUNK - 1:CHUNK, :]
        blast = bcol[CHUNK - 1:CHUNK, :]
        wk = jnp.exp(blast - bcol + icol - mnew)
        decay = jnp.exp(blast + mprev - mnew)
        wkk = wk * kh
        c_ref[hd] = decay * c_ref[hd] + lax.dot_general(wkk.astype(bf16), vh.astype(bf16), (((0,), (0,)), ((), ())),
                                                        preferred_element_type=f32)
        n_ref[hd:hd + 1, :] = decay * n_ref[hd:hd + 1, :] + jnp.sum(wkk, axis=0, keepdims=True)
        m_ref[hd:hd + 1, :] = jnp.broadcast_to(mnew, (1, LANE))
        o_ref[:, hs] = (jax.nn.sigmoid(do_ref[:, hs]) * _rms(hh, nw[:, hs])).astype(bf16)

    @pl.when(c == NCHUNK - 1)
    def _():
        c_out[0] = c_ref[...]
        n_out[0] = n_ref[...]
        m_out[0] = m_ref[...]


def _prompt_d(h, par, nw):
    return pl.pallas_call(
        _pd_body,
        out_shape=(jax.ShapeDtypeStruct((N_TOK, BW), bf16),
                   jax.ShapeDtypeStruct((BATCH, D_HEADS, LANE, LANE), f32),
                   jax.ShapeDtypeStruct((BATCH, 8, LANE), f32),
                   jax.ShapeDtypeStruct((BATCH, 8, LANE), f32)),
        grid=(BATCH, NCHUNK),
        in_specs=[_h_spec(CB_DQ), _h_spec(CB_DK), _h_spec(CB_DV), _h_spec(CB_DO), _SMALL_SPEC,
                  _full_spec((8, LANE)), _full_spec((1, BW))],
        out_specs=(_BR_SPEC,
                   pl.BlockSpec((1, D_HEADS, LANE, LANE), lambda b, c: (b, 0, 0, 0)),
                   pl.BlockSpec((1, 8, LANE), lambda b, c: (b, 0, 0)),
                   pl.BlockSpec((1, 8, LANE), lambda b, c: (b, 0, 0))),
        scratch_shapes=[pltpu.VMEM((D_HEADS, LANE, LANE), f32), pltpu.VMEM((8, LANE), f32), pltpu.VMEM((8, LANE), f32)],
        compiler_params=_cparams(("parallel", "arbitrary")),
        name="prompt_mlstm",
    )(h, h, h, h, h, par, nw)


def _sample_body(h_ref, hg_ref, ssm_ref, cv_ref, mc_ref, mn_ref, mm_ref,
                 lnw_ref, lnb_ref, ws8_ref, bs8_ref, lb_ref, bnw_ref,
                 cw_ref, cb_ref, cpar_ref, cnw_ref, dpar_ref, dnw_ref,
                 oa_ref, ob_ref, oc_ref, od_ref, chv_ref,
                 hg_out, ssm_out, cv_out, mc_out, mn_out, mm_out):
    row = lax.broadcasted_iota(i32, (8, 1), 0)
    first = row < DEC_SEQ
    tpos = row & (DEC_SEQ - 1)
    eye = lax.broadcasted_iota(i32, (LANE, LANE), 0) == lax.broadcasted_iota(i32, (LANE, LANE), 1)
    lane = lax.broadcasted_iota(i32, (1, LANE), 1)
    lo = lane < C_HEADDIM

    def tile(p, carry):
        r = pl.multiple_of(p * 8, 8)

        def col(blk, width=BW):
            return h_ref[pl.ds(r, 8), blk * BW:blk * BW + width]

        small = h_ref[pl.ds(r, 8), SMALL_COL:SMALL_COL + LANE]

        u, v = _a_uv(col(CB_AU), col(CB_AV), lnw_ref[...], lnb_ref[...])
        chv_ref[pl.ds(r, 8), :] = v
        sp = bs8_ref[...]
        for s in range(8):
            sp = sp + ws8_ref[s] * v[s:s + 1, :]
        oa_ref[pl.ds(r, 8), :] = (u * sp).astype(bf16)

        q, kb, logf = _b_pre(col(CB_BQ), col(CB_BF), lb_ref[...])
        bi = col(CB_BI)
        bg = col(CB_BG)
        for hd in range(B_HEADS):
            hs = slice(hd * LANE, (hd + 1) * LANE)
            outs = []
            for which in range(2):
                sq = 2 * p + which
                valid = first if which == 0 else jnp.logical_not(first)
                o, s_new = _hgrn_chunk(q[:, hs], kb[:, hs], bi[:, hs], logf[:, hs], hg_ref[sq, hd], 8, eye, valid)
                hg_out[sq, hd] = s_new
                outs.append(o)
            o = jnp.where(first, outs[0], outs[1])
            ob_ref[pl.ds(r, 8), hs] = _b_post(o, bg[:, hs], bnw_ref[...]).astype(bf16)

        x = h_ref[pl.ds(r, 8), CB_CX0 * BW:CB_CX0 * BW + C_CONV_DIM]
        bufs = [jnp.where(first, cv_ref[2 * p, k:k + 1, :], cv_ref[2 * p + 1, k:k + 1, :]) for k in range(3)]
        r1 = pltpu.roll(x, 1, 0)
        r2 = pltpu.roll(x, 2, 0)
        r3 = pltpu.roll(x, 3, 0)
        sh1 = jnp.where(tpos >= 1, r1, bufs[2])
        sh2 = jnp.where(tpos >= 2, r2, jnp.where(tpos == 0, bufs[1], bufs[2]))
        sh3 = jnp.where(tpos >= 3, r3, jnp.where(tpos == 0, bufs[0], jnp.where(tpos == 1, bufs[1], bufs[2])))
        xbc = _c_conv_silu(sh3, sh2, sh1, x, cw_ref, cb_ref)
        cv_out[2 * p] = pltpu.roll(x, 7, 0)[0:3, :]
        cv_out[2 * p + 1] = r3[0:3, :]
        xs = xbc[:, 0:BW]
        bm = xbc[:, BW:BW + 2 * C_DSTATE]
        cm = xbc[:, BW + 2 * C_DSTATE:]
        cpar = cpar_ref[...]
        dt = jax.nn.softplus(small + cpar[0:1, :])
        gl_all = dt * (-jnp.exp(cpar[1:2, :]))
        ys = [None] * (C_HEADS // 2)
        for which in range(2):
            sq = 2 * p + which
            valid = first if which == 0 else jnp.logical_not(first)
            g = _cumsum_rows_small(jnp.where(valid, gl_all, 0.0), 8)
            dtv = jnp.where(valid, dt, 0.0)
            dots = []
            for grp in range(C_GROUPS):
                cg = cm[:, grp * C_DSTATE:(grp + 1) * C_DSTATE]
                bgp = bm[:, grp * C_DSTATE:(grp + 1) * C_DSTATE]
                dots.append([jnp.sum(cg * bgp[s:s + 1, :], axis=-1, keepdims=True) for s in range(8)])
            for pr in range(C_HEADS // 2):
                xp = xs[:, pr * LANE:(pr + 1) * LANE]
                sp_lo = ssm_ref[sq, 2 * pr]
                sp_hi = ssm_ref[sq, 2 * pr + 1]
                spair = jnp.concatenate([sp_lo, sp_hi], axis=-1)
                y = jnp.zeros((8, LANE), f32)
                upd = jnp.zeros((C_DSTATE, LANE), f32)
                dl = jnp.zeros((1, LANE), f32)
                cdl = jnp.zeros((1, LANE), f32)
                for sub in range(2):
                    hd = 2 * pr + sub
                    grp = hd // (C_HEADS // C_GROUPS)
                    lm = lo if sub == 0 else jnp.logical_not(lo)
                    gcol = g[:, hd:hd + 1]
                    g_last = gcol[7:8, :]
                    xm = jnp.where(lm, xp, 0.0)
                    for s in range(8):
                        coef = dots[grp][s] * jnp.where(row >= s, jnp.exp(jnp.where(row >= s, gcol - gcol[s:s + 1, :], 0.0))
                                                        * dtv[s:s + 1, hd:hd + 1], 0.0)
                        y = y + coef * xm[s:s + 1, :]
                    cg = cm[:, grp * C_DSTATE:(grp + 1) * C_DSTATE]
                    bgp = bm[:, grp * C_DSTATE:(grp + 1) * C_DSTATE]
                    y = y + jnp.dot((cg * jnp.exp(gcol)).astype(bf16), jnp.where(lm, spair, 0.0).astype(bf16),
                                    preferred_element_type=f32)
                    kd = bgp * (dtv[:, hd:hd + 1] * jnp.exp(g_last - gcol))
                    upd = upd + lax.dot_general(kd.astype(bf16), xm.astype(bf16), (((0,), (0,)), ((), ())),
                                                preferred_element_type=f32)
                    dl = jnp.where(lm, jnp.exp(g_last), dl)
                    cdl = jnp.where(lm, cpar[2:3, hd:hd + 1], cdl)
                snew = dl * spair + upd
                ssm_out[sq, 2 * pr] = snew[:, 0:C_HEADDIM]
                ssm_out[sq, 2 * pr + 1] = snew[:, C_HEADDIM:]
                y = y + cdl * xp
                ys[pr] = y if which == 0 else jnp.where(first, ys[pr], y)
        yc = jnp.concatenate(ys, axis=-1)
        oc_ref[pl.ds(r, 8), :] = _c_post(yc, col(CB_CZ), cnw_ref[...]).astype(bf16)

        smd = small + dpar_ref[0:1, :]
        lsd = jax.nn.log_sigmoid(smd)
        dq = col(CB_DQ)
        dk = col(CB_DK)
        dv = col(CB_DV)
        do = col(CB_DO)
        dnw = dnw_ref[...]
        hs_out = [None] * D_HEADS
        for which in range(2):
            sq = 2 * p + which
            valid = first if which == 0 else jnp.logical_not(first)
            bcs = _cumsum_rows_small(jnp.where(valid, lsd, 0.0), 8)
            igv = jnp.where(valid, smd, NEG_BIG)
            mrow = mm_ref[pl.ds(sq, 1), :]
            mnew_row = jnp.zeros((1, D_HEADS), f32)
            hlane = lax.broadcasted_iota(i32, (1, D_HEADS), 1)
            for hd in range(D_HEADS):
                hs = slice(hd * LANE, (hd + 1) * LANE)
                li, lf = 8 + hd, 12 + hd
                bcol = bcs[:, lf:lf + 1]
                icol = igv[:, li:li + 1]
                mprev = mrow[:, hd:hd + 1]
                dcols = [jnp.where(row >= s, bcol - bcol[s:s + 1, :] + icol[s:s + 1, :], NEG_BIG) for s in range(8)]
                mx = dcols[0]
                for s in range(1, 8):
                    mx = jnp.maximum(mx, dcols[s])
                inter = bcol + mprev
                mt = jnp.maximum(inter, mx)
                w_inter = jnp.exp(inter - mt)
                qh = dq[:, hs] * (D_DK ** -0.5)
                kh = dk[:, hs]
                vh = dv[:, hs]
                cmat = mc_ref[sq, hd]
                nrow = mn_ref[sq, pl.ds(hd, 1), :]
                num = w_inter * jnp.dot(qh.astype(bf16), cmat.astype(bf16), preferred_element_type=f32)
                den = w_inter * jnp.sum(qh * nrow, axis=-1, keepdims=True)
                for s in range(8):
                    w = jnp.where(row >= s, jnp.exp(dcols[s] - mt), 0.0)
                    qk = jnp.sum(qh * kh[s:s + 1, :], axis=-1, keepdims=True) * w
                    num = num + qk * vh[s:s + 1, :]
                    den = den + qk
                hh = num / jnp.maximum(jnp.abs(den), jnp.exp(-mt))
                mnew = mt[7:8, :]
                blast = bcol[7:8, :]
                wk = jnp.exp(blast - bcol + icol - mnew)
                decay = jnp.exp(blast + mprev - mnew)
                wkk = wk * kh
                mc_out[sq, hd] = decay * cmat + lax.dot_general(wkk.astype(bf16), vh.astype(bf16), (((0,), (0,)), ((), ())),
                                                                preferred_element_type=f32)
                mn_out[sq, pl.ds(hd, 1), :] = decay * nrow + jnp.sum(wkk, axis=0, keepdims=True)
                mnew_row = jnp.where(hlane == hd, mnew, mnew_row)
                o = jax.nn.sigmoid(do[:, hs]) * _rms(hh, dnw[:, hs])
                hs_out[hd] = o if which == 0 else jnp.where(first, hs_out[hd], o)
            mm_out[pl.ds(sq, 1), :] = mnew_row
        for hd in range(D_HEADS):
            od_ref[pl.ds(r, 8), hd * LANE:(hd + 1) * LANE] = hs_out[hd].astype(bf16)
        return carry

    lax.fori_loop(0, SB // 2, tile, 0)


_N_SAMPLE_IN = 19


def _sample_body_aliased(n_alias, *refs):
    _sample_body(*refs[:_N_SAMPLE_IN], *refs[_N_SAMPLE_IN + n_alias:])


def _sample_mixer(h, st_hgrn, st_ssm, st_conv, st_c, st_n, st_m, pa, pb, pc, pd, layer, brs, stacked):
    rows = SB * DEC_SEQ
    row0 = N_PROMPT // rows

    def blk(shape):
        nd = len(shape)
        return pl.BlockSpec((SB,) + shape, lambda i, nd=nd: (i,) + (0,) * nd)

    def blk_stacked(shape):
        nd = len(shape)
        return pl.BlockSpec((None, SB) + shape, lambda i, nd=nd: (layer, i) + (0,) * nd)

    def full(shape):
        nd = len(shape)
        return pl.BlockSpec(shape, lambda i, nd=nd: (0,) * nd)

    big = ((B_HEADS, LANE, LANE), (C_HEADS, C_DSTATE, C_HEADDIM), (D_HEADS, LANE, LANE))
    in_state_specs = [blk_stacked(big[0]), blk_stacked(big[1]), blk_stacked((C_CONV - 1, C_CONV_DIM)),
                      blk_stacked(big[2]), blk_stacked((D_HEADS, LANE)), blk_stacked((D_HEADS,))]
    out_state_specs = [blk_stacked(big[0]), blk_stacked(big[1]), blk((C_CONV - 1, C_CONV_DIM)), blk_stacked(big[2]),
                       blk((D_HEADS, LANE)), blk((D_HEADS,))]
    params = list(pa) + list(pb) + list(pc) + list(pd)
    br_spec = pl.BlockSpec((rows, BW), lambda i: (row0 + i, 0))
    br_shape = jax.ShapeDtypeStruct((N_TOK, BW), bf16)
    stacked_shape = lambda a: jax.ShapeDtypeStruct(a.shape, f32)
    per_layer = lambda a: jax.ShapeDtypeStruct(a.shape[1:], f32)
    inputs = [h, st_hgrn, st_ssm, st_conv, st_c, st_n, st_m] + params
    in_specs = [pl.BlockSpec((rows, MIX_W), lambda i: (row0 + i, 0))] + in_state_specs + [full(p.shape) for p in params]
    assert len(inputs) == _N_SAMPLE_IN
    alias_in = list(brs) + (list(stacked) if stacked is not None else [])
    alias_out = [0, 1, 2, 3] + ([5, 6, 8] if stacked is not None else [])
    aliases = {_N_SAMPLE_IN + k: o for k, o in enumerate(alias_out)}
    return pl.pallas_call(
        functools.partial(_sample_body_aliased, len(alias_in)),
        out_shape=(br_shape, br_shape, br_shape, br_shape, jax.ShapeDtypeStruct((N_SAMPLE, BW), f32),
                   stacked_shape(st_hgrn), stacked_shape(st_ssm),
                   per_layer(st_conv), stacked_shape(st_c), per_layer(st_n), per_layer(st_m)),
        grid=(DEC_BATCH // SB,),
        in_specs=in_specs + [pl.BlockSpec(memory_space=pl.ANY)] * len(alias_in),
        out_specs=(br_spec, br_spec, br_spec, br_spec, pl.BlockSpec((rows, BW), lambda i: (i, 0)))
        + tuple(out_state_specs),
        input_output_aliases=aliases,
        compiler_params=_cparams(("parallel",)),
        name="sample_mixer",
    )(*inputs, *alias_in)


def _merge_body(x_ref, nw_ref, ba_ref, bb_ref, bc_ref, bd_ref, wg_ref, wb_ref, o_ref, xn_ref, acc_ref):
    jc = pl.program_id(1)
    n = pl.program_id(2)

    @pl.when((jc == 0) & (n == 0))
    def _():
        xn_ref[...] = _rms(x_ref[...], nw_ref[...]).astype(bf16)

    gate = jax.nn.sigmoid(jnp.dot(xn_ref[...], wg_ref[...], preferred_element_type=f32))
    for k, br_ref in enumerate((ba_ref, bb_ref, bc_ref, bd_ref)):
        @pl.when(n == k)
        def _(br_ref=br_ref, k=k):
            contrib = gate * jnp.dot(br_ref[...], wb_ref[0], preferred_element_type=f32)
            if k == 0:
                acc_ref[...] = contrib
            else:
                acc_ref[...] = acc_ref[...] + contrib

    @pl.when(n == N_BRANCH - 1)
    def _():
        o_ref[...] = acc_ref[...].astype(bf16)


def _merge(x, nw, brs, w_gate, w_branch, tm, tn, layer):
    m = x.shape[0]
    ncol = D_MODEL // tn
    br_spec = pl.BlockSpec((tm, BW), lambda i, jc, n: (i, 0), pipeline_mode=pl.Buffered(1))
    return pl.pallas_call(
        _merge_body,
        out_shape=jax.ShapeDtypeStruct((m, D_MODEL), bf16),
        grid=(m // tm, ncol, N_BRANCH),
        in_specs=[pl.BlockSpec((tm, D_MODEL), lambda i, jc, n: (i, 0), pipeline_mode=pl.Buffered(1)),
                  pl.BlockSpec((1, D_MODEL), lambda i, jc, n: (0, 0)),
                  br_spec, br_spec, br_spec, br_spec,
                  pl.BlockSpec((D_MODEL, tn), lambda i, jc, n: (0, n * ncol + jc)),
                  pl.BlockSpec((None, 1, BW, tn), lambda i, jc, n: (layer, n, 0, jc))],
        out_specs=pl.BlockSpec((tm, tn), lambda i, jc, n: (i, jc)),
        scratch_shapes=[pltpu.VMEM((tm, D_MODEL), bf16), pltpu.VMEM((tm, tn), f32)],
        compiler_params=_cparams(("parallel", "arbitrary", "arbitrary")),
        name="merge",
    )(x, nw, *brs, w_gate, w_branch)


def _top16_rows(s, rid=None):
    if rid is None:
        rid = lax.broadcasted_iota(i32, s.shape, 0).astype(f32)
    out = []
    for _ in range(PEER_TOPK):
        m = jnp.max(s, axis=0, keepdims=True)
        am = jnp.min(jnp.where(s == m, rid, float(PEER_N)), axis=0, keepdims=True)
        out.append((m, am))
        s = jnp.where(rid == am, -jnp.inf, s)
    return out


def _collect16(pairs, tb):
    r16 = lax.broadcasted_iota(i32, (PEER_TOPK, tb), 0)
    v = jnp.zeros((PEER_TOPK, tb), f32)
    ix = jnp.zeros((PEER_TOPK, tb), f32)
    for k, (m, am) in enumerate(pairs):
        v = jnp.where(r16 == k, m, v)
        ix = jnp.where(r16 == k, am, ix)
    return v, ix.astype(i32)


def _peer_route_body(q_ref, keys_ref, ia_ref, ib_ref, gt_ref):
    tb = q_ref.shape[0]
    vals = []
    idxs = []
    for p in range(2):
        st = lax.dot_general(keys_ref[0, p], q_ref[:, p * LANE:(p + 1) * LANE], (((1,), (1,)), ((), ())),
                             preferred_element_type=f32)
        v, ix = _collect16(_top16_rows(st), tb)
        vals.append(v)
        idxs.append(ix)
    va, vb = vals
    half = PEER_TOPK // 2
    cand = jnp.concatenate([va[0:1, :] + vb] + [va[k:k + 1, :] + vb[0:half, :] for k in range(1, half)]
                           + [va[half:, :] + vb[0:1, :]], axis=0)
    r = lax.broadcasted_iota(i32, cand.shape, 0)
    r2 = r - PEER_TOPK
    mid = PEER_TOPK * (1 + (r2 >> 3)) + (r2 & (half - 1))
    pid = jnp.where(r < PEER_TOPK, r, jnp.where(r < PEER_TOPK + half * (half - 1), mid, (r - 8 * half) * PEER_TOPK))
    fs, pos = _collect16(_top16_rows(cand, pid.astype(f32)), tb)
    ka = pos >> 4
    kb = pos & (PEER_TOPK - 1)
    i1 = jnp.zeros((PEER_TOPK, tb), i32)
    i2 = jnp.zeros((PEER_TOPK, tb), i32)
    for j in range(PEER_TOPK):
        i1 = jnp.where(ka == j, idxs[0][j:j + 1, :], i1)
        i2 = jnp.where(kb == j, idxs[1][j:j + 1, :], i2)
    pe = jnp.exp(fs - fs[0:1, :])
    ia_ref[...] = i1
    ib_ref[...] = i2
    gt_ref[...] = pe / jnp.sum(pe, axis=0, keepdims=True)


def _peer_route(q, keys, tb, layer):
    m = q.shape[0]
    spec = pl.BlockSpec((PEER_TOPK, tb), lambda i, hd: (hd, i))
    nslot = PEER_HEADS * PEER_TOPK
    return pl.pallas_call(
        _peer_route_body,
        out_shape=(jax.ShapeDtypeStruct((nslot, m), i32), jax.ShapeDtypeStruct((nslot, m), i32),
                   jax.ShapeDtypeStruct((nslot, m), f32)),
        grid=(m // tb, PEER_HEADS),
        in_specs=[pl.BlockSpec((tb, 2 * LANE), lambda i, hd: (i, hd)),
                  pl.BlockSpec((None, 1, 2, PEER_NKEYS, LANE), lambda i, hd: (layer, hd, 0, 0, 0))],
        out_specs=(spec, spec, spec),
        compiler_params=_cparams(("parallel", "parallel")),
        name="peer_route",
    )(q, keys)


GATE_UNROLL = 8
RELAYOUT_TOKENS = 16


def _peer_gates_body(ia_ref, ib_ref, gt_ref, o_ref, a_s, b_s, g_s, gm_s):
    tg = ia_ref.shape[1]
    a_s[...] = ia_ref[...].astype(f32).T
    b_s[...] = ib_ref[...].astype(f32).T
    g_s[...] = gt_ref[...].T
    sub = lax.broadcasted_iota(i32, (LANE, LANE), 0).astype(f32)

    def step(r, carry):
        r8 = pl.multiple_of(r * GATE_UNROLL, GATE_UNROLL)
        a8 = a_s[pl.ds(r8, GATE_UNROLL), :]
        b8 = b_s[pl.ds(r8, GATE_UNROLL), :]
        g8 = g_s[pl.ds(r8, GATE_UNROLL), :]
        for u in range(GATE_UNROLL):
            at = jnp.where(sub == a8[u:u + 1, :], 1.0, 0.0).astype(bf16)
            bt = jnp.where(sub == b8[u:u + 1, :], g8[u:u + 1, :], 0.0).astype(bf16)
            gm = lax.dot_general(at, bt, (((1,), (1,)), ((), ())), preferred_element_type=f32)
            gm_s[r8 + u] = gm
        return carry

    lax.fori_loop(0, tg // GATE_UNROLL, step, 0)

    def relayout(c, carry):
        t0 = pl.multiple_of(c * RELAYOUT_TOKENS, RELAYOUT_TOKENS)
        for blk in range(PEER_NKEYS // 8):
            x = gm_s[pl.ds(t0, RELAYOUT_TOKENS), pl.ds(blk * 8, 8), :]
            y = jnp.swapaxes(x, 0, 1)
            for i in range(8):
                o_ref[pl.ds(t0, RELAYOUT_TOKENS), pl.ds((blk * 8 + i) * PEER_NKEYS, PEER_NKEYS)] = y[i].astype(bf16)
        return carry

    lax.fori_loop(0, tg // RELAYOUT_TOKENS, relayout, 0)


def _peer_gates(ia, ib, gt, tg):
    nslot, m = ia.shape
    spec = pl.BlockSpec((nslot, tg), lambda i: (0, i))
    return pl.pallas_call(
        _peer_gates_body,
        out_shape=jax.ShapeDtypeStruct((m, PEER_N), bf16),
        grid=(m // tg,),
        in_specs=[spec, spec, spec],
        out_specs=pl.BlockSpec((tg, PEER_N), lambda i: (i, 0)),
        scratch_shapes=[pltpu.VMEM((tg, nslot), f32)] * 3 + [pltpu.VMEM((tg, PEER_NKEYS, PEER_NKEYS), f32)],
        compiler_params=_cparams(("parallel",)),
        name="peer_gates",
    )(ia, ib, gt)


def _peer_experts_body(x_ref, nw_ref, u_ref, v_ref, g_ref, o_ref, xn_ref):
    @pl.when(pl.program_id(1) == 0)
    def _():
        x = x_ref[...]
        xn_ref[...] = _rms(x, nw_ref[...]).astype(bf16)
        o_ref[...] = x

    hmat = lax.dot_general(xn_ref[...], u_ref[...], (((1,), (1,)), ((), ())), preferred_element_type=f32)
    w = (_gelu(hmat) * g_ref[...].astype(f32)).astype(bf16)
    o_ref[...] += jnp.dot(w, v_ref[...], preferred_element_type=f32)


def _peer_experts(x, nw, u, v, g, tb, eb, layer):
    m = x.shape[0]
    return pl.pallas_call(
        _peer_experts_body,
        out_shape=jax.ShapeDtypeStruct((m, D_MODEL), f32),
        grid=(m // tb, PEER_N // eb),
        in_specs=[pl.BlockSpec((tb, D_MODEL), lambda i, j: (i, 0), pipeline_mode=pl.Buffered(1)),
                  pl.BlockSpec((1, D_MODEL), lambda i, j: (0, 0)),
                  pl.BlockSpec((None, eb, D_MODEL), lambda i, j: (layer, j, 0)),
                  pl.BlockSpec((None, eb, D_MODEL), lambda i, j: (layer, j, 0)),
                  pl.BlockSpec((tb, eb), lambda i, j: (i, j))],
        out_specs=pl.BlockSpec((tb, D_MODEL), lambda i, j: (i, 0), pipeline_mode=pl.Buffered(1)),
        scratch_shapes=[pltpu.VMEM((tb, D_MODEL), bf16)],
        compiler_params=_cparams(("parallel", "arbitrary")),
        name="peer_experts",
    )(x, nw, u, v, g)


_IN_OFFS = tuple(int(v) for v in np.cumsum((0,) + IN_SPLITS))
_O_CDT, _O_DQ, _O_DIG, _O_GATES, _O_END = _IN_OFFS[8], _IN_OFFS[9], _IN_OFFS[13], _IN_OFFS[15], _IN_OFFS[16]
PREP_ROWS = 128


def _prep_w_in_body(w_ref, om_ref, og_ref):
    rows = w_ref.shape[0]
    om_ref[:, 0:_O_CDT] = w_ref[:, 0:_O_CDT].astype(bf16)
    om_ref[:, _O_CDT:SMALL_COL] = w_ref[:, _O_DQ:_O_DIG].astype(bf16)
    small = jnp.concatenate([w_ref[:, _O_CDT:_O_DQ], w_ref[:, _O_DIG:_O_GATES],
                             jnp.zeros((rows, LANE - 16), w_ref.dtype)], axis=-1)
    om_ref[:, SMALL_COL:SMALL_COL + LANE] = small.astype(bf16)
    om_ref[:, SMALL_COL + LANE:MIX_W] = jnp.zeros((rows, MIX_W - SMALL_COL - LANE), bf16)
    og_ref[...] = w_ref[:, _O_GATES:_O_END].astype(bf16)


def _prep_w_in(w, layer):
    _, k, n = w.shape
    ng = _O_END - _O_GATES
    return pl.pallas_call(
        _prep_w_in_body,
        out_shape=(jax.ShapeDtypeStruct((k, MIX_W), bf16), jax.ShapeDtypeStruct((k, ng), bf16)),
        grid=(k // PREP_ROWS,),
        in_specs=[pl.BlockSpec((None, PREP_ROWS, n), lambda i: (layer, i, 0))],
        out_specs=(pl.BlockSpec((PREP_ROWS, MIX_W), lambda i: (i, 0)), pl.BlockSpec((PREP_ROWS, ng), lambda i: (i, 0))),
        compiler_params=_cparams(("parallel",)),
        name="prep_w_in",
    )(w)


def _lane_row(vals, start):
    row = jnp.zeros((LANE,), f32)
    return row.at[start:start + vals.shape[0]].set(vals)


def kernel(x_prompt, x_sample, state_hgrn, state_ssm, state_conv, state_mlstm_c, state_mlstm_n, state_mlstm_m, norm1_w, w_in, a_ln_w, a_ln_b, a_ws, a_bs, b_lb_logits, b_norm_w, c_conv_w, c_conv_b, c_dt_bias, c_a_log, c_d, c_norm_w, d_ig_b, d_fg_b, d_norm_w, w_branch, w_out, norm2_w, peer_wq, peer_keys, peer_u, peer_v, final_norm_w):
    x = jnp.concatenate([x_prompt.reshape(N_PROMPT, D_MODEL), x_sample.reshape(N_SAMPLE, D_MODEL)], axis=0)
    lbs = jax.nn.softmax(b_lb_logits.astype(f32), axis=0)
    lbs = jnp.cumsum(lbs, axis=0) - lbs[0]
    zeros8 = jnp.zeros((8, LANE), f32)
    news_p = []
    news_s = []
    stacked_s = None
    w_in_b, w_branch_b, w_out_b, wq_b = (w.astype(bf16) for w in (w_in, w_branch, w_out, peer_wq))
    keys_b, u_b, v_b = (w.astype(bf16) for w in (peer_keys, peer_u, peer_v))
    for l in range(DEPTH):
        w_mix, w_gate = _prep_w_in(w_in_b, l)
        h = _norm_mm(x, norm1_w[l][None, :], w_mix, f32, 1088, 1024)

        lnw = a_ln_w[l][None, :]
        lnb = a_ln_b[l][None, :]
        lb = lbs[l][None, :]
        bnw = b_norm_w[l][None, :]
        cw = c_conv_w[l]
        cb = c_conv_b[l][None, :]
        cpar = zeros8.at[0].set(_lane_row(c_dt_bias[l], 0)).at[1].set(_lane_row(c_a_log[l], 0)).at[2].set(_lane_row(c_d[l], 0))
        cnw = c_norm_w[l][None, :]
        dpar = zeros8.at[0].set(_lane_row(d_ig_b[l], 8) + _lane_row(d_fg_b[l], 12))
        dnw = d_norm_w[l][None, :]

        bs_full = jnp.repeat(a_bs[l].T, LANE, axis=1)
        br_a = _prompt_a(h, lnw, lnb, a_ws[l], bs_full)
        br_b, hg_p = _prompt_b(h, lb, bnw)
        br_c, ssm_p, cv_p = _prompt_c(h, cw, cb, cpar, cnw)
        br_d, mc_p, mn_p, mm_p = _prompt_d(h, dpar, dnw)
        news_p.append((hg_p, ssm_p, cv_p, mc_p, mn_p[:, :D_HEADS, :], mm_p[:, :D_HEADS, 0]))

        w4 = jnp.tril(a_ws[l][:, :DEC_SEQ, :DEC_SEQ])
        w8 = jnp.zeros((A_GROUPS, 8, 8), f32).at[:, :4, :4].set(w4).at[:, 4:, 4:].set(w4)
        ws8 = jnp.repeat(jnp.transpose(w8, (2, 1, 0)), LANE, axis=2)
        bs8 = jnp.repeat(jnp.tile(a_bs[l][:, :DEC_SEQ], (1, 2)).T, LANE, axis=1)
        outs = _sample_mixer(h, state_hgrn, state_ssm, state_conv, state_mlstm_c, state_mlstm_n, state_mlstm_m,
                             (lnw, lnb, ws8, bs8), (lb, bnw), (cw, cb, cpar, cnw), (dpar, dnw), l,
                             (br_a, br_b, br_c, br_d), stacked_s)
        brs = outs[:4]
        stacked_s = (outs[5], outs[6], outs[8])
        news_s.append((outs[7], outs[9], outs[10], outs[4].reshape(DEC_BATCH, DEC_SEQ, BW)))

        mixin = _merge(x, norm1_w[l][None, :], brs, w_gate, w_branch_b, 1088, 1024, l)
        x = _mm_res(mixin, w_out_b, x, 1088, 512, l)

        q = _norm_mm(x, norm2_w[l][None, :], wq_b, bf16, 1088, 1024, l)
        ia, ib, gt = _peer_route(q, keys_b, 512, l)
        g = _peer_gates(ia, ib, gt, 128)
        x = _peer_experts(x, norm2_w[l][None, :], u_b, v_b, g, 1088, 1024, l)

    y_p, y_s = _final_norm(x, final_norm_w[None, :])
    y_prompt = y_p.reshape(BATCH, SEQ, D_MODEL)
    y_sample = y_s.reshape(DEC_BATCH, DEC_SEQ, D_MODEL)
    stack = lambda news, k: jnp.stack([n[k] for n in news], axis=0)
    hgrn_s, ssm_s, mc_s = stacked_s
    conv_s, mn_s, mm_s, chunk_v_s = (stack(news_s, k) for k in range(4))
    return ((y_prompt, y_sample) + tuple(stack(news_p, k) for k in range(6))
            + (hgrn_s, ssm_s, conv_s, mc_s, mn_s, mm_s, chunk_v_s))
```

```python
import functools

import jax
import jax.numpy as jnp
import numpy as np
from jax import lax
from jax.experimental import pallas as pl
from jax.experimental.pallas import tpu as pltpu

f32 = jnp.float32
bf16 = jnp.bfloat16
i32 = jnp.int32
HI = lax.Precision.HIGHEST

D_MODEL = 2048
BATCH = 4
SEQ = 2048
DEPTH = 2
DEC_BATCH = 128
DEC_SEQ = 4
N_BRANCH = 4
BW = 512
A_GROUPS = 4
A_CHUNK = 128
B_HEADS = 4
LB_FLOOR = 1e-30
C_HEADDIM = 64
C_HEADS = 8
C_GROUPS = 2
C_DSTATE = 128
C_CONV = 4
C_CONV_DIM = 1024
D_HEADS = 4
D_DK = 128
NEG_BIG = -1e30
IN_SPLITS = (512, 512, 512, 512, 512, 512, 512, 1024, 8, 512, 512, 512, 512, 4, 4, 8192)
PEER_NKEYS = 128
PEER_N = PEER_NKEYS * PEER_NKEYS
PEER_HEADS = 8
PEER_TOPK = 16
EPS = 1e-6
INV_SQRT2 = 0.7071067811865476

N_PROMPT = BATCH * SEQ
N_SAMPLE = DEC_BATCH * DEC_SEQ
N_TOK = N_PROMPT + N_SAMPLE

CB_AU, CB_AV, CB_BQ, CB_BF, CB_BI, CB_BG, CB_CZ, CB_CX0, CB_CX1, CB_DQ, CB_DK, CB_DV, CB_DO = range(13)
SMALL_COL = 13 * 512
MIX_W = 14 * 512
LANE = 128
CHUNK = 128
NCHUNK = SEQ // CHUNK
SUB = 16
SB = 8
VMEM_LIMIT = 56 * 1024 * 1024


def _gelu(x):
    return 0.5 * x * (1.0 + lax.erf(x * INV_SQRT2))


def _rms(x, w):
    ms = jnp.mean(x * x, axis=-1, keepdims=True)
    return x * lax.rsqrt(ms + EPS) * w


def _tri(n):
    r = lax.broadcasted_iota(i32, (n, n), 0)
    c = lax.broadcasted_iota(i32, (n, n), 1)
    return r >= c


def _cumsum_rows_small(x, n):
    row = lax.broadcasted_iota(i32, (n, 1), 0)
    acc = jnp.zeros_like(x)
    for s in range(n):
        acc = acc + jnp.where(row >= s, x[s:s + 1, :], 0.0)
    return acc


def _row_to_col(row, eye):
    return jnp.sum(jnp.where(eye, row, 0.0), axis=-1, keepdims=True)


def _cparams(sem, vmem=VMEM_LIMIT):
    return pltpu.CompilerParams(dimension_semantics=sem, vmem_limit_bytes=vmem)


def _norm_mm_body(x_ref, nw_ref, w_ref, o_ref, xn_ref):
    @pl.when(pl.program_id(1) == 0)
    def _():
        xn_ref[...] = _rms(x_ref[...], nw_ref[...]).astype(bf16)

    o_ref[...] = jnp.dot(xn_ref[...], w_ref[...], preferred_element_type=f32).astype(o_ref.dtype)


def _w_spec(w, layer, block, index_map):
    if layer is None:
        return pl.BlockSpec(block, index_map)
    return pl.BlockSpec((None,) + block, lambda *g: (layer,) + index_map(*g))


def _norm_mm(x, nw, w, out_dtype, tm, tn, layer=None):
    m, k = x.shape
    n = w.shape[-1]
    return pl.pallas_call(
        _norm_mm_body,
        out_shape=jax.ShapeDtypeStruct((m, n), out_dtype),
        grid=(m // tm, n // tn),
        in_specs=[pl.BlockSpec((tm, k), lambda i, j: (i, 0), pipeline_mode=pl.Buffered(1)),
                  pl.BlockSpec((1, k), lambda i, j: (0, 0)),
                  _w_spec(w, layer, (k, tn), lambda i, j: (0, j))],
        out_specs=pl.BlockSpec((tm, tn), lambda i, j: (i, j)),
        scratch_shapes=[pltpu.VMEM((tm, k), bf16)],
        compiler_params=_cparams(("parallel", "arbitrary")),
        name="norm_mm",
    )(x, nw, w)


def _mm_res_body(a_ref, w_ref, r_ref, o_ref):
    o_ref[...] = r_ref[...] + jnp.dot(a_ref[...], w_ref[...], preferred_element_type=f32)


def _mm_res(a, w, res, tm, tn, layer=None):
    m, k = a.shape
    n = w.shape[-1]
    return pl.pallas_call(
        _mm_res_body,
        out_shape=jax.ShapeDtypeStruct((m, n), f32),
        grid=(m // tm, n // tn),
        in_specs=[pl.BlockSpec((tm, k), lambda i, j: (i, 0)),
                  _w_spec(w, layer, (k, tn), lambda i, j: (0, j)),
                  pl.BlockSpec((tm, tn), lambda i, j: (i, j))],
        out_specs=pl.BlockSpec((tm, tn), lambda i, j: (i, j)),
        compiler_params=_cparams(("parallel", "arbitrary")),
        name="mm_res",
    )(a, w, res)


def _final_norm_body(x_ref, w_ref, op_ref, os_ref):
    y = _rms(x_ref[...], w_ref[...])
    i = pl.program_id(0)

    @pl.when(i < N_PROMPT // N_SAMPLE)
    def _():
        op_ref[...] = y

    @pl.when(i == N_PROMPT // N_SAMPLE)
    def _():
        os_ref[...] = y


def _final_norm(x, w):
    m, k = x.shape
    tm = N_SAMPLE
    last_p = N_PROMPT // tm - 1
    return pl.pallas_call(
        _final_norm_body,
        out_shape=(jax.ShapeDtypeStruct((N_PROMPT, k), f32), jax.ShapeDtypeStruct((N_SAMPLE, k), f32)),
        grid=(m // tm,),
        in_specs=[pl.BlockSpec((tm, k), lambda i: (i, 0)), pl.BlockSpec((1, k), lambda i: (0, 0))],
        out_specs=(pl.BlockSpec((tm, k), lambda i: (jnp.minimum(i, last_p), 0)),
                   pl.BlockSpec((tm, k), lambda i: (0, 0))),
        compiler_params=_cparams(("arbitrary",)),
        name="final_norm",
    )(x, w)


def _a_uv(au, av, lnw, lnb):
    u = _gelu(au)
    g = _gelu(av)
    xc = g - jnp.mean(g, axis=-1, keepdims=True)
    var = jnp.mean(xc * xc, axis=-1, keepdims=True)
    v = xc * lax.rsqrt(var + EPS) * lnw + lnb
    return u, v


def _b_pre(bq, bf_, lb):
    q = bq * jax.nn.sigmoid(bq)
    logf = jnp.logaddexp(jnp.log(jnp.maximum(lb, LB_FLOOR)), jnp.log1p(-lb) + jax.nn.log_sigmoid(bf_))
    kb = (1.0 - lb) * jax.nn.sigmoid(-bf_)
    return q, kb, logf


def _hgrn_chunks(chains, n, eye):
    row = lax.broadcasted_iota(i32, (n, 1), 0)
    qs, ks, vs, gs, ss = [], [], [], [], []
    for q, k, v, gl, s_mat, valid in chains:
        if valid is not None:
            gl = jnp.where(valid, gl, 0.0)
            k = jnp.where(valid, k, 0.0)
        qs.append(q)
        ks.append(k)
        vs.append(v)
        gs.append(_cumsum_rows_small(gl, n))
        ss.append(s_mat)
    nc = len(chains)
    os_ = [jnp.dot((qs[c] * jnp.exp(gs[c])).astype(bf16), ss[c].astype(bf16), preferred_element_type=f32)
           for c in range(nc)]
    upds = []
    for c in range(nc):
        k_dec = ks[c] * jnp.exp(gs[c][n - 1:n, :] - gs[c])
        upds.append(lax.dot_general(k_dec.astype(bf16), vs[c].astype(bf16), (((0,), (0,)), ((), ())),
                                    preferred_element_type=f32))
    for s in range(n):
        m = row >= s
        for c in range(nc):
            g = gs[c]
            d = jnp.where(m, g - g[s:s + 1, :], 0.0)
            p = jnp.where(m, qs[c] * ks[c][s:s + 1, :] * jnp.exp(d), 0.0)
            os_[c] = os_[c] + jnp.sum(p, axis=-1, keepdims=True) * vs[c][s:s + 1, :]
    out = []
    for c in range(nc):
        s_new = jnp.exp(_row_to_col(gs[c][n - 1:n, :], eye)) * ss[c] + upds[c]
        out.append((os_[c], s_new))
    return out


def _b_post(o, bg, nw):
    return _rms(o, nw) * (bg * jax.nn.sigmoid(bg))


def _hrow(b, c):
    return b * NCHUNK + c


def _h_spec(colblk):
    return pl.BlockSpec((CHUNK, BW), lambda b, c, cb=colblk: (_hrow(b, c), cb))


_SMALL_SPEC = pl.BlockSpec((CHUNK, LANE), lambda b, c: (_hrow(b, c), SMALL_COL // LANE))
_BR_SPEC = pl.BlockSpec((CHUNK, BW), lambda b, c: (_hrow(b, c), 0))


def _full_spec(shape):
    nd = len(shape)
    return pl.BlockSpec(shape, lambda b, c, nd=nd: (0,) * nd)


def _pa_body(au_ref, av_ref, lnw_ref, lnb_ref, ws_ref, bs_ref, o_ref):
    u, v = _a_uv(au_ref[...], av_ref[...], lnw_ref[...], lnb_ref[...])
    tri = _tri(CHUNK)
    vb = v.astype(bf16)
    parts = []
    for g in range(A_GROUPS):
        w = jnp.where(tri, ws_ref[g], 0.0).astype(bf16)
        parts.append(jnp.dot(w, vb[:, g * LANE:(g + 1) * LANE], preferred_element_type=f32))
    sp = jnp.concatenate(parts, axis=-1) + bs_ref[...]
    o_ref[...] = (u * sp).astype(bf16)


def _prompt_a(h, lnw, lnb, ws, bs_full):
    return pl.pallas_call(
        _pa_body,
        out_shape=jax.ShapeDtypeStruct((N_TOK, BW), bf16),
        grid=(BATCH, NCHUNK),
        in_specs=[_h_spec(CB_AU), _h_spec(CB_AV), _full_spec((1, BW)), _full_spec((1, BW)),
                  _full_spec((A_GROUPS, A_CHUNK, A_CHUNK)), _full_spec((A_CHUNK, BW))],
        out_specs=_BR_SPEC,
        compiler_params=_cparams(("parallel", "parallel")),
        name="prompt_gmlp",
    )(h, h, lnw, lnb, ws, bs_full)


def _pb_body(bq_ref, bf_ref, bi_ref, bg_ref, lb_ref, nw_ref, o_ref, st_ref, s_ref, q_s, k_s, v_s, g_s, o_s):
    c = pl.program_id(1)

    @pl.when(c == 0)
    def _():
        s_ref[...] = jnp.zeros_like(s_ref)

    q, kb, logf = _b_pre(bq_ref[...], bf_ref[...], lb_ref[...])
    q_s[...] = q
    k_s[...] = kb
    v_s[...] = bi_ref[...]
    g_s[...] = logf
    eye = lax.broadcasted_iota(i32, (LANE, LANE), 0) == lax.broadcasted_iota(i32, (LANE, LANE), 1)
    def sub(j, carry):
        r = pl.multiple_of(j * SUB, SUB)
        heads = [slice(hd * LANE, (hd + 1) * LANE) for hd in range(B_HEADS)]
        chains = [(q_s[pl.ds(r, SUB), hs], k_s[pl.ds(r, SUB), hs], v_s[pl.ds(r, SUB), hs], g_s[pl.ds(r, SUB), hs],
                   s_ref[hd], None) for hd, hs in enumerate(heads)]
        for hd, (o, s_new) in enumerate(_hgrn_chunks(chains, SUB, eye)):
            o_s[pl.ds(r, SUB), heads[hd]] = o
            s_ref[hd] = s_new
        return carry

    lax.fori_loop(0, CHUNK // SUB, sub, 0, unroll=2)
    bg = bg_ref[...]
    nw = nw_ref[...]
    for hd in range(B_HEADS):
        hs = slice(hd * LANE, (hd + 1) * LANE)
        o_ref[:, hs] = _b_post(o_s[:, hs], bg[:, hs], nw).astype(bf16)

    @pl.when(c == NCHUNK - 1)
    def _():
        st_ref[0] = s_ref[...]


def _prompt_b(h, lb, nw):
    return pl.pallas_call(
        _pb_body,
        out_shape=(jax.ShapeDtypeStruct((N_TOK, BW), bf16),
                   jax.ShapeDtypeStruct((BATCH, B_HEADS, LANE, LANE), f32)),
        grid=(BATCH, NCHUNK),
        in_specs=[_h_spec(CB_BQ), _h_spec(CB_BF), _h_spec(CB_BI), _h_spec(CB_BG),
                  _full_spec((1, BW)), _full_spec((1, LANE))],
        out_specs=(_BR_SPEC, pl.BlockSpec((1, B_HEADS, LANE, LANE), lambda b, c: (b, 0, 0, 0))),
        scratch_shapes=[pltpu.VMEM((B_HEADS, LANE, LANE), f32)] + [pltpu.VMEM((CHUNK, BW), f32)] * 5,
        compiler_params=_cparams(("parallel", "arbitrary")),
        name="prompt_hgrn",
    )(h, h, h, h, lb, nw)


def _c_conv_silu(win0, win1, win2, win3, cw_ref, cb_ref):
    y = cb_ref[...] + win0 * cw_ref[0:1, :] + win1 * cw_ref[1:2, :] + win2 * cw_ref[2:3, :] + win3 * cw_ref[3:4, :]
    return y * jax.nn.sigmoid(y)


def _c_post(yc, cz, nw):
    y = yc * (cz * jax.nn.sigmoid(cz))
    gw = BW // C_GROUPS
    parts = [_rms(y[:, g * gw:(g + 1) * gw], nw[:, g * gw:(g + 1) * gw]) for g in range(C_GROUPS)]
    return jnp.concatenate(parts, axis=-1)


def _pc_body(cz_ref, cx0_ref, cx1_ref, sm_ref, cw_ref, cb_ref, par_ref, nw_ref,
             o_ref, st_ref, cv_ref, xpad, sp_ref):
    c = pl.program_id(1)

    @pl.when(c == 0)
    def _():
        xpad[pl.ds(0, 8), :] = jnp.zeros((8, C_CONV_DIM), f32)
        sp_ref[...] = jnp.zeros_like(sp_ref)

    xpad[pl.ds(8, CHUNK), 0:BW] = cx0_ref[...]
    xpad[pl.ds(8, CHUNK), BW:2 * BW] = cx1_ref[...]
    xbc = _c_conv_silu(xpad[pl.ds(5, CHUNK), :], xpad[pl.ds(6, CHUNK), :], xpad[pl.ds(7, CHUNK), :],
                       xpad[pl.ds(8, CHUNK), :], cw_ref, cb_ref)

    @pl.when(c == NCHUNK - 1)
    def _():
        cv_ref[0] = xpad[pl.ds(CHUNK + 5, 3), :]

    xpad[pl.ds(0, 8), :] = xpad[pl.ds(CHUNK, 8), :]

    xs = xbc[:, 0:BW]
    bm = xbc[:, BW:BW + 2 * C_DSTATE]
    cm = xbc[:, BW + 2 * C_DSTATE:]
    par = par_ref[...]
    dt = jax.nn.softplus(sm_ref[...] + par[0:1, :])
    gl = dt * (-jnp.exp(par[1:2, :]))
    tri = _tri(CHUNK)
    g = jnp.dot(tri.astype(f32), gl, precision=HI, preferred_element_type=f32)
    gt = g.T
    dtt = dt.T
    lane = lax.broadcasted_iota(i32, (1, LANE), 1)
    lo = lane < C_HEADDIM
    cbs = []
    for grp in range(C_GROUPS):
        cg = cm[:, grp * C_DSTATE:(grp + 1) * C_DSTATE].astype(bf16)
        bg = bm[:, grp * C_DSTATE:(grp + 1) * C_DSTATE].astype(bf16)
        cbs.append(lax.dot_general(cg, bg, (((1,), (1,)), ((), ())), preferred_element_type=f32))
    ys = []
    for pr in range(C_HEADS // 2):
        xp = xs[:, pr * LANE:(pr + 1) * LANE]
        sp = sp_ref[pr]
        y = jnp.zeros((CHUNK, LANE), f32)
        upd = jnp.zeros((C_DSTATE, LANE), f32)
        dl = jnp.zeros((1, LANE), f32)
        cdl = jnp.zeros((1, LANE), f32)
        for sub in range(2):
            hd = 2 * pr + sub
            grp = hd // (C_HEADS // C_GROUPS)
            lm = lo if sub == 0 else jnp.logical_not(lo)
            col = g[:, hd:hd + 1]
            g_last = col[CHUNK - 1:CHUNK, :]
            dec = jnp.exp(jnp.where(tri, col - gt[hd:hd + 1, :], 0.0))
            sc = jnp.where(tri, cbs[grp] * dec * dtt[hd:hd + 1, :], 0.0)
            xm = jnp.where(lm, xp, 0.0).astype(bf16)
            cg = cm[:, grp * C_DSTATE:(grp + 1) * C_DSTATE]
            bg = bm[:, grp * C_DSTATE:(grp + 1) * C_DSTATE]
            y = y + jnp.dot(sc.astype(bf16), xm, preferred_element_type=f32)
            y = y + jnp.dot((cg * jnp.exp(col)).astype(bf16), jnp.where(lm, sp, 0.0).astype(bf16),
                            preferred_element_type=f32)
            kd = bg * (dt[:, hd:hd + 1] * jnp.exp(g_last - col))
            upd = upd + lax.dot_general(kd.astype(bf16), xm, (((0,), (0,)), ((), ())), preferred_element_type=f32)
            dl = jnp.where(lm, jnp.exp(g_last), dl)
            cdl = jnp.where(lm, par[2:3, hd:hd + 1], cdl)
        sp_ref[pr] = dl * sp + upd
        ys.append(y + cdl * xp)
    yc = jnp.concatenate(ys, axis=-1)
    o_ref[...] = _c_post(yc, cz_ref[...], nw_ref[...]).astype(bf16)

    @pl.when(c == NCHUNK - 1)
    def _():
        for pr in range(C_HEADS // 2):
            st_ref[0, 2 * pr] = sp_ref[pr][:, 0:C_HEADDIM]
            st_ref[0, 2 * pr + 1] = sp_ref[pr][:, C_HEADDIM:]


def _prompt_c(h, cw, cb, par, nw):
    return pl.pallas_call(
        _pc_body,
        out_shape=(jax.ShapeDtypeStruct((N_TOK, BW), bf16),
                   jax.ShapeDtypeStruct((BATCH, C_HEADS, C_DSTATE, C_HEADDIM), f32),
                   jax.ShapeDtypeStruct((BATCH, C_CONV - 1, C_CONV_DIM), f32)),
        grid=(BATCH, NCHUNK),
        in_specs=[_h_spec(CB_CZ), _h_spec(CB_CX0), _h_spec(CB_CX1), _SMALL_SPEC,
                  _full_spec((C_CONV, C_CONV_DIM)), _full_spec((1, C_CONV_DIM)), _full_spec((8, LANE)),
                  _full_spec((1, BW))],
        out_specs=(_BR_SPEC,
                   pl.BlockSpec((1, C_HEADS, C_DSTATE, C_HEADDIM), lambda b, c: (b, 0, 0, 0)),
                   pl.BlockSpec((1, C_CONV - 1, C_CONV_DIM), lambda b, c: (b, 0, 0))),
        scratch_shapes=[pltpu.VMEM((CHUNK + 8, C_CONV_DIM), f32), pltpu.VMEM((C_HEADS // 2, C_DSTATE, LANE), f32)],
        compiler_params=_cparams(("parallel", "arbitrary")),
        name="prompt_ssd",
    )(h, h, h, h, cw, cb, par, nw)


def _pd_body(dq_ref, dk_ref, dv_ref, do_ref, sm_ref, par_ref, nw_ref,
             o_ref, c_out, n_out, m_out, c_ref, n_ref, m_ref):
    c = pl.program_id(1)

    @pl.when(c == 0)
    def _():
        c_ref[...] = jnp.zeros_like(c_ref)
        n_ref[...] = jnp.zeros_like(n_ref)
        m_ref[...] = jnp.zeros_like(m_ref)

    sm = sm_ref[...] + par_ref[0:1, :]
    ls = jax.nn.log_sigmoid(sm)
    tri = _tri(CHUNK)
    bc = jnp.dot(tri.astype(f32), ls, precision=HI, preferred_element_type=f32)
    bct = bc.T
    smt = sm.T
    nw = nw_ref[...]
    for hd in range(D_HEADS):
        hs = slice(hd * LANE, (hd + 1) * LANE)
        li, lf = 8 + hd, 12 + hd
        bcol = bc[:, lf:lf + 1]
        brow = bct[lf:lf + 1, :]
        irow = smt[li:li + 1, :]
        icol = sm[:, li:li + 1]
        mprev = m_ref[hd:hd + 1, 0:1]
        dmat = jnp.where(tri, bcol - brow + irow, NEG_BIG)
        inter = bcol + mprev
        mt = jnp.maximum(inter, jnp.max(dmat, axis=-1, keepdims=True))
        w_intra = jnp.where(tri, jnp.exp(dmat - mt), 0.0)
        w_inter = jnp.exp(inter - mt)
        qh = dq_ref[:, hs] * (D_DK ** -0.5)
        kh = dk_ref[:, hs]
        vh = dv_ref[:, hs]
        qb = qh.astype(bf16)
        qk = lax.dot_general(qb, kh.astype(bf16), (((1,), (1,)), ((), ())), preferred_element_type=f32) * w_intra
        num = w_inter * jnp.dot(qb, c_ref[hd].astype(bf16), preferred_element_type=f32)
        num = num + jnp.dot(qk.astype(bf16), vh.astype(bf16), preferred_element_type=f32)
        den = w_inter * jnp.sum(qh * n_ref[hd:hd + 1, :], axis=-1, keepdims=True) + jnp.sum(qk, axis=-1, keepdims=True)
        hh = num / jnp.maximum(jnp.abs(den), jnp.exp(-mt))
        mnew = mt[CHUNK - 1:CHUNK, :]
        blast = bcol[CHUNK - 1:CHUNK, :]
        wk = jnp.exp(blast - bcol + icol - mnew)
        decay = jnp.exp(blast + mprev - mnew)
        wkk = wk * kh
        c_ref[hd] = decay * c_ref[hd] + lax.dot_general(wkk.astype(bf16), vh.astype(bf16), (((0,), (0,)), ((), ())),
                                                        preferred_element_type=f32)
        n_ref[hd:hd + 1, :] = decay * n_ref[hd:hd + 1, :] + jnp.sum(wkk, axis=0, keepdims=True)
        m_ref[hd:hd + 1, :] = jnp.broadcast_to(mnew, (1, LANE))
        o_ref[:, hs] = (jax.nn.sigmoid(do_ref[:, hs]) * _rms(hh, nw[:, hs])).astype(bf16)

    @pl.when(c == NCHUNK - 1)
    def _():
        c_out[0] = c_ref[...]
        n_out[0] = n_ref[...]
        m_out[0] = m_ref[...]


def _prompt_d(h, par, nw):
    return pl.pallas_call(
        _pd_body,
        out_shape=(jax.ShapeDtypeStruct((N_TOK, BW), bf16),
                   jax.ShapeDtypeStruct((BATCH, D_HEADS, LANE, LANE), f32),
                   jax.ShapeDtypeStruct((BATCH, 8, LANE), f32),
                   jax.ShapeDtypeStruct((BATCH, 8, LANE), f32)),
        grid=(BATCH, NCHUNK),
        in_specs=[_h_spec(CB_DQ), _h_spec(CB_DK), _h_spec(CB_DV), _h_spec(CB_DO), _SMALL_SPEC,
                  _full_spec((8, LANE)), _full_spec((1, BW))],
        out_specs=(_BR_SPEC,
                   pl.BlockSpec((1, D_HEADS, LANE, LANE), lambda b, c: (b, 0, 0, 0)),
                   pl.BlockSpec((1, 8, LANE), lambda b, c: (b, 0, 0)),
                   pl.BlockSpec((1, 8, LANE), lambda b, c: (b, 0, 0))),
        scratch_shapes=[pltpu.VMEM((D_HEADS, LANE, LANE), f32), pltpu.VMEM((8, LANE), f32), pltpu.VMEM((8, LANE), f32)],
        compiler_params=_cparams(("parallel", "arbitrary")),
        name="prompt_mlstm",
    )(h, h, h, h, h, par, nw)


def _sample_body(h_ref, hg_ref, ssm_ref, cv_ref, mc_ref, mn_ref, mm_ref,
                 lnw_ref, lnb_ref, ws8_ref, bs8_ref, lb_ref, bnw_ref,
                 cw_ref, cb_ref, cpar_ref, cnw_ref, dpar_ref, dnw_ref,
                 oa_ref, ob_ref, oc_ref, od_ref, chv_ref,
                 hg_out, ssm_out, cv_out, mc_out, mn_out, mm_out):
    row = lax.broadcasted_iota(i32, (8, 1), 0)
    first = row < DEC_SEQ
    tpos = row & (DEC_SEQ - 1)
    eye = lax.broadcasted_iota(i32, (LANE, LANE), 0) == lax.broadcasted_iota(i32, (LANE, LANE), 1)
    lane = lax.broadcasted_iota(i32, (1, LANE), 1)
    lo = lane < C_HEADDIM

    def tile(p, carry):
        r = pl.multiple_of(p * 8, 8)

        def col(blk, width=BW):
            return h_ref[pl.ds(r, 8), blk * BW:blk * BW + width]

        small = h_ref[pl.ds(r, 8), SMALL_COL:SMALL_COL + LANE]

        u, v = _a_uv(col(CB_AU), col(CB_AV), lnw_ref[...], lnb_ref[...])
        chv_ref[pl.ds(r, 8), :] = v
        sp = bs8_ref[...]
        for s in range(8):
            sp = sp + ws8_ref[s] * v[s:s + 1, :]
        oa_ref[pl.ds(r, 8), :] = (u * sp).astype(bf16)

        q, kb, logf = _b_pre(col(CB_BQ), col(CB_BF), lb_ref[...])
        bi = col(CB_BI)
        bg = col(CB_BG)
        chains = []
        for hd in range(B_HEADS):
            hs = slice(hd * LANE, (hd + 1) * LANE)
            for which in range(2):
                valid = first if which == 0 else jnp.logical_not(first)
                chains.append((q[:, hs], kb[:, hs], bi[:, hs], logf[:, hs], hg_ref[2 * p + which, hd], valid))
        res = _hgrn_chunks(chains, 8, eye)
        for hd in range(B_HEADS):
            hs = slice(hd * LANE, (hd + 1) * LANE)
            for which in range(2):
                hg_out[2 * p + which, hd] = res[2 * hd + which][1]
            o = jnp.where(first, res[2 * hd][0], res[2 * hd + 1][0])
            ob_ref[pl.ds(r, 8), hs] = _b_post(o, bg[:, hs], bnw_ref[...]).astype(bf16)

        x = h_ref[pl.ds(r, 8), CB_CX0 * BW:CB_CX0 * BW + C_CONV_DIM]
        bufs = [jnp.where(first, cv_ref[2 * p, k:k + 1, :], cv_ref[2 * p + 1, k:k + 1, :]) for k in range(3)]
        r1 = pltpu.roll(x, 1, 0)
        r2 = pltpu.roll(x, 2, 0)
        r3 = pltpu.roll(x, 3, 0)
        sh1 = jnp.where(tpos >= 1, r1, bufs[2])
        sh2 = jnp.where(tpos >= 2, r2, jnp.where(tpos == 0, bufs[1], bufs[2]))
        sh3 = jnp.where(tpos >= 3, r3, jnp.where(tpos == 0, bufs[0], jnp.where(tpos == 1, bufs[1], bufs[2])))
        xbc = _c_conv_silu(sh3, sh2, sh1, x, cw_ref, cb_ref)
        cv_out[2 * p] = pltpu.roll(x, 7, 0)[0:3, :]
        cv_out[2 * p + 1] = r3[0:3, :]
        xs = xbc[:, 0:BW]
        bm = xbc[:, BW:BW + 2 * C_DSTATE]
        cm = xbc[:, BW + 2 * C_DSTATE:]
        cpar = cpar_ref[...]
        dt = jax.nn.softplus(small + cpar[0:1, :])
        gl_all = dt * (-jnp.exp(cpar[1:2, :]))
        ys = [None] * (C_HEADS // 2)
        for which in range(2):
            sq = 2 * p + which
            valid = first if which == 0 else jnp.logical_not(first)
            g = _cumsum_rows_small(jnp.where(valid, gl_all, 0.0), 8)
            dtv = jnp.where(valid, dt, 0.0)
            dots = []
            for grp in range(C_GROUPS):
                cg = cm[:, grp * C_DSTATE:(grp + 1) * C_DSTATE]
                bgp = bm[:, grp * C_DSTATE:(grp + 1) * C_DSTATE]
                dots.append([jnp.sum(cg * bgp[s:s + 1, :], axis=-1, keepdims=True) for s in range(8)])
            for pr in range(C_HEADS // 2):
                xp = xs[:, pr * LANE:(pr + 1) * LANE]
                sp_lo = ssm_ref[sq, 2 * pr]
                sp_hi = ssm_ref[sq, 2 * pr + 1]
                spair = jnp.concatenate([sp_lo, sp_hi], axis=-1)
                y = jnp.zeros((8, LANE), f32)
                upd = jnp.zeros((C_DSTATE, LANE), f32)
                dl = jnp.zeros((1, LANE), f32)
                cdl = jnp.zeros((1, LANE), f32)
                for sub in range(2):
                    hd = 2 * pr + sub
                    grp = hd // (C_HEADS // C_GROUPS)
                    lm = lo if sub == 0 else jnp.logical_not(lo)
                    gcol = g[:, hd:hd + 1]
                    g_last = gcol[7:8, :]
                    xm = jnp.where(lm, xp, 0.0)
                    for s in range(8):
                        coef = dots[grp][s] * jnp.where(row >= s, jnp.exp(jnp.where(row >= s, gcol - gcol[s:s + 1, :], 0.0))
                                                        * dtv[s:s + 1, hd:hd + 1], 0.0)
                        y = y + coef * xm[s:s + 1, :]
                    cg = cm[:, grp * C_DSTATE:(grp + 1) * C_DSTATE]
                    bgp = bm[:, grp * C_DSTATE:(grp + 1) * C_DSTATE]
                    y = y + jnp.dot((cg * jnp.exp(gcol)).astype(bf16), jnp.where(lm, spair, 0.0).astype(bf16),
                                    preferred_element_type=f32)
                    kd = bgp * (dtv[:, hd:hd + 1] * jnp.exp(g_last - gcol))
                    upd = upd + lax.dot_general(kd.astype(bf16), xm.astype(bf16), (((0,), (0,)), ((), ())),
                                                preferred_element_type=f32)
                    dl = jnp.where(lm, jnp.exp(g_last), dl)
                    cdl = jnp.where(lm, cpar[2:3, hd:hd + 1], cdl)
                snew = dl * spair + upd
                ssm_out[sq, 2 * pr] = snew[:, 0:C_HEADDIM]
                ssm_out[sq, 2 * pr + 1] = snew[:, C_HEADDIM:]
                y = y + cdl * xp
                ys[pr] = y if which == 0 else jnp.where(first, ys[pr], y)
        yc = jnp.concatenate(ys, axis=-1)
        oc_ref[pl.ds(r, 8), :] = _c_post(yc, col(CB_CZ), cnw_ref[...]).astype(bf16)

        smd = small + dpar_ref[0:1, :]
        lsd = jax.nn.log_sigmoid(smd)
        dq = col(CB_DQ)
        dk = col(CB_DK)
        dv = col(CB_DV)
        do = col(CB_DO)
        dnw = dnw_ref[...]
        hs_out = [None] * D_HEADS
        for which in range(2):
            sq = 2 * p + which
            valid = first if which == 0 else jnp.logical_not(first)
            bcs = _cumsum_rows_small(jnp.where(valid, lsd, 0.0), 8)
            igv = jnp.where(valid, smd, NEG_BIG)
            mrow = mm_ref[pl.ds(sq, 1), :]
            mnew_row = jnp.zeros((1, D_HEADS), f32)
            hlane = lax.broadcasted_iota(i32, (1, D_HEADS), 1)
            for hd in range(D_HEADS):
                hs = slice(hd * LANE, (hd + 1) * LANE)
                li, lf = 8 + hd, 12 + hd
                bcol = bcs[:, lf:lf + 1]
                icol = igv[:, li:li + 1]
                mprev = mrow[:, hd:hd + 1]
                dcols = [jnp.where(row >= s, bcol - bcol[s:s + 1, :] + icol[s:s + 1, :], NEG_BIG) for s in range(8)]
                mx = dcols[0]
                for s in range(1, 8):
                    mx = jnp.maximum(mx, dcols[s])
                inter = bcol + mprev
                mt = jnp.maximum(inter, mx)
                w_inter = jnp.exp(inter - mt)
                qh = dq[:, hs] * (D_DK ** -0.5)
                kh = dk[:, hs]
                vh = dv[:, hs]
                cmat = mc_ref[sq, hd]
                nrow = mn_ref[sq, pl.ds(hd, 1), :]
                num = w_inter * jnp.dot(qh.astype(bf16), cmat.astype(bf16), preferred_element_type=f32)
                den = w_inter * jnp.sum(qh * nrow, axis=-1, keepdims=True)
                for s in range(8):
                    w = jnp.where(row >= s, jnp.exp(dcols[s] - mt), 0.0)
                    qk = jnp.sum(qh * kh[s:s + 1, :], axis=-1, keepdims=True) * w
                    num = num + qk * vh[s:s + 1, :]
                    den = den + qk
                hh = num / jnp.maximum(jnp.abs(den), jnp.exp(-mt))
                mnew = mt[7:8, :]
                blast = bcol[7:8, :]
                wk = jnp.exp(blast - bcol + icol - mnew)
                decay = jnp.exp(blast + mprev - mnew)
                wkk = wk * kh
                mc_out[sq, hd] = decay * cmat + lax.dot_general(wkk.astype(bf16), vh.astype(bf16), (((0,), (0,)), ((), ())),
                                                                preferred_element_type=f32)
                mn_out[sq, pl.ds(hd, 1), :] = decay * nrow + jnp.sum(wkk, axis=0, keepdims=True)
                mnew_row = jnp.where(hlane == hd, mnew, mnew_row)
                o = jax.nn.sigmoid(do[:, hs]) * _rms(hh, dnw[:, hs])
                hs_out[hd] = o if which == 0 else jnp.where(first, hs_out[hd], o)
            mm_out[pl.ds(sq, 1), :] = mnew_row
        for hd in range(D_HEADS):
            od_ref[pl.ds(r, 8), hd * LANE:(hd + 1) * LANE] = hs_out[hd].astype(bf16)
        return carry

    lax.fori_loop(0, SB // 2, tile, 0)


_N_SAMPLE_IN = 19


def _sample_body_aliased(n_alias, *refs):
    _sample_body(*refs[:_N_SAMPLE_IN], *refs[_N_SAMPLE_IN + n_alias:])


def _sample_mixer(h, st_hgrn, st_ssm, st_conv, st_c, st_n, st_m, pa, pb, pc, pd, layer, brs, stacked):
    rows = SB * DEC_SEQ
    row0 = N_PROMPT // rows

    def blk(shape):
        nd = len(shape)
        return pl.BlockSpec((SB,) + shape, lambda i, nd=nd: (i,) + (0,) * nd)

    def blk_stacked(shape):
        nd = len(shape)
        return pl.BlockSpec((None, SB) + shape, lambda i, nd=nd: (layer, i) + (0,) * nd)

    def full(shape):
        nd = len(shape)
        return pl.BlockSpec(shape, lambda i, nd=nd: (0,) * nd)

    big = ((B_HEADS, LANE, LANE), (C_HEADS, C_DSTATE, C_HEADDIM), (D_HEADS, LANE, LANE))
    in_state_specs = [blk_stacked(big[0]), blk_stacked(big[1]), blk_stacked((C_CONV - 1, C_CONV_DIM)),
                      blk_stacked(big[2]), blk_stacked((D_HEADS, LANE)), blk_stacked((D_HEADS,))]
    out_state_specs = [blk_stacked(big[0]), blk_stacked(big[1]), blk((C_CONV - 1, C_CONV_DIM)), blk_stacked(big[2]),
                       blk((D_HEADS, LANE)), blk((D_HEADS,))]
    params = list(pa) + list(pb) + list(pc) + list(pd)
    br_spec = pl.BlockSpec((rows, BW), lambda i: (row0 + i, 0))
    br_shape = jax.ShapeDtypeStruct((N_TOK, BW), bf16)
    stacked_shape = lambda a: jax.ShapeDtypeStruct(a.shape, f32)
    per_layer = lambda a: jax.ShapeDtypeStruct(a.shape[1:], f32)
    inputs = [h, st_hgrn, st_ssm, st_conv, st_c, st_n, st_m] + params
    in_specs = [pl.BlockSpec((rows, MIX_W), lambda i: (row0 + i, 0))] + in_state_specs + [full(p.shape) for p in params]
    assert len(inputs) == _N_SAMPLE_IN
    alias_in = list(brs) + (list(stacked) if stacked is not None else [])
    alias_out = [0, 1, 2, 3] + ([5, 6, 8] if stacked is not None else [])
    aliases = {_N_SAMPLE_IN + k: o for k, o in enumerate(alias_out)}
    return pl.pallas_call(
        functools.partial(_sample_body_aliased, len(alias_in)),
        out_shape=(br_shape, br_shape, br_shape, br_shape, jax.ShapeDtypeStruct((N_SAMPLE, BW), f32),
                   stacked_shape(st_hgrn), stacked_shape(st_ssm),
                   per_layer(st_conv), stacked_shape(st_c), per_layer(st_n), per_layer(st_m)),
        grid=(DEC_BATCH // SB,),
        in_specs=in_specs + [pl.BlockSpec(memory_space=pl.ANY)] * len(alias_in),
        out_specs=(br_spec, br_spec, br_spec, br_spec, pl.BlockSpec((rows, BW), lambda i: (i, 0)))
        + tuple(out_state_specs),
        input_output_aliases=aliases,
        compiler_params=_cparams(("parallel",)),
        name="sample_mixer",
    )(*inputs, *alias_in)


def _merge_body(x_ref, nw_ref, ba_ref, bb_ref, bc_ref, bd_ref, wg_ref, wb_ref, o_ref, xn_ref, acc_ref):
    jc = pl.program_id(1)
    n = pl.program_id(2)

    @pl.when((jc == 0) & (n == 0))
    def _():
        xn_ref[...] = _rms(x_ref[...], nw_ref[...]).astype(bf16)

    gate = jax.nn.sigmoid(jnp.dot(xn_ref[...], wg_ref[...], preferred_element_type=f32))
    for k, br_ref in enumerate((ba_ref, bb_ref, bc_ref, bd_ref)):
        @pl.when(n == k)
        def _(br_ref=br_ref, k=k):
            contrib = gate * jnp.dot(br_ref[...], wb_ref[0], preferred_element_type=f32)
            if k == 0:
                acc_ref[...] = contrib
            else:
                acc_ref[...] = acc_ref[...] + contrib

    @pl.when(n == N_BRANCH - 1)
    def _():
        o_ref[...] = acc_ref[...].astype(bf16)


def _merge(x, nw, brs, w_gate, w_branch, tm, tn, layer):
    m = x.shape[0]
    ncol = D_MODEL // tn
    br_spec = pl.BlockSpec((tm, BW), lambda i, jc, n: (i, 0), pipeline_mode=pl.Buffered(1))
    return pl.pallas_call(
        _merge_body,
        out_shape=jax.ShapeDtypeStruct((m, D_MODEL), bf16),
        grid=(m // tm, ncol, N_BRANCH),
        in_specs=[pl.BlockSpec((tm, D_MODEL), lambda i, jc, n: (i, 0), pipeline_mode=pl.Buffered(1)),
                  pl.BlockSpec((1, D_MODEL), lambda i, jc, n: (0, 0)),
                  br_spec, br_spec, br_spec, br_spec,
                  pl.BlockSpec((D_MODEL, tn), lambda i, jc, n: (0, n * ncol + jc)),
                  pl.BlockSpec((None, 1, BW, tn), lambda i, jc, n: (layer, n, 0, jc))],
        out_specs=pl.BlockSpec((tm, tn), lambda i, jc, n: (i, jc)),
        scratch_shapes=[pltpu.VMEM((tm, D_MODEL), bf16), pltpu.VMEM((tm, tn), f32)],
        compiler_params=_cparams(("parallel", "arbitrary", "arbitrary")),
        name="merge",
    )(x, nw, *brs, w_gate, w_branch)


def _top16_rows(s, rid=None):
    if rid is None:
        rid = lax.broadcasted_iota(i32, s.shape, 0).astype(f32)
    out = []
    for _ in range(PEER_TOPK):
        m = jnp.max(s, axis=0, keepdims=True)
        am = jnp.min(jnp.where(s == m, rid, float(PEER_N)), axis=0, keepdims=True)
        out.append((m, am))
        s = jnp.where(rid == am, -jnp.inf, s)
    return out


def _collect16(pairs, tb):
    r16 = lax.broadcasted_iota(i32, (PEER_TOPK, tb), 0)
    v = jnp.zeros((PEER_TOPK, tb), f32)
    ix = jnp.zeros((PEER_TOPK, tb), f32)
    for k, (m, am) in enumerate(pairs):
        v = jnp.where(r16 == k, m, v)
        ix = jnp.where(r16 == k, am, ix)
    return v, ix.astype(i32)


def _peer_route_body(q_ref, keys_ref, ia_ref, ib_ref, gt_ref):
    tb = q_ref.shape[0]
    vals = []
    idxs = []
    for p in range(2):
        st = lax.dot_general(keys_ref[0, p], q_ref[:, p * LANE:(p + 1) * LANE], (((1,), (1,)), ((), ())),
                             preferred_element_type=f32)
        v, ix = _collect16(_top16_rows(st), tb)
        vals.append(v)
        idxs.append(ix)
    va, vb = vals
    half = PEER_TOPK // 2
    cand = jnp.concatenate([va[0:1, :] + vb] + [va[k:k + 1, :] + vb[0:half, :] for k in range(1, half)]
                           + [va[half:, :] + vb[0:1, :]], axis=0)
    r = lax.broadcasted_iota(i32, cand.shape, 0)
    r2 = r - PEER_TOPK
    mid = PEER_TOPK * (1 + (r2 >> 3)) + (r2 & (half - 1))
    pid = jnp.where(r < PEER_TOPK, r, jnp.where(r < PEER_TOPK + half * (half - 1), mid, (r - 8 * half) * PEER_TOPK))
    fs, pos = _collect16(_top16_rows(cand, pid.astype(f32)), tb)
    ka = pos >> 4
    kb = pos & (PEER_TOPK - 1)
    i1 = jnp.zeros((PEER_TOPK, tb), i32)
    i2 = jnp.zeros((PEER_TOPK, tb), i32)
    for j in range(PEER_TOPK):
        i1 = jnp.where(ka == j, idxs[0][j:j + 1, :], i1)
        i2 = jnp.where(kb == j, idxs[1][j:j + 1, :], i2)
    pe = jnp.exp(fs - fs[0:1, :])
    ia_ref[...] = i1
    ib_ref[...] = i2
    gt_ref[...] = pe / jnp.sum(pe, axis=0, keepdims=True)


def _peer_route(q, keys, tb, layer):
    m = q.shape[0]
    spec = pl.BlockSpec((PEER_TOPK, tb), lambda i, hd: (hd, i))
    nslot = PEER_HEADS * PEER_TOPK
    return pl.pallas_call(
        _peer_route_body,
        out_shape=(jax.ShapeDtypeStruct((nslot, m), i32), jax.ShapeDtypeStruct((nslot, m), i32),
                   jax.ShapeDtypeStruct((nslot, m), f32)),
        grid=(m // tb, PEER_HEADS),
        in_specs=[pl.BlockSpec((tb, 2 * LANE), lambda i, hd: (i, hd)),
                  pl.BlockSpec((None, 1, 2, PEER_NKEYS, LANE), lambda i, hd: (layer, hd, 0, 0, 0))],
        out_specs=(spec, spec, spec),
        compiler_params=_cparams(("parallel", "parallel")),
        name="peer_route",
    )(q, keys)


GATE_UNROLL = 32
RELAYOUT_TOKENS = 16


def _peer_gates_body(ia_ref, ib_ref, gt_ref, o_ref, a_s, b_s, g_s, gm_s):
    tg = ia_ref.shape[1]
    a_s[...] = ia_ref[...].astype(f32).T
    b_s[...] = ib_ref[...].astype(f32).T
    g_s[...] = gt_ref[...].T
    sub = lax.broadcasted_iota(i32, (LANE, LANE), 0).astype(f32)

    def step(r, carry):
        r8 = pl.multiple_of(r * GATE_UNROLL, GATE_UNROLL)
        a8 = a_s[pl.ds(r8, GATE_UNROLL), :]
        b8 = b_s[pl.ds(r8, GATE_UNROLL), :]
        g8 = g_s[pl.ds(r8, GATE_UNROLL), :]
        for u in range(GATE_UNROLL):
            at = jnp.where(sub == a8[u:u + 1, :], 1.0, 0.0).astype(bf16)
            bt = jnp.where(sub == b8[u:u + 1, :], g8[u:u + 1, :], 0.0).astype(bf16)
            gm = lax.dot_general(at, bt, (((1,), (1,)), ((), ())), preferred_element_type=f32)
            gm_s[r8 + u] = gm
        return carry

    lax.fori_loop(0, tg // GATE_UNROLL, step, 0)

    def relayout(c, carry):
        t0 = pl.multiple_of(c * RELAYOUT_TOKENS, RELAYOUT_TOKENS)
        for blk in range(PEER_NKEYS // 8):
            x = gm_s[pl.ds(t0, RELAYOUT_TOKENS), pl.ds(blk * 8, 8), :]
            y = jnp.swapaxes(x, 0, 1)
            for i in range(8):
                o_ref[pl.ds(t0, RELAYOUT_TOKENS), pl.ds((blk * 8 + i) * PEER_NKEYS, PEER_NKEYS)] = y[i].astype(bf16)
        return carry

    lax.fori_loop(0, tg // RELAYOUT_TOKENS, relayout, 0)


def _peer_gates(ia, ib, gt, tg):
    nslot, m = ia.shape
    spec = pl.BlockSpec((nslot, tg), lambda i: (0, i))
    return pl.pallas_call(
        _peer_gates_body,
        out_shape=jax.ShapeDtypeStruct((m, PEER_N), bf16),
        grid=(m // tg,),
        in_specs=[spec, spec, spec],
        out_specs=pl.BlockSpec((tg, PEER_N), lambda i: (i, 0)),
        scratch_shapes=[pltpu.VMEM((tg, nslot), f32)] * 3 + [pltpu.VMEM((tg, PEER_NKEYS, PEER_NKEYS), f32)],
        compiler_params=_cparams(("parallel",)),
        name="peer_gates",
    )(ia, ib, gt)


def _peer_experts_body(x_ref, nw_ref, u_ref, v_ref, g_ref, o_ref, xn_ref):
    @pl.when(pl.program_id(1) == 0)
    def _():
        x = x_ref[...]
        xn_ref[...] = _rms(x, nw_ref[...]).astype(bf16)
        o_ref[...] = x

    hmat = lax.dot_general(xn_ref[...], u_ref[...], (((1,), (1,)), ((), ())), preferred_element_type=f32)
    w = (_gelu(hmat) * g_ref[...].astype(f32)).astype(bf16)
    o_ref[...] += jnp.dot(w, v_ref[...], preferred_element_type=f32)


def _peer_experts(x, nw, u, v, g, tb, eb, layer):
    m = x.shape[0]
    return pl.pallas_call(
        _peer_experts_body,
        out_shape=jax.ShapeDtypeStruct((m, D_MODEL), f32),
        grid=(m // tb, PEER_N // eb),
        in_specs=[pl.BlockSpec((tb, D_MODEL), lambda i, j: (i, 0), pipeline_mode=pl.Buffered(1)),
                  pl.BlockSpec((1, D_MODEL), lambda i, j: (0, 0)),
                  pl.BlockSpec((None, eb, D_MODEL), lambda i, j: (layer, j, 0)),
                  pl.BlockSpec((None, eb, D_MODEL), lambda i, j: (layer, j, 0)),
                  pl.BlockSpec((tb, eb), lambda i, j: (i, j))],
        out_specs=pl.BlockSpec((tb, D_MODEL), lambda i, j: (i, 0), pipeline_mode=pl.Buffered(1)),
        scratch_shapes=[pltpu.VMEM((tb, D_MODEL), bf16)],
        compiler_params=_cparams(("parallel", "arbitrary")),
        name="peer_experts",
    )(x, nw, u, v, g)


_IN_OFFS = tuple(int(v) for v in np.cumsum((0,) + IN_SPLITS))
_O_CDT, _O_DQ, _O_DIG, _O_GATES, _O_END = _IN_OFFS[8], _IN_OFFS[9], _IN_OFFS[13], _IN_OFFS[15], _IN_OFFS[16]
PREP_ROWS = 128


def _prep_w_in_body(w_ref, om_ref, og_ref):
    rows = w_ref.shape[0]
    om_ref[:, 0:_O_CDT] = w_ref[:, 0:_O_CDT].astype(bf16)
    om_ref[:, _O_CDT:SMALL_COL] = w_ref[:, _O_DQ:_O_DIG].astype(bf16)
    small = jnp.concatenate([w_ref[:, _O_CDT:_O_DQ], w_ref[:, _O_DIG:_O_GATES],
                             jnp.zeros((rows, LANE - 16), w_ref.dtype)], axis=-1)
    om_ref[:, SMALL_COL:SMALL_COL + LANE] = small.astype(bf16)
    om_ref[:, SMALL_COL + LANE:MIX_W] = jnp.zeros((rows, MIX_W - SMALL_COL - LANE), bf16)
    og_ref[...] = w_ref[:, _O_GATES:_O_END].astype(bf16)


def _prep_w_in(w, layer):
    _, k, n = w.shape
    ng = _O_END - _O_GATES
    return pl.pallas_call(
        _prep_w_in_body,
        out_shape=(jax.ShapeDtypeStruct((k, MIX_W), bf16), jax.ShapeDtypeStruct((k, ng), bf16)),
        grid=(k // PREP_ROWS,),
        in_specs=[pl.BlockSpec((None, PREP_ROWS, n), lambda i: (layer, i, 0))],
        out_specs=(pl.BlockSpec((PREP_ROWS, MIX_W), lambda i: (i, 0)), pl.BlockSpec((PREP_ROWS, ng), lambda i: (i, 0))),
        compiler_params=_cparams(("parallel",)),
        name="prep_w_in",
    )(w)


def _lane_row(vals, start):
    row = jnp.zeros((LANE,), f32)
    return row.at[start:start + vals.shape[0]].set(vals)


def kernel(x_prompt, x_sample, state_hgrn, state_ssm, state_conv, state_mlstm_c, state_mlstm_n, state_mlstm_m, norm1_w, w_in, a_ln_w, a_ln_b, a_ws, a_bs, b_lb_logits, b_norm_w, c_conv_w, c_conv_b, c_dt_bias, c_a_log, c_d, c_norm_w, d_ig_b, d_fg_b, d_norm_w, w_branch, w_out, norm2_w, peer_wq, peer_keys, peer_u, peer_v, final_norm_w):
    x = jnp.concatenate([x_prompt.reshape(N_PROMPT, D_MODEL), x_sample.reshape(N_SAMPLE, D_MODEL)], axis=0)
    lbs = jax.nn.softmax(b_lb_logits.astype(f32), axis=0)
    lbs = jnp.cumsum(lbs, axis=0) - lbs[0]
    zeros8 = jnp.zeros((8, LANE), f32)
    news_p = []
    news_s = []
    stacked_s = None
    w_in_b, w_branch_b, w_out_b, wq_b = (w.astype(bf16) for w in (w_in, w_branch, w_out, peer_wq))
    keys_b, u_b, v_b = (w.astype(bf16) for w in (peer_keys, peer_u, peer_v))
    for l in range(DEPTH):
        w_mix, w_gate = _prep_w_in(w_in_b, l)
        h = _norm_mm(x, norm1_w[l][None, :], w_mix, f32, 1088, 1024)

        lnw = a_ln_w[l][None, :]
        lnb = a_ln_b[l][None, :]
        lb = lbs[l][None, :]
        bnw = b_norm_w[l][None, :]
        cw = c_conv_w[l]
        cb = c_conv_b[l][None, :]
        cpar = zeros8.at[0].set(_lane_row(c_dt_bias[l], 0)).at[1].set(_lane_row(c_a_log[l], 0)).at[2].set(_lane_row(c_d[l], 0))
        cnw = c_norm_w[l][None, :]
        dpar = zeros8.at[0].set(_lane_row(d_ig_b[l], 8) + _lane_row(d_fg_b[l], 12))
        dnw = d_norm_w[l][None, :]

        bs_full = jnp.repeat(a_bs[l].T, LANE, axis=1)
        br_a = _prompt_a(h, lnw, lnb, a_ws[l], bs_full)
        br_b, hg_p = _prompt_b(h, lb, bnw)
        br_c, ssm_p, cv_p = _prompt_c(h, cw, cb, cpar, cnw)
        br_d, mc_p, mn_p, mm_p = _prompt_d(h, dpar, dnw)
        news_p.append((hg_p, ssm_p, cv_p, mc_p, mn_p[:, :D_HEADS, :], mm_p[:, :D_HEADS, 0]))

        w4 = jnp.tril(a_ws[l][:, :DEC_SEQ, :DEC_SEQ])
        w8 = jnp.zeros((A_GROUPS, 8, 8), f32).at[:, :4, :4].set(w4).at[:, 4:, 4:].set(w4)
        ws8 = jnp.repeat(jnp.transpose(w8, (2, 1, 0)), LANE, axis=2)
        bs8 = jnp.repeat(jnp.tile(a_bs[l][:, :DEC_SEQ], (1, 2)).T, LANE, axis=1)
        outs = _sample_mixer(h, state_hgrn, state_ssm, state_conv, state_mlstm_c, state_mlstm_n, state_mlstm_m,
                             (lnw, lnb, ws8, bs8), (lb, bnw), (cw, cb, cpar, cnw), (dpar, dnw), l,
                             (br_a, br_b, br_c, br_d), stacked_s)
        brs = outs[:4]
        stacked_s = (outs[5], outs[6], outs[8])
        news_s.append((outs[7], outs[9], outs[10], outs[4].reshape(DEC_BATCH, DEC_SEQ, BW)))

        mixin = _merge(x, norm1_w[l][None, :], brs, w_gate, w_branch_b, 1088, 1024, l)
        x = _mm_res(mixin, w_out_b, x, 1088, 512, l)

        q = _norm_mm(x, norm2_w[l][None, :], wq_b, bf16, 1088, 1024, l)
        ia, ib, gt = _peer_route(q, keys_b, 512, l)
        g = _peer_gates(ia, ib, gt, 128)
        x = _peer_experts(x, norm2_w[l][None, :], u_b, v_b, g, 1088, 1024, l)

    y_p, y_s = _final_norm(x, final_norm_w[None, :])
    y_prompt = y_p.reshape(BATCH, SEQ, D_MODEL)
    y_sample = y_s.reshape(DEC_BATCH, DEC_SEQ, D_MODEL)
    stack = lambda news, k: jnp.stack([n[k] for n in news], axis=0)
    hgrn_s, ssm_s, mc_s = stacked_s
    conv_s, mn_s, mm_s, chunk_v_s = (stack(news_s, k) for k in range(4))
    return ((y_prompt, y_sample) + tuple(stack(news_p, k) for k in range(6))
            + (hgrn_s, ssm_s, conv_s, mc_s, mn_s, mm_s, chunk_v_s))
```

```python
import functools

import jax
import jax.numpy as jnp
import numpy as np
from jax import lax
from jax.experimental import pallas as pl
from jax.experimental.pallas import tpu as pltpu

f32 = jnp.float32
bf16 = jnp.bfloat16
i32 = jnp.int32
HI = lax.Precision.HIGHEST

D_MODEL = 2048
BATCH = 4
SEQ = 2048
DEPTH = 2
DEC_BATCH = 128
DEC_SEQ = 4
N_BRANCH = 4
BW = 512
A_GROUPS = 4
A_CHUNK = 128
B_HEADS = 4
LB_FLOOR = 1e-30
C_HEADDIM = 64
C_HEADS = 8
C_GROUPS = 2
C_DSTATE = 128
C_CONV = 4
C_CONV_DIM = 1024
D_HEADS = 4
D_DK = 128
NEG_BIG = -1e30
IN_SPLITS = (512, 512, 512, 512, 512, 512, 512, 1024, 8, 512, 512, 512, 512, 4, 4, 8192)
PEER_NKEYS = 128
PEER_N = PEER_NKEYS * PEER_NKEYS
PEER_HEADS = 8
PEER_TOPK = 16
EPS = 1e-6
INV_SQRT2 = 0.7071067811865476

N_PROMPT = BATCH * SEQ
N_SAMPLE = DEC_BATCH * DEC_SEQ
N_TOK = N_PROMPT + N_SAMPLE

CB_AU, CB_AV, CB_BQ, CB_BF, CB_BI, CB_BG, CB_CZ, CB_CX0, CB_CX1, CB_DQ, CB_DK, CB_DV, CB_DO = range(13)
SMALL_COL = 13 * 512
MIX_W = 14 * 512
LANE = 128
CHUNK = 128
NCHUNK = SEQ // CHUNK
SUB = 16
SB = 8
VMEM_LIMIT = 56 * 1024 * 1024


def _gelu(x):
    return 0.5 * x * (1.0 + lax.erf(x * INV_SQRT2))


def _rms(x, w):
    ms = jnp.mean(x * x, axis=-1, keepdims=True)
    return x * lax.rsqrt(ms + EPS) * w


def _tri(n):
    r = lax.broadcasted_iota(i32, (n, n), 0)
    c = lax.broadcasted_iota(i32, (n, n), 1)
    return r >= c


def _cumsum_rows_small(x, n):
    row = lax.broadcasted_iota(i32, (n, 1), 0)
    acc = jnp.zeros_like(x)
    for s in range(n):
        acc = acc + jnp.where(row >= s, x[s:s + 1, :], 0.0)
    return acc


def _row_to_col(row, eye):
    return jnp.sum(jnp.where(eye, row, 0.0), axis=-1, keepdims=True)


def _cparams(sem, vmem=VMEM_LIMIT):
    return pltpu.CompilerParams(dimension_semantics=sem, vmem_limit_bytes=vmem)


def _norm_mm_body(x_ref, nw_ref, w_ref, o_ref, xn_ref):
    @pl.when(pl.program_id(1) == 0)
    def _():
        xn_ref[...] = _rms(x_ref[...], nw_ref[...]).astype(bf16)

    o_ref[...] = jnp.dot(xn_ref[...], w_ref[...], preferred_element_type=f32).astype(o_ref.dtype)


def _w_spec(w, layer, block, index_map):
    if layer is None:
        return pl.BlockSpec(block, index_map)
    return pl.BlockSpec((None,) + block, lambda *g: (layer,) + index_map(*g))


def _norm_mm(x, nw, w, out_dtype, tm, tn, layer=None):
    m, k = x.shape
    n = w.shape[-1]
    return pl.pallas_call(
        _norm_mm_body,
        out_shape=jax.ShapeDtypeStruct((m, n), out_dtype),
        grid=(m // tm, n // tn),
        in_specs=[pl.BlockSpec((tm, k), lambda i, j: (i, 0), pipeline_mode=pl.Buffered(1)),
                  pl.BlockSpec((1, k), lambda i, j: (0, 0)),
                  _w_spec(w, layer, (k, tn), lambda i, j: (0, j))],
        out_specs=pl.BlockSpec((tm, tn), lambda i, j: (i, j)),
        scratch_shapes=[pltpu.VMEM((tm, k), bf16)],
        compiler_params=_cparams(("parallel", "arbitrary")),
        name="norm_mm",
    )(x, nw, w)


def _mm_res_body(a_ref, w_ref, r_ref, o_ref):
    o_ref[...] = r_ref[...] + jnp.dot(a_ref[...], w_ref[...], preferred_element_type=f32)


def _mm_res(a, w, res, tm, tn, layer=None):
    m, k = a.shape
    n = w.shape[-1]
    return pl.pallas_call(
        _mm_res_body,
        out_shape=jax.ShapeDtypeStruct((m, n), f32),
        grid=(m // tm, n // tn),
        in_specs=[pl.BlockSpec((tm, k), lambda i, j: (i, 0)),
                  _w_spec(w, layer, (k, tn), lambda i, j: (0, j)),
                  pl.BlockSpec((tm, tn), lambda i, j: (i, j))],
        out_specs=pl.BlockSpec((tm, tn), lambda i, j: (i, j)),
        compiler_params=_cparams(("parallel", "arbitrary")),
        name="mm_res",
    )(a, w, res)


def _final_norm_body(x_ref, w_ref, op_ref, os_ref):
    y = _rms(x_ref[...], w_ref[...])
    i = pl.program_id(0)

    @pl.when(i < N_PROMPT // N_SAMPLE)
    def _():
        op_ref[...] = y

    @pl.when(i == N_PROMPT // N_SAMPLE)
    def _():
        os_ref[...] = y


def _final_norm(x, w):
    m, k = x.shape
    tm = N_SAMPLE
    last_p = N_PROMPT // tm - 1
    return pl.pallas_call(
        _final_norm_body,
        out_shape=(jax.ShapeDtypeStruct((N_PROMPT, k), f32), jax.ShapeDtypeStruct((N_SAMPLE, k), f32)),
        grid=(m // tm,),
        in_specs=[pl.BlockSpec((tm, k), lambda i: (i, 0)), pl.BlockSpec((1, k), lambda i: (0, 0))],
        out_specs=(pl.BlockSpec((tm, k), lambda i: (jnp.minimum(i, last_p), 0)),
                   pl.BlockSpec((tm, k), lambda i: (0, 0))),
        compiler_params=_cparams(("arbitrary",)),
        name="final_norm",
    )(x, w)


def _a_uv(au, av, lnw, lnb):
    u = _gelu(au)
    g = _gelu(av)
    xc = g - jnp.mean(g, axis=-1, keepdims=True)
    var = jnp.mean(xc * xc, axis=-1, keepdims=True)
    v = xc * lax.rsqrt(var + EPS) * lnw + lnb
    return u, v


def _b_pre(bq, bf_, lb):
    q = bq * jax.nn.sigmoid(bq)
    logf = jnp.logaddexp(jnp.log(jnp.maximum(lb, LB_FLOOR)), jnp.log1p(-lb) + jax.nn.log_sigmoid(bf_))
    kb = (1.0 - lb) * jax.nn.sigmoid(-bf_)
    return q, kb, logf


def _hgrn_chunks(chains, n, eye):
    row = lax.broadcasted_iota(i32, (n, 1), 0)
    qs, ks, vs, gs, ss = [], [], [], [], []
    for q, k, v, gl, s_mat, valid in chains:
        if valid is not None:
            gl = jnp.where(valid, gl, 0.0)
            k = jnp.where(valid, k, 0.0)
        qs.append(q)
        ks.append(k)
        vs.append(v)
        gs.append(_cumsum_rows_small(gl, n))
        ss.append(s_mat)
    nc = len(chains)
    os_ = [jnp.dot((qs[c] * jnp.exp(gs[c])).astype(bf16), ss[c].astype(bf16), preferred_element_type=f32)
           for c in range(nc)]
    upds = []
    for c in range(nc):
        k_dec = ks[c] * jnp.exp(gs[c][n - 1:n, :] - gs[c])
        upds.append(lax.dot_general(k_dec.astype(bf16), vs[c].astype(bf16), (((0,), (0,)), ((), ())),
                                    preferred_element_type=f32))
    for s in range(n):
        m = row >= s
        for c in range(nc):
            g = gs[c]
            d = jnp.where(m, g - g[s:s + 1, :], 0.0)
            p = jnp.where(m, qs[c] * ks[c][s:s + 1, :] * jnp.exp(d), 0.0)
            os_[c] = os_[c] + jnp.sum(p, axis=-1, keepdims=True) * vs[c][s:s + 1, :]
    out = []
    for c in range(nc):
        s_new = jnp.exp(_row_to_col(gs[c][n - 1:n, :], eye)) * ss[c] + upds[c]
        out.append((os_[c], s_new))
    return out


def _b_post(o, bg, nw):
    return _rms(o, nw) * (bg * jax.nn.sigmoid(bg))


def _hrow(b, c):
    return b * NCHUNK + c


def _h_spec(colblk):
    return pl.BlockSpec((CHUNK, BW), lambda b, c, cb=colblk: (_hrow(b, c), cb))


_SMALL_SPEC = pl.BlockSpec((CHUNK, LANE), lambda b, c: (_hrow(b, c), SMALL_COL // LANE))
_BR_SPEC = pl.BlockSpec((CHUNK, BW), lambda b, c: (_hrow(b, c), 0))


def _full_spec(shape):
    nd = len(shape)
    return pl.BlockSpec(shape, lambda b, c, nd=nd: (0,) * nd)


def _pa_body(au_ref, av_ref, lnw_ref, lnb_ref, ws_ref, bs_ref, o_ref):
    u, v = _a_uv(au_ref[...], av_ref[...], lnw_ref[...], lnb_ref[...])
    tri = _tri(CHUNK)
    vb = v.astype(bf16)
    parts = []
    for g in range(A_GROUPS):
        w = jnp.where(tri, ws_ref[g], 0.0).astype(bf16)
        parts.append(jnp.dot(w, vb[:, g * LANE:(g + 1) * LANE], preferred_element_type=f32))
    sp = jnp.concatenate(parts, axis=-1) + bs_ref[...]
    o_ref[...] = (u * sp).astype(bf16)


def _prompt_a(h, lnw, lnb, ws, bs_full):
    return pl.pallas_call(
        _pa_body,
        out_shape=jax.ShapeDtypeStruct((N_TOK, BW), bf16),
        grid=(BATCH, NCHUNK),
        in_specs=[_h_spec(CB_AU), _h_spec(CB_AV), _full_spec((1, BW)), _full_spec((1, BW)),
                  _full_spec((A_GROUPS, A_CHUNK, A_CHUNK)), _full_spec((A_CHUNK, BW))],
        out_specs=_BR_SPEC,
        compiler_params=_cparams(("parallel", "parallel")),
        name="prompt_gmlp",
    )(h, h, lnw, lnb, ws, bs_full)


def _pb_body(bq_ref, bf_ref, bi_ref, bg_ref, lb_ref, nw_ref, o_ref, st_ref, s_ref, q_s, k_s, v_s, g_s, o_s):
    c = pl.program_id(1)

    @pl.when(c == 0)
    def _():
        s_ref[...] = jnp.zeros_like(s_ref)

    q, kb, logf = _b_pre(bq_ref[...], bf_ref[...], lb_ref[...])
    q_s[...] = q
    k_s[...] = kb
    v_s[...] = bi_ref[...]
    g_s[...] = logf
    eye = lax.broadcasted_iota(i32, (LANE, LANE), 0) == lax.broadcasted_iota(i32, (LANE, LANE), 1)
    def sub(j, carry):
        r = pl.multiple_of(j * SUB, SUB)
        heads = [slice(hd * LANE, (hd + 1) * LANE) for hd in range(B_HEADS)]
        chains = [(q_s[pl.ds(r, SUB), hs], k_s[pl.ds(r, SUB), hs], v_s[pl.ds(r, SUB), hs], g_s[pl.ds(r, SUB), hs],
                   s_ref[hd], None) for hd, hs in enumerate(heads)]
        for hd, (o, s_new) in enumerate(_hgrn_chunks(chains, SUB, eye)):
            o_s[pl.ds(r, SUB), heads[hd]] = o
            s_ref[hd] = s_new
        return carry

    lax.fori_loop(0, CHUNK // SUB, sub, 0, unroll=2)
    bg = bg_ref[...]
    nw = nw_ref[...]
    for hd in range(B_HEADS):
        hs = slice(hd * LANE, (hd + 1) * LANE)
        o_ref[:, hs] = _b_post(o_s[:, hs], bg[:, hs], nw).astype(bf16)

    @pl.when(c == NCHUNK - 1)
    def _():
        st_ref[0] = s_ref[...]


def _prompt_b(h, lb, nw):
    return pl.pallas_call(
        _pb_body,
        out_shape=(jax.ShapeDtypeStruct((N_TOK, BW), bf16),
                   jax.ShapeDtypeStruct((BATCH, B_HEADS, LANE, LANE), f32)),
        grid=(BATCH, NCHUNK),
        in_specs=[_h_spec(CB_BQ), _h_spec(CB_BF), _h_spec(CB_BI), _h_spec(CB_BG),
                  _full_spec((1, BW)), _full_spec((1, LANE))],
        out_specs=(_BR_SPEC, pl.BlockSpec((1, B_HEADS, LANE, LANE), lambda b, c: (b, 0, 0, 0))),
        scratch_shapes=[pltpu.VMEM((B_HEADS, LANE, LANE), f32)] + [pltpu.VMEM((CHUNK, BW), f32)] * 5,
        compiler_params=_cparams(("parallel", "arbitrary")),
        name="prompt_hgrn",
    )(h, h, h, h, lb, nw)


def _c_conv_silu(win0, win1, win2, win3, cw_ref, cb_ref):
    y = cb_ref[...] + win0 * cw_ref[0:1, :] + win1 * cw_ref[1:2, :] + win2 * cw_ref[2:3, :] + win3 * cw_ref[3:4, :]
    return y * jax.nn.sigmoid(y)


def _c_post(yc, cz, nw):
    y = yc * (cz * jax.nn.sigmoid(cz))
    gw = BW // C_GROUPS
    parts = [_rms(y[:, g * gw:(g + 1) * gw], nw[:, g * gw:(g + 1) * gw]) for g in range(C_GROUPS)]
    return jnp.concatenate(parts, axis=-1)


def _pc_body(cz_ref, cx0_ref, cx1_ref, sm_ref, cw_ref, cb_ref, par_ref, nw_ref,
             o_ref, st_ref, cv_ref, xpad, sp_ref):
    c = pl.program_id(1)

    @pl.when(c == 0)
    def _():
        xpad[pl.ds(0, 8), :] = jnp.zeros((8, C_CONV_DIM), f32)
        sp_ref[...] = jnp.zeros_like(sp_ref)

    xpad[pl.ds(8, CHUNK), 0:BW] = cx0_ref[...]
    xpad[pl.ds(8, CHUNK), BW:2 * BW] = cx1_ref[...]
    xbc = _c_conv_silu(xpad[pl.ds(5, CHUNK), :], xpad[pl.ds(6, CHUNK), :], xpad[pl.ds(7, CHUNK), :],
                       xpad[pl.ds(8, CHUNK), :], cw_ref, cb_ref)

    @pl.when(c == NCHUNK - 1)
    def _():
        cv_ref[0] = xpad[pl.ds(CHUNK + 5, 3), :]

    xpad[pl.ds(0, 8), :] = xpad[pl.ds(CHUNK, 8), :]

    xs = xbc[:, 0:BW]
    bm = xbc[:, BW:BW + 2 * C_DSTATE]
    cm = xbc[:, BW + 2 * C_DSTATE:]
    par = par_ref[...]
    dt = jax.nn.softplus(sm_ref[...] + par[0:1, :])
    gl = dt * (-jnp.exp(par[1:2, :]))
    tri = _tri(CHUNK)
    g = jnp.dot(tri.astype(f32), gl, precision=HI, preferred_element_type=f32)
    gt = g.T
    dtt = dt.T
    lane = lax.broadcasted_iota(i32, (1, LANE), 1)
    lo = lane < C_HEADDIM
    cbs = []
    for grp in range(C_GROUPS):
        cg = cm[:, grp * C_DSTATE:(grp + 1) * C_DSTATE].astype(bf16)
        bg = bm[:, grp * C_DSTATE:(grp + 1) * C_DSTATE].astype(bf16)
        cbs.append(lax.dot_general(cg, bg, (((1,), (1,)), ((), ())), preferred_element_type=f32))
    ys = []
    for pr in range(C_HEADS // 2):
        xp = xs[:, pr * LANE:(pr + 1) * LANE]
        sp = sp_ref[pr]
        y = jnp.zeros((CHUNK, LANE), f32)
        upd = jnp.zeros((C_DSTATE, LANE), f32)
        dl = jnp.zeros((1, LANE), f32)
        cdl = jnp.zeros((1, LANE), f32)
        for sub in range(2):
            hd = 2 * pr + sub
            grp = hd // (C_HEADS // C_GROUPS)
            lm = lo if sub == 0 else jnp.logical_not(lo)
            col = g[:, hd:hd + 1]
            g_last = col[CHUNK - 1:CHUNK, :]
            dec = jnp.exp(jnp.where(tri, col - gt[hd:hd + 1, :], 0.0))
            sc = jnp.where(tri, cbs[grp] * dec * dtt[hd:hd + 1, :], 0.0)
            xm = jnp.where(lm, xp, 0.0).astype(bf16)
            cg = cm[:, grp * C_DSTATE:(grp + 1) * C_DSTATE]
            bg = bm[:, grp * C_DSTATE:(grp + 1) * C_DSTATE]
            y = y + jnp.dot(sc.astype(bf16), xm, preferred_element_type=f32)
            y = y + jnp.dot((cg * jnp.exp(col)).astype(bf16), jnp.where(lm, sp, 0.0).astype(bf16),
                            preferred_element_type=f32)
            kd = bg * (dt[:, hd:hd + 1] * jnp.exp(g_last - col))
            upd = upd + lax.dot_general(kd.astype(bf16), xm, (((0,), (0,)), ((), ())), preferred_element_type=f32)
            dl = jnp.where(lm, jnp.exp(g_last), dl)
            cdl = jnp.where(lm, par[2:3, hd:hd + 1], cdl)
        sp_ref[pr] = dl * sp + upd
        ys.append(y + cdl * xp)
    yc = jnp.concatenate(ys, axis=-1)
    o_ref[...] = _c_post(yc, cz_ref[...], nw_ref[...]).astype(bf16)

    @pl.when(c == NCHUNK - 1)
    def _():
        for pr in range(C_HEADS // 2):
            st_ref[0, 2 * pr] = sp_ref[pr][:, 0:C_HEADDIM]
            st_ref[0, 2 * pr + 1] = sp_ref[pr][:, C_HEADDIM:]


def _prompt_c(h, cw, cb, par, nw):
    return pl.pallas_call(
        _pc_body,
        out_shape=(jax.ShapeDtypeStruct((N_TOK, BW), bf16),
                   jax.ShapeDtypeStruct((BATCH, C_HEADS, C_DSTATE, C_HEADDIM), f32),
                   jax.ShapeDtypeStruct((BATCH, C_CONV - 1, C_CONV_DIM), f32)),
        grid=(BATCH, NCHUNK),
        in_specs=[_h_spec(CB_CZ), _h_spec(CB_CX0), _h_spec(CB_CX1), _SMALL_SPEC,
                  _full_spec((C_CONV, C_CONV_DIM)), _full_spec((1, C_CONV_DIM)), _full_spec((8, LANE)),
                  _full_spec((1, BW))],
        out_specs=(_BR_SPEC,
                   pl.BlockSpec((1, C_HEADS, C_DSTATE, C_HEADDIM), lambda b, c: (b, 0, 0, 0)),
                   pl.BlockSpec((1, C_CONV - 1, C_CONV_DIM), lambda b, c: (b, 0, 0))),
        scratch_shapes=[pltpu.VMEM((CHUNK + 8, C_CONV_DIM), f32), pltpu.VMEM((C_HEADS // 2, C_DSTATE, LANE), f32)],
        compiler_params=_cparams(("parallel", "arbitrary")),
        name="prompt_ssd",
    )(h, h, h, h, cw, cb, par, nw)


def _pd_body(dq_ref, dk_ref, dv_ref, do_ref, sm_ref, par_ref, nw_ref,
             o_ref, c_out, n_out, m_out, c_ref, n_ref, m_ref):
    c = pl.program_id(1)

    @pl.when(c == 0)
    def _():
        c_ref[...] = jnp.zeros_like(c_ref)
        n_ref[...] = jnp.zeros_like(n_ref)
        m_ref[...] = jnp.zeros_like(m_ref)

    sm = sm_ref[...] + par_ref[0:1, :]
    ls = jax.nn.log_sigmoid(sm)
    tri = _tri(CHUNK)
    bc = jnp.dot(tri.astype(f32), ls, precision=HI, preferred_element_type=f32)
    bct = bc.T
    smt = sm.T
    nw = nw_ref[...]
    for hd in range(D_HEADS):
        hs = slice(hd * LANE, (hd + 1) * LANE)
        li, lf = 8 + hd, 12 + hd
        bcol = bc[:, lf:lf + 1]
        brow = bct[lf:lf + 1, :]
        irow = smt[li:li + 1, :]
        icol = sm[:, li:li + 1]
        mprev = m_ref[hd:hd + 1, 0:1]
        dmat = jnp.where(tri, bcol - brow + irow, NEG_BIG)
        inter = bcol + mprev
        mt = jnp.maximum(inter, jnp.max(dmat, axis=-1, keepdims=True))
        w_intra = jnp.where(tri, jnp.exp(dmat - mt), 0.0)
        w_inter = jnp.exp(inter - mt)
        qh = dq_ref[:, hs] * (D_DK ** -0.5)
        kh = dk_ref[:, hs]
        vh = dv_ref[:, hs]
        qb = qh.astype(bf16)
        qk = lax.dot_general(qb, kh.astype(bf16), (((1,), (1,)), ((), ())), preferred_element_type=f32) * w_intra
        num = w_inter * jnp.dot(qb, c_ref[hd].astype(bf16), preferred_element_type=f32)
        num = num + jnp.dot(qk.astype(bf16), vh.astype(bf16), preferred_element_type=f32)
        den = w_inter * jnp.sum(qh * n_ref[hd:hd + 1, :], axis=-1, keepdims=True) + jnp.sum(qk, axis=-1, keepdims=True)
        hh = num / jnp.maximum(jnp.abs(den), jnp.exp(-mt))
        mnew = mt[CHUNK - 1:CHUNK, :]
        blast = bcol[CHUNK - 1:CHUNK, :]
        wk = jnp.exp(blast - bcol + icol - mnew)
        decay = jnp.exp(blast + mprev - mnew)
        wkk = wk * kh
        c_ref[hd] = decay * c_ref[hd] + lax.dot_general(wkk.astype(bf16), vh.astype(bf16), (((0,), (0,)), ((), ())),
                                                        preferred_element_type=f32)
        n_ref[hd:hd + 1, :] = decay * n_ref[hd:hd + 1, :] + jnp.sum(wkk, axis=0, keepdims=True)
        m_ref[hd:hd + 1, :] = jnp.broadcast_to(mnew, (1, LANE))
        o_ref[:, hs] = (jax.nn.sigmoid(do_ref[:, hs]) * _rms(hh, nw[:, hs])).astype(bf16)

    @pl.when(c == NCHUNK - 1)
    def _():
        c_out[0] = c_ref[...]
        n_out[0] = n_ref[...]
        m_out[0] = m_ref[...]


def _prompt_d(h, par, nw):
    return pl.pallas_call(
        _pd_body,
        out_shape=(jax.ShapeDtypeStruct((N_TOK, BW), bf16),
                   jax.ShapeDtypeStruct((BATCH, D_HEADS, LANE, LANE), f32),
                   jax.ShapeDtypeStruct((BATCH, 8, LANE), f32),
                   jax.ShapeDtypeStruct((BATCH, 8, LANE), f32)),
        grid=(BATCH, NCHUNK),
        in_specs=[_h_spec(CB_DQ), _h_spec(CB_DK), _h_spec(CB_DV), _h_spec(CB_DO), _SMALL_SPEC,
                  _full_spec((8, LANE)), _full_spec((1, BW))],
        out_specs=(_BR_SPEC,
                   pl.BlockSpec((1, D_HEADS, LANE, LANE), lambda b, c: (b, 0, 0, 0)),
                   pl.BlockSpec((1, 8, LANE), lambda b, c: (b, 0, 0)),
                   pl.BlockSpec((1, 8, LANE), lambda b, c: (b, 0, 0))),
        scratch_shapes=[pltpu.VMEM((D_HEADS, LANE, LANE), f32), pltpu.VMEM((8, LANE), f32), pltpu.VMEM((8, LANE), f32)],
        compiler_params=_cparams(("parallel", "arbitrary")),
        name="prompt_mlstm",
    )(h, h, h, h, h, par, nw)


def _sample_body(h_ref, hg_ref, ssm_ref, cv_ref, mc_ref, mn_ref, mm_ref,
                 lnw_ref, lnb_ref, ws8_ref, bs8_ref, lb_ref, bnw_ref,
                 cw_ref, cb_ref, cpar_ref, cnw_ref, dpar_ref, dnw_ref,
                 oa_ref, ob_ref, oc_ref, od_ref, chv_ref,
                 hg_out, ssm_out, cv_out, mc_out, mn_out, mm_out):
    row = lax.broadcasted_iota(i32, (8, 1), 0)
    first = row < DEC_SEQ
    tpos = row & (DEC_SEQ - 1)
    seg = [((row >= s) & first) if s < DEC_SEQ else (row >= s) for s in range(8)]

    def cumsum_seg(x):
        acc = jnp.zeros_like(x)
        for s in range(8):
            acc = acc + jnp.where(seg[s], x[s:s + 1, :], 0.0)
        return acc

    def both(x):
        return jnp.concatenate([jnp.where(first, x, 0.0), jnp.where(first, 0.0, x)], axis=-1)

    def last_rows(x):
        return jnp.where(first, x[DEC_SEQ - 1:DEC_SEQ, :], x[7:8, :])
    eye = lax.broadcasted_iota(i32, (LANE, LANE), 0) == lax.broadcasted_iota(i32, (LANE, LANE), 1)
    lane = lax.broadcasted_iota(i32, (1, LANE), 1)
    lo = lane < C_HEADDIM

    def tile(p, carry):
        r = pl.multiple_of(p * 8, 8)

        def col(blk, width=BW):
            return h_ref[pl.ds(r, 8), blk * BW:blk * BW + width]

        small = h_ref[pl.ds(r, 8), SMALL_COL:SMALL_COL + LANE]

        u, v = _a_uv(col(CB_AU), col(CB_AV), lnw_ref[...], lnb_ref[...])
        chv_ref[pl.ds(r, 8), :] = v
        sp = bs8_ref[...]
        for s in range(8):
            sp = sp + ws8_ref[s] * v[s:s + 1, :]
        oa_ref[pl.ds(r, 8), :] = (u * sp).astype(bf16)

        q, kb, logf = _b_pre(col(CB_BQ), col(CB_BF), lb_ref[...])
        bi = col(CB_BI)
        bg = col(CB_BG)
        for hd in range(B_HEADS):
            hs = slice(hd * LANE, (hd + 1) * LANE)
            qh, kh, vh = q[:, hs], kb[:, hs], bi[:, hs]
            g = cumsum_seg(logf[:, hs])
            s_a = hg_ref[2 * p, hd]
            s_b = hg_ref[2 * p + 1, hd]
            o = jnp.dot(both(qh * jnp.exp(g)).astype(bf16), jnp.concatenate([s_a, s_b], axis=0).astype(bf16),
                        preferred_element_type=f32)
            for s in range(8):
                m = seg[s]
                d = jnp.where(m, g - g[s:s + 1, :], 0.0)
                pp = jnp.where(m, qh * kh[s:s + 1, :] * jnp.exp(d), 0.0)
                o = o + jnp.sum(pp, axis=-1, keepdims=True) * vh[s:s + 1, :]
            k_dec = kh * jnp.exp(last_rows(g) - g)
            upd = lax.dot_general(k_dec.astype(bf16), both(vh).astype(bf16), (((0,), (0,)), ((), ())),
                                  preferred_element_type=f32)
            hg_out[2 * p, hd] = jnp.exp(_row_to_col(g[DEC_SEQ - 1:DEC_SEQ, :], eye)) * s_a + upd[:, 0:LANE]
            hg_out[2 * p + 1, hd] = jnp.exp(_row_to_col(g[7:8, :], eye)) * s_b + upd[:, LANE:]
            ob_ref[pl.ds(r, 8), hs] = _b_post(o, bg[:, hs], bnw_ref[...]).astype(bf16)

        x = h_ref[pl.ds(r, 8), CB_CX0 * BW:CB_CX0 * BW + C_CONV_DIM]
        bufs = [jnp.where(first, cv_ref[2 * p, k:k + 1, :], cv_ref[2 * p + 1, k:k + 1, :]) for k in range(3)]
        r1 = pltpu.roll(x, 1, 0)
        r2 = pltpu.roll(x, 2, 0)
        r3 = pltpu.roll(x, 3, 0)
        sh1 = jnp.where(tpos >= 1, r1, bufs[2])
        sh2 = jnp.where(tpos >= 2, r2, jnp.where(tpos == 0, bufs[1], bufs[2]))
        sh3 = jnp.where(tpos >= 3, r3, jnp.where(tpos == 0, bufs[0], jnp.where(tpos == 1, bufs[1], bufs[2])))
        xbc = _c_conv_silu(sh3, sh2, sh1, x, cw_ref, cb_ref)
        cv_out[2 * p] = pltpu.roll(x, 7, 0)[0:3, :]
        cv_out[2 * p + 1] = r3[0:3, :]
        xs = xbc[:, 0:BW]
        bm = xbc[:, BW:BW + 2 * C_DSTATE]
        cm = xbc[:, BW + 2 * C_DSTATE:]
        cpar = cpar_ref[...]
        dt = jax.nn.softplus(small + cpar[0:1, :])
        gl_all = dt * (-jnp.exp(cpar[1:2, :]))
        g = cumsum_seg(gl_all)
        g_lastr = last_rows(g)
        dots = []
        for grp in range(C_GROUPS):
            cg = cm[:, grp * C_DSTATE:(grp + 1) * C_DSTATE]
            bgp = bm[:, grp * C_DSTATE:(grp + 1) * C_DSTATE]
            dots.append([jnp.sum(cg * bgp[s:s + 1, :], axis=-1, keepdims=True) for s in range(8)])
        ys = []
        for pr in range(C_HEADS // 2):
            xp = xs[:, pr * LANE:(pr + 1) * LANE]
            grp = (2 * pr) // (C_HEADS // C_GROUPS)
            cg = cm[:, grp * C_DSTATE:(grp + 1) * C_DSTATE]
            bgp = bm[:, grp * C_DSTATE:(grp + 1) * C_DSTATE]
            spairs = [jnp.concatenate([ssm_ref[2 * p + w, 2 * pr], ssm_ref[2 * p + w, 2 * pr + 1]], axis=-1)
                      for w in range(2)]
            y = jnp.zeros((8, LANE), f32)
            ces, kds, xms = [], [], []
            for sub in range(2):
                hd = 2 * pr + sub
                lm = lo if sub == 0 else jnp.logical_not(lo)
                gcol = g[:, hd:hd + 1]
                xm = jnp.where(lm, xp, 0.0)
                for s in range(8):
                    coef = dots[grp][s] * jnp.where(seg[s], jnp.exp(jnp.where(seg[s], gcol - gcol[s:s + 1, :], 0.0))
                                                    * dt[s:s + 1, hd:hd + 1], 0.0)
                    y = y + coef * xm[s:s + 1, :]
                ces.append(cg * jnp.exp(gcol))
                kds.append(bgp * (dt[:, hd:hd + 1] * jnp.exp(g_lastr[:, hd:hd + 1] - gcol)))
                xms.append(xm)
            lhs = jnp.concatenate([jnp.where(first, ces[0], 0.0), jnp.where(first, ces[1], 0.0),
                                   jnp.where(first, 0.0, ces[0]), jnp.where(first, 0.0, ces[1])], axis=-1)
            rhs = jnp.concatenate([jnp.where(lo, spairs[0], 0.0), jnp.where(lo, 0.0, spairs[0]),
                                   jnp.where(lo, spairs[1], 0.0), jnp.where(lo, 0.0, spairs[1])], axis=0)
            y = y + jnp.dot(lhs.astype(bf16), rhs.astype(bf16), preferred_element_type=f32)
            kd16 = jnp.concatenate(kds, axis=0)
            xm16 = jnp.concatenate([both(xms[0]), both(xms[1])], axis=0)
            upd = lax.dot_general(kd16.astype(bf16), xm16.astype(bf16), (((0,), (0,)), ((), ())),
                                  preferred_element_type=f32)
            for w in range(2):
                last = DEC_SEQ - 1 if w == 0 else 7
                dl = jnp.where(lo, jnp.exp(g[last:last + 1, 2 * pr:2 * pr + 1]),
                               jnp.exp(g[last:last + 1, 2 * pr + 1:2 * pr + 2]))
                snew = dl * spairs[w] + upd[:, w * LANE:(w + 1) * LANE]
                ssm_out[2 * p + w, 2 * pr] = snew[:, 0:C_HEADDIM]
                ssm_out[2 * p + w, 2 * pr + 1] = snew[:, C_HEADDIM:]
            cdl = jnp.where(lo, cpar[2:3, 2 * pr:2 * pr + 1], cpar[2:3, 2 * pr + 1:2 * pr + 2])
            ys.append(y + cdl * xp)
        yc = jnp.concatenate(ys, axis=-1)
        oc_ref[pl.ds(r, 8), :] = _c_post(yc, col(CB_CZ), cnw_ref[...]).astype(bf16)

        smd = small + dpar_ref[0:1, :]
        lsd = jax.nn.log_sigmoid(smd)
        dq = col(CB_DQ)
        dk = col(CB_DK)
        dv = col(CB_DV)
        do = col(CB_DO)
        dnw = dnw_ref[...]
        bcs = cumsum_seg(lsd)
        mrows = [mm_ref[pl.ds(2 * p + w, 1), :] for w in range(2)]
        mnew_rows = [jnp.zeros((1, D_HEADS), f32), jnp.zeros((1, D_HEADS), f32)]
        hlane = lax.broadcasted_iota(i32, (1, D_HEADS), 1)
        for hd in range(D_HEADS):
            hs = slice(hd * LANE, (hd + 1) * LANE)
            li, lf = 8 + hd, 12 + hd
            bcol = bcs[:, lf:lf + 1]
            icol = smd[:, li:li + 1]
            mprev = jnp.where(first, mrows[0][:, hd:hd + 1], mrows[1][:, hd:hd + 1])
            dcols = [jnp.where(seg[s], bcol - bcol[s:s + 1, :] + icol[s:s + 1, :], NEG_BIG) for s in range(8)]
            mx = dcols[0]
            for s in range(1, 8):
                mx = jnp.maximum(mx, dcols[s])
            inter = bcol + mprev
            mt = jnp.maximum(inter, mx)
            w_inter = jnp.exp(inter - mt)
            qh = dq[:, hs] * (D_DK ** -0.5)
            kh = dk[:, hs]
            vh = dv[:, hs]
            cmats = [mc_ref[2 * p + w, hd] for w in range(2)]
            nrows = [mn_ref[2 * p + w, pl.ds(hd, 1), :] for w in range(2)]
            num = w_inter * jnp.dot(both(qh).astype(bf16), jnp.concatenate(cmats, axis=0).astype(bf16),
                                    preferred_element_type=f32)
            den = w_inter * jnp.sum(qh * jnp.where(first, nrows[0], nrows[1]), axis=-1, keepdims=True)
            for s in range(8):
                w = jnp.where(seg[s], jnp.exp(dcols[s] - mt), 0.0)
                qk = jnp.sum(qh * kh[s:s + 1, :], axis=-1, keepdims=True) * w
                num = num + qk * vh[s:s + 1, :]
                den = den + qk
            hh = num / jnp.maximum(jnp.abs(den), jnp.exp(-mt))
            wk = jnp.exp(last_rows(bcol) - bcol + icol - last_rows(mt))
            wkk = wk * kh
            upd = lax.dot_general(wkk.astype(bf16), both(vh).astype(bf16), (((0,), (0,)), ((), ())),
                                  preferred_element_type=f32)
            for w in range(2):
                last = DEC_SEQ - 1 if w == 0 else 7
                mnew = mt[last:last + 1, :]
                decay = jnp.exp(bcol[last:last + 1, :] + mrows[w][:, hd:hd + 1] - mnew)
                mine = first if w == 0 else jnp.logical_not(first)
                mc_out[2 * p + w, hd] = decay * cmats[w] + upd[:, w * LANE:(w + 1) * LANE]
                mn_out[2 * p + w, pl.ds(hd, 1), :] = decay * nrows[w] + jnp.sum(jnp.where(mine, wkk, 0.0), axis=0,
                                                                                 keepdims=True)
                mnew_rows[w] = jnp.where(hlane == hd, mnew, mnew_rows[w])
            od_ref[pl.ds(r, 8), hs] = (jax.nn.sigmoid(do[:, hs]) * _rms(hh, dnw[:, hs])).astype(bf16)
        for w in range(2):
            mm_out[pl.ds(2 * p + w, 1), :] = mnew_rows[w]
        return carry

    lax.fori_loop(0, SB // 2, tile, 0)


_N_SAMPLE_IN = 19


def _sample_body_aliased(n_alias, *refs):
    _sample_body(*refs[:_N_SAMPLE_IN], *refs[_N_SAMPLE_IN + n_alias:])


def _sample_mixer(h, st_hgrn, st_ssm, st_conv, st_c, st_n, st_m, pa, pb, pc, pd, layer, brs, stacked):
    rows = SB * DEC_SEQ
    row0 = N_PROMPT // rows

    def blk(shape):
        nd = len(shape)
        return pl.BlockSpec((SB,) + shape, lambda i, nd=nd: (i,) + (0,) * nd)

    def blk_stacked(shape):
        nd = len(shape)
        return pl.BlockSpec((None, SB) + shape, lambda i, nd=nd: (layer, i) + (0,) * nd)

    def full(shape):
        nd = len(shape)
        return pl.BlockSpec(shape, lambda i, nd=nd: (0,) * nd)

    big = ((B_HEADS, LANE, LANE), (C_HEADS, C_DSTATE, C_HEADDIM), (D_HEADS, LANE, LANE))
    in_state_specs = [blk_stacked(big[0]), blk_stacked(big[1]), blk_stacked((C_CONV - 1, C_CONV_DIM)),
                      blk_stacked(big[2]), blk_stacked((D_HEADS, LANE)), blk_stacked((D_HEADS,))]
    out_state_specs = [blk_stacked(big[0]), blk_stacked(big[1]), blk((C_CONV - 1, C_CONV_DIM)), blk_stacked(big[2]),
                       blk((D_HEADS, LANE)), blk((D_HEADS,))]
    params = list(pa) + list(pb) + list(pc) + list(pd)
    br_spec = pl.BlockSpec((rows, BW), lambda i: (row0 + i, 0))
    br_shape = jax.ShapeDtypeStruct((N_TOK, BW), bf16)
    stacked_shape = lambda a: jax.ShapeDtypeStruct(a.shape, f32)
    per_layer = lambda a: jax.ShapeDtypeStruct(a.shape[1:], f32)
    inputs = [h, st_hgrn, st_ssm, st_conv, st_c, st_n, st_m] + params
    in_specs = [pl.BlockSpec((rows, MIX_W), lambda i: (row0 + i, 0))] + in_state_specs + [full(p.shape) for p in params]
    assert len(inputs) == _N_SAMPLE_IN
    alias_in = list(brs) + (list(stacked) if stacked is not None else [])
    alias_out = [0, 1, 2, 3] + ([5, 6, 8] if stacked is not None else [])
    aliases = {_N_SAMPLE_IN + k: o for k, o in enumerate(alias_out)}
    return pl.pallas_call(
        functools.partial(_sample_body_aliased, len(alias_in)),
        out_shape=(br_shape, br_shape, br_shape, br_shape, jax.ShapeDtypeStruct((N_SAMPLE, BW), f32),
                   stacked_shape(st_hgrn), stacked_shape(st_ssm),
                   per_layer(st_conv), stacked_shape(st_c), per_layer(st_n), per_layer(st_m)),
        grid=(DEC_BATCH // SB,),
        in_specs=in_specs + [pl.BlockSpec(memory_space=pl.ANY)] * len(alias_in),
        out_specs=(br_spec, br_spec, br_spec, br_spec, pl.BlockSpec((rows, BW), lambda i: (i, 0)))
        + tuple(out_state_specs),
        input_output_aliases=aliases,
        compiler_params=_cparams(("parallel",)),
        name="sample_mixer",
    )(*inputs, *alias_in)


def _merge_body(x_ref, nw_ref, ba_ref, bb_ref, bc_ref, bd_ref, wg_ref, wb_ref, o_ref, xn_ref, acc_ref):
    jc = pl.program_id(1)
    n = pl.program_id(2)

    @pl.when((jc == 0) & (n == 0))
    def _():
        xn_ref[...] = _rms(x_ref[...], nw_ref[...]).astype(bf16)

    gate = jax.nn.sigmoid(jnp.dot(xn_ref[...], wg_ref[...], preferred_element_type=f32))
    for k, br_ref in enumerate((ba_ref, bb_ref, bc_ref, bd_ref)):
        @pl.when(n == k)
        def _(br_ref=br_ref, k=k):
            contrib = gate * jnp.dot(br_ref[...], wb_ref[0], preferred_element_type=f32)
            if k == 0:
                acc_ref[...] = contrib
            else:
                acc_ref[...] = acc_ref[...] + contrib

    @pl.when(n == N_BRANCH - 1)
    def _():
        o_ref[...] = acc_ref[...].astype(bf16)


def _merge(x, nw, brs, w_gate, w_branch, tm, tn, layer):
    m = x.shape[0]
    ncol = D_MODEL // tn
    br_spec = pl.BlockSpec((tm, BW), lambda i, jc, n: (i, 0))
    return pl.pallas_call(
        _merge_body,
        out_shape=jax.ShapeDtypeStruct((m, D_MODEL), bf16),
        grid=(m // tm, ncol, N_BRANCH),
        in_specs=[pl.BlockSpec((tm, D_MODEL), lambda i, jc, n: (i, 0)),
                  pl.BlockSpec((1, D_MODEL), lambda i, jc, n: (0, 0)),
                  br_spec, br_spec, br_spec, br_spec,
                  pl.BlockSpec((D_MODEL, tn), lambda i, jc, n: (0, n * ncol + jc)),
                  pl.BlockSpec((None, 1, BW, tn), lambda i, jc, n: (layer, n, 0, jc))],
        out_specs=pl.BlockSpec((tm, tn), lambda i, jc, n: (i, jc)),
        scratch_shapes=[pltpu.VMEM((tm, D_MODEL), bf16), pltpu.VMEM((tm, tn), f32)],
        compiler_params=_cparams(("parallel", "arbitrary", "arbitrary")),
        name="merge",
    )(x, nw, *brs, w_gate, w_branch)


def _top16_rows(s, rid=None):
    if rid is None:
        rid = lax.broadcasted_iota(i32, s.shape, 0).astype(f32)
    out = []
    for _ in range(PEER_TOPK):
        m = jnp.max(s, axis=0, keepdims=True)
        am = jnp.min(jnp.where(s == m, rid, float(PEER_N)), axis=0, keepdims=True)
        out.append((m, am))
        s = jnp.where(rid == am, -jnp.inf, s)
    return out


def _collect16(pairs, tb):
    r16 = lax.broadcasted_iota(i32, (PEER_TOPK, tb), 0)
    v = jnp.zeros((PEER_TOPK, tb), f32)
    ix = jnp.zeros((PEER_TOPK, tb), f32)
    for k, (m, am) in enumerate(pairs):
        v = jnp.where(r16 == k, m, v)
        ix = jnp.where(r16 == k, am, ix)
    return v, ix.astype(i32)


def _peer_route_body(q_ref, keys_ref, ia_ref, ib_ref, gt_ref):
    tb = q_ref.shape[0]
    vals = []
    idxs = []
    for p in range(2):
        st = lax.dot_general(keys_ref[0, p], q_ref[:, p * LANE:(p + 1) * LANE], (((1,), (1,)), ((), ())),
                             preferred_element_type=f32)
        v, ix = _collect16(_top16_rows(st), tb)
        vals.append(v)
        idxs.append(ix)
    va, vb = vals
    half = PEER_TOPK // 2
    cand = jnp.concatenate([va[0:1, :] + vb] + [va[k:k + 1, :] + vb[0:half, :] for k in range(1, half)]
                           + [va[half:, :] + vb[0:1, :]], axis=0)
    r = lax.broadcasted_iota(i32, cand.shape, 0)
    r2 = r - PEER_TOPK
    mid = PEER_TOPK * (1 + (r2 >> 3)) + (r2 & (half - 1))
    pid = jnp.where(r < PEER_TOPK, r, jnp.where(r < PEER_TOPK + half * (half - 1), mid, (r - 8 * half) * PEER_TOPK))
    fs, pos = _collect16(_top16_rows(cand, pid.astype(f32)), tb)
    ka = pos >> 4
    kb = pos & (PEER_TOPK - 1)
    i1 = jnp.zeros((PEER_TOPK, tb), i32)
    i2 = jnp.zeros((PEER_TOPK, tb), i32)
    for j in range(PEER_TOPK):
        i1 = jnp.where(ka == j, idxs[0][j:j + 1, :], i1)
        i2 = jnp.where(kb == j, idxs[1][j:j + 1, :], i2)
    pe = jnp.exp(fs - fs[0:1, :])
    ia_ref[...] = i1
    ib_ref[...] = i2
    gt_ref[...] = pe / jnp.sum(pe, axis=0, keepdims=True)


def _peer_route(q, keys, tb, layer):
    m = q.shape[0]
    spec = pl.BlockSpec((PEER_TOPK, tb), lambda i, hd: (hd, i))
    nslot = PEER_HEADS * PEER_TOPK
    return pl.pallas_call(
        _peer_route_body,
        out_shape=(jax.ShapeDtypeStruct((nslot, m), i32), jax.ShapeDtypeStruct((nslot, m), i32),
                   jax.ShapeDtypeStruct((nslot, m), f32)),
        grid=(m // tb, PEER_HEADS),
        in_specs=[pl.BlockSpec((tb, 2 * LANE), lambda i, hd: (i, hd)),
                  pl.BlockSpec((None, 1, 2, PEER_NKEYS, LANE), lambda i, hd: (layer, hd, 0, 0, 0))],
        out_specs=(spec, spec, spec),
        compiler_params=_cparams(("parallel", "parallel")),
        name="peer_route",
    )(q, keys)


GATE_UNROLL = 32
RELAYOUT_TOKENS = 16


def _peer_gates_body(ia_ref, ib_ref, gt_ref, o_ref, a_s, b_s, g_s, gm_s):
    tg = ia_ref.shape[1]
    a_s[...] = ia_ref[...].astype(f32).T
    b_s[...] = ib_ref[...].astype(f32).T
    g_s[...] = gt_ref[...].T
    sub = lax.broadcasted_iota(i32, (LANE, LANE), 0).astype(f32)

    def step(r, carry):
        r8 = pl.multiple_of(r * GATE_UNROLL, GATE_UNROLL)
        a8 = a_s[pl.ds(r8, GATE_UNROLL), :]
        b8 = b_s[pl.ds(r8, GATE_UNROLL), :]
        g8 = g_s[pl.ds(r8, GATE_UNROLL), :]
        for u in range(GATE_UNROLL):
            at = jnp.where(sub == a8[u:u + 1, :], 1.0, 0.0).astype(bf16)
            bt = jnp.where(sub == b8[u:u + 1, :], g8[u:u + 1, :], 0.0).astype(bf16)
            gm = lax.dot_general(at, bt, (((1,), (1,)), ((), ())), preferred_element_type=f32)
            gm_s[r8 + u] = gm
        return carry

    lax.fori_loop(0, tg // GATE_UNROLL, step, 0)

    def relayout(c, carry):
        t0 = pl.multiple_of(c * RELAYOUT_TOKENS, RELAYOUT_TOKENS)
        for blk in range(PEER_NKEYS // 8):
            x = gm_s[pl.ds(t0, RELAYOUT_TOKENS), pl.ds(blk * 8, 8), :]
            y = jnp.swapaxes(x, 0, 1)
            for i in range(8):
                o_ref[pl.ds(t0, RELAYOUT_TOKENS), pl.ds((blk * 8 + i) * PEER_NKEYS, PEER_NKEYS)] = y[i].astype(bf16)
        return carry

    lax.fori_loop(0, tg // RELAYOUT_TOKENS, relayout, 0)


def _peer_gates(ia, ib, gt, tg):
    nslot, m = ia.shape
    spec = pl.BlockSpec((nslot, tg), lambda i: (0, i))
    return pl.pallas_call(
        _peer_gates_body,
        out_shape=jax.ShapeDtypeStruct((m, PEER_N), bf16),
        grid=(m // tg,),
        in_specs=[spec, spec, spec],
        out_specs=pl.BlockSpec((tg, PEER_N), lambda i: (i, 0)),
        scratch_shapes=[pltpu.VMEM((tg, nslot), f32)] * 3 + [pltpu.VMEM((tg, PEER_NKEYS, PEER_NKEYS), f32)],
        compiler_params=_cparams(("parallel",)),
        name="peer_gates",
    )(ia, ib, gt)


def _peer_experts_body(x_ref, nw_ref, u_ref, v_ref, g_ref, o_ref, xn_ref):
    @pl.when(pl.program_id(1) == 0)
    def _():
        x = x_ref[...]
        xn_ref[...] = _rms(x, nw_ref[...]).astype(bf16)
        o_ref[...] = x

    hmat = lax.dot_general(xn_ref[...], u_ref[...], (((1,), (1,)), ((), ())), preferred_element_type=f32)
    w = (_gelu(hmat) * g_ref[...].astype(f32)).astype(bf16)
    o_ref[...] += jnp.dot(w, v_ref[...], preferred_element_type=f32)


def _peer_experts(x, nw, u, v, g, tb, eb, layer):
    m = x.shape[0]
    return pl.pallas_call(
        _peer_experts_body,
        out_shape=jax.ShapeDtypeStruct((m, D_MODEL), f32),
        grid=(m // tb, PEER_N // eb),
        in_specs=[pl.BlockSpec((tb, D_MODEL), lambda i, j: (i, 0), pipeline_mode=pl.Buffered(1)),
                  pl.BlockSpec((1, D_MODEL), lambda i, j: (0, 0)),
                  pl.BlockSpec((None, eb, D_MODEL), lambda i, j: (layer, j, 0)),
                  pl.BlockSpec((None, eb, D_MODEL), lambda i, j: (layer, j, 0)),
                  pl.BlockSpec((tb, eb), lambda i, j: (i, j))],
        out_specs=pl.BlockSpec((tb, D_MODEL), lambda i, j: (i, 0), pipeline_mode=pl.Buffered(1)),
        scratch_shapes=[pltpu.VMEM((tb, D_MODEL), bf16)],
        compiler_params=_cparams(("parallel", "arbitrary")),
        name="peer_experts",
    )(x, nw, u, v, g)


_IN_OFFS = tuple(int(v) for v in np.cumsum((0,) + IN_SPLITS))
_O_CDT, _O_DQ, _O_DIG, _O_GATES, _O_END = _IN_OFFS[8], _IN_OFFS[9], _IN_OFFS[13], _IN_OFFS[15], _IN_OFFS[16]
PREP_ROWS = 128


def _prep_w_in_body(w_ref, om_ref, og_ref):
    rows = w_ref.shape[0]
    om_ref[:, 0:_O_CDT] = w_ref[:, 0:_O_CDT].astype(bf16)
    om_ref[:, _O_CDT:SMALL_COL] = w_ref[:, _O_DQ:_O_DIG].astype(bf16)
    small = jnp.concatenate([w_ref[:, _O_CDT:_O_DQ], w_ref[:, _O_DIG:_O_GATES],
                             jnp.zeros((rows, LANE - 16), w_ref.dtype)], axis=-1)
    om_ref[:, SMALL_COL:SMALL_COL + LANE] = small.astype(bf16)
    om_ref[:, SMALL_COL + LANE:MIX_W] = jnp.zeros((rows, MIX_W - SMALL_COL - LANE), bf16)
    og_ref[...] = w_ref[:, _O_GATES:_O_END].astype(bf16)


def _prep_w_in(w, layer):
    _, k, n = w.shape
    ng = _O_END - _O_GATES
    return pl.pallas_call(
        _prep_w_in_body,
        out_shape=(jax.ShapeDtypeStruct((k, MIX_W), bf16), jax.ShapeDtypeStruct((k, ng), bf16)),
        grid=(k // PREP_ROWS,),
        in_specs=[pl.BlockSpec((None, PREP_ROWS, n), lambda i: (layer, i, 0))],
        out_specs=(pl.BlockSpec((PREP_ROWS, MIX_W), lambda i: (i, 0)), pl.BlockSpec((PREP_ROWS, ng), lambda i: (i, 0))),
        compiler_params=_cparams(("parallel",)),
        name="prep_w_in",
    )(w)


def _lane_row(vals, start):
    row = jnp.zeros((LANE,), f32)
    return row.at[start:start + vals.shape[0]].set(vals)


def kernel(x_prompt, x_sample, state_hgrn, state_ssm, state_conv, state_mlstm_c, state_mlstm_n, state_mlstm_m, norm1_w, w_in, a_ln_w, a_ln_b, a_ws, a_bs, b_lb_logits, b_norm_w, c_conv_w, c_conv_b, c_dt_bias, c_a_log, c_d, c_norm_w, d_ig_b, d_fg_b, d_norm_w, w_branch, w_out, norm2_w, peer_wq, peer_keys, peer_u, peer_v, final_norm_w):
    x = jnp.concatenate([x_prompt.reshape(N_PROMPT, D_MODEL), x_sample.reshape(N_SAMPLE, D_MODEL)], axis=0)
    lbs = jax.nn.softmax(b_lb_logits.astype(f32), axis=0)
    lbs = jnp.cumsum(lbs, axis=0) - lbs[0]
    zeros8 = jnp.zeros((8, LANE), f32)
    news_p = []
    news_s = []
    stacked_s = None
    w_in_b, w_branch_b, w_out_b, wq_b = (w.astype(bf16) for w in (w_in, w_branch, w_out, peer_wq))
    keys_b, u_b, v_b = (w.astype(bf16) for w in (peer_keys, peer_u, peer_v))
    for l in range(DEPTH):
        w_mix, w_gate = _prep_w_in(w_in_b, l)
        h = _norm_mm(x, norm1_w[l][None, :], w_mix, f32, 1088, 1024)

        lnw = a_ln_w[l][None, :]
        lnb = a_ln_b[l][None, :]
        lb = lbs[l][None, :]
        bnw = b_norm_w[l][None, :]
        cw = c_conv_w[l]
        cb = c_conv_b[l][None, :]
        cpar = zeros8.at[0].set(_lane_row(c_dt_bias[l], 0)).at[1].set(_lane_row(c_a_log[l], 0)).at[2].set(_lane_row(c_d[l], 0))
        cnw = c_norm_w[l][None, :]
        dpar = zeros8.at[0].set(_lane_row(d_ig_b[l], 8) + _lane_row(d_fg_b[l], 12))
        dnw = d_norm_w[l][None, :]

        bs_full = jnp.repeat(a_bs[l].T, LANE, axis=1)
        br_a = _prompt_a(h, lnw, lnb, a_ws[l], bs_full)
        br_b, hg_p = _prompt_b(h, lb, bnw)
        br_c, ssm_p, cv_p = _prompt_c(h, cw, cb, cpar, cnw)
        br_d, mc_p, mn_p, mm_p = _prompt_d(h, dpar, dnw)
        news_p.append((hg_p, ssm_p, cv_p, mc_p, mn_p[:, :D_HEADS, :], mm_p[:, :D_HEADS, 0]))

        w4 = jnp.tril(a_ws[l][:, :DEC_SEQ, :DEC_SEQ])
        w8 = jnp.zeros((A_GROUPS, 8, 8), f32).at[:, :4, :4].set(w4).at[:, 4:, 4:].set(w4)
        ws8 = jnp.repeat(jnp.transpose(w8, (2, 1, 0)), LANE, axis=2)
        bs8 = jnp.repeat(jnp.tile(a_bs[l][:, :DEC_SEQ], (1, 2)).T, LANE, axis=1)
        outs = _sample_mixer(h, state_hgrn, state_ssm, state_conv, state_mlstm_c, state_mlstm_n, state_mlstm_m,
                             (lnw, lnb, ws8, bs8), (lb, bnw), (cw, cb, cpar, cnw), (dpar, dnw), l,
                             (br_a, br_b, br_c, br_d), stacked_s)
        brs = outs[:4]
        stacked_s = (outs[5], outs[6], outs[8])
        news_s.append((outs[7], outs[9], outs[10], outs[4].reshape(DEC_BATCH, DEC_SEQ, BW)))

        mixin = _merge(x, norm1_w[l][None, :], brs, w_gate, w_branch_b, 1088, 1024, l)
        x = _mm_res(mixin, w_out_b, x, 1088, 512, l)

        q = _norm_mm(x, norm2_w[l][None, :], wq_b, bf16, 1088, 1024, l)
        ia, ib, gt = _peer_route(q, keys_b, 512, l)
        g = _peer_gates(ia, ib, gt, 128)
        x = _peer_experts(x, norm2_w[l][None, :], u_b, v_b, g, 1088, 1024, l)

    y_p, y_s = _final_norm(x, final_norm_w[None, :])
    y_prompt = y_p.reshape(BATCH, SEQ, D_MODEL)
    y_sample = y_s.reshape(DEC_BATCH, DEC_SEQ, D_MODEL)
    stack = lambda news, k: jnp.stack([n[k] for n in news], axis=0)
    hgrn_s, ssm_s, mc_s = stacked_s
    conv_s, mn_s, mm_s, chunk_v_s = (stack(news_s, k) for k in range(4))
    return ((y_prompt, y_sample) + tuple(stack(news_p, k) for k in range(6))
            + (hgrn_s, ssm_s, conv_s, mc_s, mn_s, mm_s, chunk_v_s))
```

```python
import functools

import jax
import jax.numpy as jnp
import numpy as np
from jax import lax
from jax.experimental import pallas as pl
from jax.experimental.pallas import tpu as pltpu

f32 = jnp.float32
bf16 = jnp.bfloat16
i32 = jnp.int32
HI = lax.Precision.HIGHEST

D_MODEL = 2048
BATCH = 4
SEQ = 2048
DEPTH = 2
DEC_BATCH = 128
DEC_SEQ = 4
N_BRANCH = 4
BW = 512
A_GROUPS = 4
A_CHUNK = 128
B_HEADS = 4
LB_FLOOR = 1e-30
C_HEADDIM = 64
C_HEADS = 8
C_GROUPS = 2
C_DSTATE = 128
C_CONV = 4
C_CONV_DIM = 1024
D_HEADS = 4
D_DK = 128
NEG_BIG = -1e30
IN_SPLITS = (512, 512, 512, 512, 512, 512, 512, 1024, 8, 512, 512, 512, 512, 4, 4, 8192)
PEER_NKEYS = 128
PEER_N = PEER_NKEYS * PEER_NKEYS
PEER_HEADS = 8
PEER_TOPK = 16
EPS = 1e-6
INV_SQRT2 = 0.7071067811865476

N_PROMPT = BATCH * SEQ
N_SAMPLE = DEC_BATCH * DEC_SEQ
N_TOK = N_PROMPT + N_SAMPLE

CB_AU, CB_AV, CB_BQ, CB_BF, CB_BI, CB_BG, CB_CZ, CB_CX0, CB_CX1, CB_DQ, CB_DK, CB_DV, CB_DO = range(13)
SMALL_COL = 13 * 512
MIX_W = 14 * 512
LANE = 128
CHUNK = 128
NCHUNK = SEQ // CHUNK
SUB = 16
SB = 8
VMEM_LIMIT = 56 * 1024 * 1024


def _gelu(x):
    return 0.5 * x * (1.0 + lax.erf(x * INV_SQRT2))


def _rms(x, w):
    ms = jnp.mean(x * x, axis=-1, keepdims=True)
    return x * lax.rsqrt(ms + EPS) * w


def _tri(n):
    r = lax.broadcasted_iota(i32, (n, n), 0)
    c = lax.broadcasted_iota(i32, (n, n), 1)
    return r >= c


def _cumsum_rows_small(x, n):
    row = lax.broadcasted_iota(i32, (n, 1), 0)
    acc = jnp.zeros_like(x)
    for s in range(n):
        acc = acc + jnp.where(row >= s, x[s:s + 1, :], 0.0)
    return acc


def _row_to_col(row, eye):
    return jnp.sum(jnp.where(eye, row, 0.0), axis=-1, keepdims=True)


def _cparams(sem, vmem=VMEM_LIMIT):
    return pltpu.CompilerParams(dimension_semantics=sem, vmem_limit_bytes=vmem)


def _norm_mm_body(x_ref, nw_ref, w_ref, o_ref, xn_ref):
    @pl.when(pl.program_id(1) == 0)
    def _():
        xn_ref[...] = _rms(x_ref[...], nw_ref[...]).astype(bf16)

    o_ref[...] = jnp.dot(xn_ref[...], w_ref[...], preferred_element_type=f32).astype(o_ref.dtype)


def _w_spec(w, layer, block, index_map):
    if layer is None:
        return pl.BlockSpec(block, index_map)
    return pl.BlockSpec((None,) + block, lambda *g: (layer,) + index_map(*g))


def _norm_mm(x, nw, w, out_dtype, tm, tn, layer=None):
    m, k = x.shape
    n = w.shape[-1]
    return pl.pallas_call(
        _norm_mm_body,
        out_shape=jax.ShapeDtypeStruct((m, n), out_dtype),
        grid=(m // tm, n // tn),
        in_specs=[pl.BlockSpec((tm, k), lambda i, j: (i, 0), pipeline_mode=pl.Buffered(1)),
                  pl.BlockSpec((1, k), lambda i, j: (0, 0)),
                  _w_spec(w, layer, (k, tn), lambda i, j: (0, j))],
        out_specs=pl.BlockSpec((tm, tn), lambda i, j: (i, j)),
        scratch_shapes=[pltpu.VMEM((tm, k), bf16)],
        compiler_params=_cparams(("parallel", "arbitrary")),
        name="norm_mm",
    )(x, nw, w)


def _mm_res_body(a_ref, w_ref, r_ref, o_ref):
    o_ref[...] = r_ref[...] + jnp.dot(a_ref[...], w_ref[...], preferred_element_type=f32)


def _mm_res(a, w, res, tm, tn, layer=None):
    m, k = a.shape
    n = w.shape[-1]
    return pl.pallas_call(
        _mm_res_body,
        out_shape=jax.ShapeDtypeStruct((m, n), f32),
        grid=(m // tm, n // tn),
        in_specs=[pl.BlockSpec((tm, k), lambda i, j: (i, 0)),
                  _w_spec(w, layer, (k, tn), lambda i, j: (0, j)),
                  pl.BlockSpec((tm, tn), lambda i, j: (i, j))],
        out_specs=pl.BlockSpec((tm, tn), lambda i, j: (i, j)),
        compiler_params=_cparams(("parallel", "arbitrary")),
        name="mm_res",
    )(a, w, res)


def _final_norm_body(x_ref, w_ref, op_ref, os_ref):
    y = _rms(x_ref[...], w_ref[...])
    i = pl.program_id(0)

    @pl.when(i < N_PROMPT // N_SAMPLE)
    def _():
        op_ref[...] = y

    @pl.when(i == N_PROMPT // N_SAMPLE)
    def _():
        os_ref[...] = y


def _final_norm(x, w):
    m, k = x.shape
    tm = N_SAMPLE
    last_p = N_PROMPT // tm - 1
    return pl.pallas_call(
        _final_norm_body,
        out_shape=(jax.ShapeDtypeStruct((N_PROMPT, k), f32), jax.ShapeDtypeStruct((N_SAMPLE, k), f32)),
        grid=(m // tm,),
        in_specs=[pl.BlockSpec((tm, k), lambda i: (i, 0)), pl.BlockSpec((1, k), lambda i: (0, 0))],
        out_specs=(pl.BlockSpec((tm, k), lambda i: (jnp.minimum(i, last_p), 0)),
                   pl.BlockSpec((tm, k), lambda i: (0, 0))),
        compiler_params=_cparams(("arbitrary",)),
        name="final_norm",
    )(x, w)


def _a_uv(au, av, lnw, lnb):
    u = _gelu(au)
    g = _gelu(av)
    xc = g - jnp.mean(g, axis=-1, keepdims=True)
    var = jnp.mean(xc * xc, axis=-1, keepdims=True)
    v = xc * lax.rsqrt(var + EPS) * lnw + lnb
    return u, v


def _b_pre(bq, bf_, lb):
    q = bq * jax.nn.sigmoid(bq)
    logf = jnp.logaddexp(jnp.log(jnp.maximum(lb, LB_FLOOR)), jnp.log1p(-lb) + jax.nn.log_sigmoid(bf_))
    kb = (1.0 - lb) * jax.nn.sigmoid(-bf_)
    return q, kb, logf


def _hgrn_chunks(chains, n, eye):
    row = lax.broadcasted_iota(i32, (n, 1), 0)
    qs, ks, vs, gs, ss = [], [], [], [], []
    for q, k, v, gl, s_mat, valid in chains:
        if valid is not None:
            gl = jnp.where(valid, gl, 0.0)
            k = jnp.where(valid, k, 0.0)
        qs.append(q)
        ks.append(k)
        vs.append(v)
        gs.append(_cumsum_rows_small(gl, n))
        ss.append(s_mat)
    nc = len(chains)
    os_ = [jnp.dot((qs[c] * jnp.exp(gs[c])).astype(bf16), ss[c].astype(bf16), preferred_element_type=f32)
           for c in range(nc)]
    upds = []
    for c in range(nc):
        k_dec = ks[c] * jnp.exp(gs[c][n - 1:n, :] - gs[c])
        upds.append(lax.dot_general(k_dec.astype(bf16), vs[c].astype(bf16), (((0,), (0,)), ((), ())),
                                    preferred_element_type=f32))
    for s in range(n):
        m = row >= s
        for c in range(nc):
            g = gs[c]
            p = jnp.where(m, qs[c] * ks[c][s:s + 1, :] * jnp.exp(g - g[s:s + 1, :]), 0.0)
            os_[c] = os_[c] + jnp.sum(p, axis=-1, keepdims=True) * vs[c][s:s + 1, :]
    out = []
    for c in range(nc):
        s_new = jnp.exp(_row_to_col(gs[c][n - 1:n, :], eye)) * ss[c] + upds[c]
        out.append((os_[c], s_new))
    return out


def _b_post(o, bg, nw):
    return _rms(o, nw) * (bg * jax.nn.sigmoid(bg))


def _hrow(b, c):
    return b * NCHUNK + c


def _h_spec(colblk):
    return pl.BlockSpec((CHUNK, BW), lambda b, c, cb=colblk: (_hrow(b, c), cb))


_SMALL_SPEC = pl.BlockSpec((CHUNK, LANE), lambda b, c: (_hrow(b, c), SMALL_COL // LANE))
_BR_SPEC = pl.BlockSpec((CHUNK, BW), lambda b, c: (_hrow(b, c), 0))


def _full_spec(shape):
    nd = len(shape)
    return pl.BlockSpec(shape, lambda b, c, nd=nd: (0,) * nd)


def _pa_body(au_ref, av_ref, lnw_ref, lnb_ref, ws_ref, bs_ref, o_ref):
    u, v = _a_uv(au_ref[...], av_ref[...], lnw_ref[...], lnb_ref[...])
    tri = _tri(CHUNK)
    vb = v.astype(bf16)
    parts = []
    for g in range(A_GROUPS):
        w = jnp.where(tri, ws_ref[g], 0.0).astype(bf16)
        parts.append(jnp.dot(w, vb[:, g * LANE:(g + 1) * LANE], preferred_element_type=f32))
    sp = jnp.concatenate(parts, axis=-1) + bs_ref[...]
    o_ref[...] = (u * sp).astype(bf16)


def _prompt_a(h, lnw, lnb, ws, bs_full):
    return pl.pallas_call(
        _pa_body,
        out_shape=jax.ShapeDtypeStruct((N_TOK, BW), bf16),
        grid=(BATCH, NCHUNK),
        in_specs=[_h_spec(CB_AU), _h_spec(CB_AV), _full_spec((1, BW)), _full_spec((1, BW)),
                  _full_spec((A_GROUPS, A_CHUNK, A_CHUNK)), _full_spec((A_CHUNK, BW))],
        out_specs=_BR_SPEC,
        compiler_params=_cparams(("parallel", "parallel")),
        name="prompt_gmlp",
    )(h, h, lnw, lnb, ws, bs_full)


def _pb_body(bq_ref, bf_ref, bi_ref, bg_ref, lb_ref, nw_ref, o_ref, st_ref, s_ref, q_s, k_s, v_s, g_s, o_s):
    c = pl.program_id(1)

    @pl.when(c == 0)
    def _():
        s_ref[...] = jnp.zeros_like(s_ref)

    q, kb, logf = _b_pre(bq_ref[...], bf_ref[...], lb_ref[...])
    q_s[...] = q
    k_s[...] = kb
    v_s[...] = bi_ref[...]
    g_s[...] = logf
    eye = lax.broadcasted_iota(i32, (LANE, LANE), 0) == lax.broadcasted_iota(i32, (LANE, LANE), 1)
    def sub(j, carry):
        r = pl.multiple_of(j * SUB, SUB)
        heads = [slice(hd * LANE, (hd + 1) * LANE) for hd in range(B_HEADS)]
        chains = [(q_s[pl.ds(r, SUB), hs], k_s[pl.ds(r, SUB), hs], v_s[pl.ds(r, SUB), hs], g_s[pl.ds(r, SUB), hs],
                   s_ref[hd], None) for hd, hs in enumerate(heads)]
        for hd, (o, s_new) in enumerate(_hgrn_chunks(chains, SUB, eye)):
            o_s[pl.ds(r, SUB), heads[hd]] = o
            s_ref[hd] = s_new
        return carry

    lax.fori_loop(0, CHUNK // SUB, sub, 0, unroll=2)
    bg = bg_ref[...]
    nw = nw_ref[...]
    for hd in range(B_HEADS):
        hs = slice(hd * LANE, (hd + 1) * LANE)
        o_ref[:, hs] = _b_post(o_s[:, hs], bg[:, hs], nw).astype(bf16)

    @pl.when(c == NCHUNK - 1)
    def _():
        st_ref[0] = s_ref[...]


def _prompt_b(h, lb, nw):
    return pl.pallas_call(
        _pb_body,
        out_shape=(jax.ShapeDtypeStruct((N_TOK, BW), bf16),
                   jax.ShapeDtypeStruct((BATCH, B_HEADS, LANE, LANE), f32)),
        grid=(BATCH, NCHUNK),
        in_specs=[_h_spec(CB_BQ), _h_spec(CB_BF), _h_spec(CB_BI), _h_spec(CB_BG),
                  _full_spec((1, BW)), _full_spec((1, LANE))],
        out_specs=(_BR_SPEC, pl.BlockSpec((1, B_HEADS, LANE, LANE), lambda b, c: (b, 0, 0, 0))),
        scratch_shapes=[pltpu.VMEM((B_HEADS, LANE, LANE), f32)] + [pltpu.VMEM((CHUNK, BW), f32)] * 5,
        compiler_params=_cparams(("parallel", "arbitrary")),
        name="prompt_hgrn",
    )(h, h, h, h, lb, nw)


def _c_conv_silu(win0, win1, win2, win3, cw_ref, cb_ref):
    y = cb_ref[...] + win0 * cw_ref[0:1, :] + win1 * cw_ref[1:2, :] + win2 * cw_ref[2:3, :] + win3 * cw_ref[3:4, :]
    return y * jax.nn.sigmoid(y)


def _c_post(yc, cz, nw):
    y = yc * (cz * jax.nn.sigmoid(cz))
    gw = BW // C_GROUPS
    parts = [_rms(y[:, g * gw:(g + 1) * gw], nw[:, g * gw:(g + 1) * gw]) for g in range(C_GROUPS)]
    return jnp.concatenate(parts, axis=-1)


def _pc_body(cz_ref, cx0_ref, cx1_ref, sm_ref, cw_ref, cb_ref, par_ref, nw_ref,
             o_ref, st_ref, cv_ref, xpad, sp_ref):
    c = pl.program_id(1)

    @pl.when(c == 0)
    def _():
        xpad[pl.ds(0, 8), :] = jnp.zeros((8, C_CONV_DIM), f32)
        sp_ref[...] = jnp.zeros_like(sp_ref)

    xpad[pl.ds(8, CHUNK), 0:BW] = cx0_ref[...]
    xpad[pl.ds(8, CHUNK), BW:2 * BW] = cx1_ref[...]
    xbc = _c_conv_silu(xpad[pl.ds(5, CHUNK), :], xpad[pl.ds(6, CHUNK), :], xpad[pl.ds(7, CHUNK), :],
                       xpad[pl.ds(8, CHUNK), :], cw_ref, cb_ref)

    @pl.when(c == NCHUNK - 1)
    def _():
        cv_ref[0] = xpad[pl.ds(CHUNK + 5, 3), :]

    xpad[pl.ds(0, 8), :] = xpad[pl.ds(CHUNK, 8), :]

    xs = xbc[:, 0:BW]
    bm = xbc[:, BW:BW + 2 * C_DSTATE]
    cm = xbc[:, BW + 2 * C_DSTATE:]
    par = par_ref[...]
    dt = jax.nn.softplus(sm_ref[...] + par[0:1, :])
    gl = dt * (-jnp.exp(par[1:2, :]))
    tri = _tri(CHUNK)
    g = jnp.dot(tri.astype(f32), gl, precision=HI, preferred_element_type=f32)
    gt = g.T
    dtt = dt.T
    lane = lax.broadcasted_iota(i32, (1, LANE), 1)
    lo = lane < C_HEADDIM
    cbs = []
    for grp in range(C_GROUPS):
        cg = cm[:, grp * C_DSTATE:(grp + 1) * C_DSTATE].astype(bf16)
        bg = bm[:, grp * C_DSTATE:(grp + 1) * C_DSTATE].astype(bf16)
        cbs.append(lax.dot_general(cg, bg, (((1,), (1,)), ((), ())), preferred_element_type=f32))
    ys = []
    for pr in range(C_HEADS // 2):
        xp = xs[:, pr * LANE:(pr + 1) * LANE]
        sp = sp_ref[pr]
        y = jnp.zeros((CHUNK, LANE), f32)
        upd = jnp.zeros((C_DSTATE, LANE), f32)
        dl = jnp.zeros((1, LANE), f32)
        cdl = jnp.zeros((1, LANE), f32)
        for sub in range(2):
            hd = 2 * pr + sub
            grp = hd // (C_HEADS // C_GROUPS)
            lm = lo if sub == 0 else jnp.logical_not(lo)
            col = g[:, hd:hd + 1]
            g_last = col[CHUNK - 1:CHUNK, :]
            dec = jnp.exp(col - gt[hd:hd + 1, :])
            sc = jnp.where(tri, cbs[grp] * dec * dtt[hd:hd + 1, :], 0.0)
            xm = jnp.where(lm, xp, 0.0).astype(bf16)
            cg = cm[:, grp * C_DSTATE:(grp + 1) * C_DSTATE]
            bg = bm[:, grp * C_DSTATE:(grp + 1) * C_DSTATE]
            y = y + jnp.dot(sc.astype(bf16), xm, preferred_element_type=f32)
            y = y + jnp.dot((cg * jnp.exp(col)).astype(bf16), jnp.where(lm, sp, 0.0).astype(bf16),
                            preferred_element_type=f32)
            kd = bg * (dt[:, hd:hd + 1] * jnp.exp(g_last - col))
            upd = upd + lax.dot_general(kd.astype(bf16), xm, (((0,), (0,)), ((), ())), preferred_element_type=f32)
            dl = jnp.where(lm, jnp.exp(g_last), dl)
            cdl = jnp.where(lm, par[2:3, hd:hd + 1], cdl)
        sp_ref[pr] = dl * sp + upd
        ys.append(y + cdl * xp)
    yc = jnp.concatenate(ys, axis=-1)
    o_ref[...] = _c_post(yc, cz_ref[...], nw_ref[...]).astype(bf16)

    @pl.when(c == NCHUNK - 1)
    def _():
        for pr in range(C_HEADS // 2):
            st_ref[0, 2 * pr] = sp_ref[pr][:, 0:C_HEADDIM]
            st_ref[0, 2 * pr + 1] = sp_ref[pr][:, C_HEADDIM:]


def _prompt_c(h, cw, cb, par, nw):
    return pl.pallas_call(
        _pc_body,
        out_shape=(jax.ShapeDtypeStruct((N_TOK, BW), bf16),
                   jax.ShapeDtypeStruct((BATCH, C_HEADS, C_DSTATE, C_HEADDIM), f32),
                   jax.ShapeDtypeStruct((BATCH, C_CONV - 1, C_CONV_DIM), f32)),
        grid=(BATCH, NCHUNK),
        in_specs=[_h_spec(CB_CZ), _h_spec(CB_CX0), _h_spec(CB_CX1), _SMALL_SPEC,
                  _full_spec((C_CONV, C_CONV_DIM)), _full_spec((1, C_CONV_DIM)), _full_spec((8, LANE)),
                  _full_spec((1, BW))],
        out_specs=(_BR_SPEC,
                   pl.BlockSpec((1, C_HEADS, C_DSTATE, C_HEADDIM), lambda b, c: (b, 0, 0, 0)),
                   pl.BlockSpec((1, C_CONV - 1, C_CONV_DIM), lambda b, c: (b, 0, 0))),
        scratch_shapes=[pltpu.VMEM((CHUNK + 8, C_CONV_DIM), f32), pltpu.VMEM((C_HEADS // 2, C_DSTATE, LANE), f32)],
        compiler_params=_cparams(("parallel", "arbitrary")),
        name="prompt_ssd",
    )(h, h, h, h, cw, cb, par, nw)


def _pd_body(dq_ref, dk_ref, dv_ref, do_ref, sm_ref, par_ref, nw_ref,
             o_ref, c_out, n_out, m_out, c_ref, n_ref, m_ref):
    c = pl.program_id(1)

    @pl.when(c == 0)
    def _():
        c_ref[...] = jnp.zeros_like(c_ref)
        n_ref[...] = jnp.zeros_like(n_ref)
        m_ref[...] = jnp.zeros_like(m_ref)

    sm = sm_ref[...] + par_ref[0:1, :]
    ls = jax.nn.log_sigmoid(sm)
    tri = _tri(CHUNK)
    bc = jnp.dot(tri.astype(f32), ls, precision=HI, preferred_element_type=f32)
    bct = bc.T
    smt = sm.T
    nw = nw_ref[...]
    for hd in range(D_HEADS):
        hs = slice(hd * LANE, (hd + 1) * LANE)
        li, lf = 8 + hd, 12 + hd
        bcol = bc[:, lf:lf + 1]
        brow = bct[lf:lf + 1, :]
        irow = smt[li:li + 1, :]
        icol = sm[:, li:li + 1]
        mprev = m_ref[hd:hd + 1, 0:1]
        dmat = jnp.where(tri, bcol - brow + irow, NEG_BIG)
        inter = bcol + mprev
        mt = jnp.maximum(inter, jnp.max(dmat, axis=-1, keepdims=True))
        w_intra = jnp.where(tri, jnp.exp(dmat - mt), 0.0)
        w_inter = jnp.exp(inter - mt)
        qh = dq_ref[:, hs] * (D_DK ** -0.5)
        kh = dk_ref[:, hs]
        vh = dv_ref[:, hs]
        qb = qh.astype(bf16)
        qk = lax.dot_general(qb, kh.astype(bf16), (((1,), (1,)), ((), ())), preferred_element_type=f32) * w_intra
        num = w_inter * jnp.dot(qb, c_ref[hd].astype(bf16), preferred_element_type=f32)
        num = num + jnp.dot(qk.astype(bf16), vh.astype(bf16), preferred_element_type=f32)
        den = w_inter * jnp.sum(qh * n_ref[hd:hd + 1, :], axis=-1, keepdims=True) + jnp.sum(qk, axis=-1, keepdims=True)
        hh = num / jnp.maximum(jnp.abs(den), jnp.exp(-mt))
        mnew = mt[CHUNK - 1:CHUNK, :]
        blast = bcol[CHUNK - 1:CHUNK, :]
        wk = jnp.exp(blast - bcol + icol - mnew)
        decay = jnp.exp(blast + mprev - mnew)
        wkk = wk * kh
        c_ref[hd] = decay * c_ref[hd] + lax.dot_general(wkk.astype(bf16), vh.astype(bf16), (((0,), (0,)), ((), ())),
                                                        preferred_element_type=f32)
        n_ref[hd:hd + 1, :] = decay * n_ref[hd:hd + 1, :] + jnp.sum(wkk, axis=0, keepdims=True)
        m_ref[hd:hd + 1, :] = jnp.broadcast_to(mnew, (1, LANE))
        o_ref[:, hs] = (jax.nn.sigmoid(do_ref[:, hs]) * _rms(hh, nw[:, hs])).astype(bf16)

    @pl.when(c == NCHUNK - 1)
    def _():
        c_out[0] = c_ref[...]
        n_out[0] = n_ref[...]
        m_out[0] = m_ref[...]


def _prompt_d(h, par, nw):
    return pl.pallas_call(
        _pd_body,
        out_shape=(jax.ShapeDtypeStruct((N_TOK, BW), bf16),
                   jax.ShapeDtypeStruct((BATCH, D_HEADS, LANE, LANE), f32),
                   jax.ShapeDtypeStruct((BATCH, 8, LANE), f32),
                   jax.ShapeDtypeStruct((BATCH, 8, LANE), f32)),
        grid=(BATCH, NCHUNK),
        in_specs=[_h_spec(CB_DQ), _h_spec(CB_DK), _h_spec(CB_DV), _h_spec(CB_DO), _SMALL_SPEC,
                  _full_spec((8, LANE)), _full_spec((1, BW))],
        out_specs=(_BR_SPEC,
                   pl.BlockSpec((1, D_HEADS, LANE, LANE), lambda b, c: (b, 0, 0, 0)),
                   pl.BlockSpec((1, 8, LANE), lambda b, c: (b, 0, 0)),
                   pl.BlockSpec((1, 8, LANE), lambda b, c: (b, 0, 0))),
        scratch_shapes=[pltpu.VMEM((D_HEADS, LANE, LANE), f32), pltpu.VMEM((8, LANE), f32), pltpu.VMEM((8, LANE), f32)],
        compiler_params=_cparams(("parallel", "arbitrary")),
        name="prompt_mlstm",
    )(h, h, h, h, h, par, nw)


def _sample_body(h_ref, hg_ref, ssm_ref, cv_ref, mc_ref, mn_ref, mm_ref,
                 lnw_ref, lnb_ref, ws8_ref, bs8_ref, lb_ref, bnw_ref,
                 cw_ref, cb_ref, cpar_ref, cnw_ref, dpar_ref, dnw_ref,
                 oa_ref, ob_ref, oc_ref, od_ref, chv_ref,
                 hg_out, ssm_out, cv_out, mc_out, mn_out, mm_out):
    row = lax.broadcasted_iota(i32, (8, 1), 0)
    first = row < DEC_SEQ
    tpos = row & (DEC_SEQ - 1)
    seg = [((row >= s) & first) if s < DEC_SEQ else (row >= s) for s in range(8)]

    def cumsum_seg(x):
        acc = jnp.zeros_like(x)
        for s in range(8):
            acc = acc + jnp.where(seg[s], x[s:s + 1, :], 0.0)
        return acc

    def both(x):
        return jnp.concatenate([jnp.where(first, x, 0.0), jnp.where(first, 0.0, x)], axis=-1)

    def last_rows(x):
        return jnp.where(first, x[DEC_SEQ - 1:DEC_SEQ, :], x[7:8, :])
    eye = lax.broadcasted_iota(i32, (LANE, LANE), 0) == lax.broadcasted_iota(i32, (LANE, LANE), 1)
    lane = lax.broadcasted_iota(i32, (1, LANE), 1)
    lo = lane < C_HEADDIM

    def tile(p, carry):
        r = pl.multiple_of(p * 8, 8)

        def col(blk, width=BW):
            return h_ref[pl.ds(r, 8), blk * BW:blk * BW + width]

        small = h_ref[pl.ds(r, 8), SMALL_COL:SMALL_COL + LANE]

        u, v = _a_uv(col(CB_AU), col(CB_AV), lnw_ref[...], lnb_ref[...])
        chv_ref[pl.ds(r, 8), :] = v
        sp = bs8_ref[...]
        for s in range(8):
            sp = sp + ws8_ref[s] * v[s:s + 1, :]
        oa_ref[pl.ds(r, 8), :] = (u * sp).astype(bf16)

        q, kb, logf = _b_pre(col(CB_BQ), col(CB_BF), lb_ref[...])
        bi = col(CB_BI)
        bg = col(CB_BG)
        for hd in range(B_HEADS):
            hs = slice(hd * LANE, (hd + 1) * LANE)
            qh, kh, vh = q[:, hs], kb[:, hs], bi[:, hs]
            g = cumsum_seg(logf[:, hs])
            s_a = hg_ref[2 * p, hd]
            s_b = hg_ref[2 * p + 1, hd]
            o = jnp.dot(both(qh * jnp.exp(g)).astype(bf16), jnp.concatenate([s_a, s_b], axis=0).astype(bf16),
                        preferred_element_type=f32)
            for s in range(8):
                m = seg[s]
                pp = jnp.where(m, qh * kh[s:s + 1, :] * jnp.exp(g - g[s:s + 1, :]), 0.0)
                o = o + jnp.sum(pp, axis=-1, keepdims=True) * vh[s:s + 1, :]
            k_dec = kh * jnp.exp(last_rows(g) - g)
            upd = lax.dot_general(k_dec.astype(bf16), both(vh).astype(bf16), (((0,), (0,)), ((), ())),
                                  preferred_element_type=f32)
            hg_out[2 * p, hd] = jnp.exp(_row_to_col(g[DEC_SEQ - 1:DEC_SEQ, :], eye)) * s_a + upd[:, 0:LANE]
            hg_out[2 * p + 1, hd] = jnp.exp(_row_to_col(g[7:8, :], eye)) * s_b + upd[:, LANE:]
            ob_ref[pl.ds(r, 8), hs] = _b_post(o, bg[:, hs], bnw_ref[...]).astype(bf16)

        x = h_ref[pl.ds(r, 8), CB_CX0 * BW:CB_CX0 * BW + C_CONV_DIM]
        bufs = [jnp.where(first, cv_ref[2 * p, k:k + 1, :], cv_ref[2 * p + 1, k:k + 1, :]) for k in range(3)]
        r1 = pltpu.roll(x, 1, 0)
        r2 = pltpu.roll(x, 2, 0)
        r3 = pltpu.roll(x, 3, 0)
        sh1 = jnp.where(tpos >= 1, r1, bufs[2])
        sh2 = jnp.where(tpos >= 2, r2, jnp.where(tpos == 0, bufs[1], bufs[2]))
        sh3 = jnp.where(tpos >= 3, r3, jnp.where(tpos == 0, bufs[0], jnp.where(tpos == 1, bufs[1], bufs[2])))
        xbc = _c_conv_silu(sh3, sh2, sh1, x, cw_ref, cb_ref)
        cv_out[2 * p] = pltpu.roll(x, 7, 0)[0:3, :]
        cv_out[2 * p + 1] = r3[0:3, :]
        xs = xbc[:, 0:BW]
        bm = xbc[:, BW:BW + 2 * C_DSTATE]
        cm = xbc[:, BW + 2 * C_DSTATE:]
        cpar = cpar_ref[...]
        dt = jax.nn.softplus(small + cpar[0:1, :])
        gl_all = dt * (-jnp.exp(cpar[1:2, :]))
        g = cumsum_seg(gl_all)
        g_lastr = last_rows(g)
        dots = []
        for grp in range(C_GROUPS):
            cg = cm[:, grp * C_DSTATE:(grp + 1) * C_DSTATE]
            bgp = bm[:, grp * C_DSTATE:(grp + 1) * C_DSTATE]
            dots.append([jnp.sum(cg * bgp[s:s + 1, :], axis=-1, keepdims=True) for s in range(8)])
        ys = []
        for pr in range(C_HEADS // 2):
            xp = xs[:, pr * LANE:(pr + 1) * LANE]
            grp = (2 * pr) // (C_HEADS // C_GROUPS)
            cg = cm[:, grp * C_DSTATE:(grp + 1) * C_DSTATE]
            bgp = bm[:, grp * C_DSTATE:(grp + 1) * C_DSTATE]
            spairs = [jnp.concatenate([ssm_ref[2 * p + w, 2 * pr], ssm_ref[2 * p + w, 2 * pr + 1]], axis=-1)
                      for w in range(2)]
            y = jnp.zeros((8, LANE), f32)
            ces, kds, xms = [], [], []
            for sub in range(2):
                hd = 2 * pr + sub
                lm = lo if sub == 0 else jnp.logical_not(lo)
                gcol = g[:, hd:hd + 1]
                xm = jnp.where(lm, xp, 0.0)
                for s in range(8):
                    coef = dots[grp][s] * jnp.where(seg[s], jnp.exp(gcol - gcol[s:s + 1, :]) * dt[s:s + 1, hd:hd + 1], 0.0)
                    y = y + coef * xm[s:s + 1, :]
                ces.append(cg * jnp.exp(gcol))
                kds.append(bgp * (dt[:, hd:hd + 1] * jnp.exp(g_lastr[:, hd:hd + 1] - gcol)))
                xms.append(xm)
            lhs = jnp.concatenate([jnp.where(first, ces[0], 0.0), jnp.where(first, ces[1], 0.0),
                                   jnp.where(first, 0.0, ces[0]), jnp.where(first, 0.0, ces[1])], axis=-1)
            rhs = jnp.concatenate([jnp.where(lo, spairs[0], 0.0), jnp.where(lo, 0.0, spairs[0]),
                                   jnp.where(lo, spairs[1], 0.0), jnp.where(lo, 0.0, spairs[1])], axis=0)
            y = y + jnp.dot(lhs.astype(bf16), rhs.astype(bf16), preferred_element_type=f32)
            kd16 = jnp.concatenate(kds, axis=0)
            xm16 = jnp.concatenate([both(xms[0]), both(xms[1])], axis=0)
            upd = lax.dot_general(kd16.astype(bf16), xm16.astype(bf16), (((0,), (0,)), ((), ())),
                                  preferred_element_type=f32)
            for w in range(2):
                last = DEC_SEQ - 1 if w == 0 else 7
                dl = jnp.where(lo, jnp.exp(g[last:last + 1, 2 * pr:2 * pr + 1]),
                               jnp.exp(g[last:last + 1, 2 * pr + 1:2 * pr + 2]))
                snew = dl * spairs[w] + upd[:, w * LANE:(w + 1) * LANE]
                ssm_out[2 * p + w, 2 * pr] = snew[:, 0:C_HEADDIM]
                ssm_out[2 * p + w, 2 * pr + 1] = snew[:, C_HEADDIM:]
            cdl = jnp.where(lo, cpar[2:3, 2 * pr:2 * pr + 1], cpar[2:3, 2 * pr + 1:2 * pr + 2])
            ys.append(y + cdl * xp)
        yc = jnp.concatenate(ys, axis=-1)
        oc_ref[pl.ds(r, 8), :] = _c_post(yc, col(CB_CZ), cnw_ref[...]).astype(bf16)

        smd = small + dpar_ref[0:1, :]
        lsd = jax.nn.log_sigmoid(smd)
        dq = col(CB_DQ)
        dk = col(CB_DK)
        dv = col(CB_DV)
        do = col(CB_DO)
        dnw = dnw_ref[...]
        bcs = cumsum_seg(lsd)
        mrows = [mm_ref[pl.ds(2 * p + w, 1), :] for w in range(2)]
        mnew_rows = [jnp.zeros((1, D_HEADS), f32), jnp.zeros((1, D_HEADS), f32)]
        hlane = lax.broadcasted_iota(i32, (1, D_HEADS), 1)
        for hd in range(D_HEADS):
            hs = slice(hd * LANE, (hd + 1) * LANE)
            li, lf = 8 + hd, 12 + hd
            bcol = bcs[:, lf:lf + 1]
            icol = smd[:, li:li + 1]
            mprev = jnp.where(first, mrows[0][:, hd:hd + 1], mrows[1][:, hd:hd + 1])
            dcols = [jnp.where(seg[s], bcol - bcol[s:s + 1, :] + icol[s:s + 1, :], NEG_BIG) for s in range(8)]
            mx = dcols[0]
            for s in range(1, 8):
                mx = jnp.maximum(mx, dcols[s])
            inter = bcol + mprev
            mt = jnp.maximum(inter, mx)
            w_inter = jnp.exp(inter - mt)
            qh = dq[:, hs] * (D_DK ** -0.5)
            kh = dk[:, hs]
            vh = dv[:, hs]
            cmats = [mc_ref[2 * p + w, hd] for w in range(2)]
            nrows = [mn_ref[2 * p + w, pl.ds(hd, 1), :] for w in range(2)]
            num = w_inter * jnp.dot(both(qh).astype(bf16), jnp.concatenate(cmats, axis=0).astype(bf16),
                                    preferred_element_type=f32)
            den = w_inter * jnp.sum(qh * jnp.where(first, nrows[0], nrows[1]), axis=-1, keepdims=True)
            for s in range(8):
                w = jnp.where(seg[s], jnp.exp(dcols[s] - mt), 0.0)
                qk = jnp.sum(qh * kh[s:s + 1, :], axis=-1, keepdims=True) * w
                num = num + qk * vh[s:s + 1, :]
                den = den + qk
            hh = num / jnp.maximum(jnp.abs(den), jnp.exp(-mt))
            wk = jnp.exp(last_rows(bcol) - bcol + icol - last_rows(mt))
            wkk = wk * kh
            upd = lax.dot_general(wkk.astype(bf16), both(vh).astype(bf16), (((0,), (0,)), ((), ())),
                                  preferred_element_type=f32)
            for w in range(2):
                last = DEC_SEQ - 1 if w == 0 else 7
                mnew = mt[last:last + 1, :]
                decay = jnp.exp(bcol[last:last + 1, :] + mrows[w][:, hd:hd + 1] - mnew)
                mine = first if w == 0 else jnp.logical_not(first)
                mc_out[2 * p + w, hd] = decay * cmats[w] + upd[:, w * LANE:(w + 1) * LANE]
                mn_out[2 * p + w, pl.ds(hd, 1), :] = decay * nrows[w] + jnp.sum(jnp.where(mine, wkk, 0.0), axis=0,
                                                                                 keepdims=True)
                mnew_rows[w] = jnp.where(hlane == hd, mnew, mnew_rows[w])
            od_ref[pl.ds(r, 8), hs] = (jax.nn.sigmoid(do[:, hs]) * _rms(hh, dnw[:, hs])).astype(bf16)
        for w in range(2):
            mm_out[pl.ds(2 * p + w, 1), :] = mnew_rows[w]
        return carry

    lax.fori_loop(0, SB // 2, tile, 0)


_N_SAMPLE_IN = 19


def _sample_body_aliased(n_alias, *refs):
    _sample_body(*refs[:_N_SAMPLE_IN], *refs[_N_SAMPLE_IN + n_alias:])


def _sample_mixer(h, st_hgrn, st_ssm, st_conv, st_c, st_n, st_m, pa, pb, pc, pd, layer, brs, stacked):
    rows = SB * DEC_SEQ
    row0 = N_PROMPT // rows

    def blk(shape):
        nd = len(shape)
        return pl.BlockSpec((SB,) + shape, lambda i, nd=nd: (i,) + (0,) * nd)

    def blk_stacked(shape):
        nd = len(shape)
        return pl.BlockSpec((None, SB) + shape, lambda i, nd=nd: (layer, i) + (0,) * nd)

    def full(shape):
        nd = len(shape)
        return pl.BlockSpec(shape, lambda i, nd=nd: (0,) * nd)

    big = ((B_HEADS, LANE, LANE), (C_HEADS, C_DSTATE, C_HEADDIM), (D_HEADS, LANE, LANE))
    in_state_specs = [blk_stacked(big[0]), blk_stacked(big[1]), blk_stacked((C_CONV - 1, C_CONV_DIM)),
                      blk_stacked(big[2]), blk_stacked((D_HEADS, LANE)), blk_stacked((D_HEADS,))]
    out_state_specs = [blk_stacked(big[0]), blk_stacked(big[1]), blk((C_CONV - 1, C_CONV_DIM)), blk_stacked(big[2]),
                       blk((D_HEADS, LANE)), blk((D_HEADS,))]
    params = list(pa) + list(pb) + list(pc) + list(pd)
    br_spec = pl.BlockSpec((rows, BW), lambda i: (row0 + i, 0))
    br_shape = jax.ShapeDtypeStruct((N_TOK, BW), bf16)
    stacked_shape = lambda a: jax.ShapeDtypeStruct(a.shape, f32)
    per_layer = lambda a: jax.ShapeDtypeStruct(a.shape[1:], f32)
    inputs = [h, st_hgrn, st_ssm, st_conv, st_c, st_n, st_m] + params
    in_specs = [pl.BlockSpec((rows, MIX_W), lambda i: (row0 + i, 0))] + in_state_specs + [full(p.shape) for p in params]
    assert len(inputs) == _N_SAMPLE_IN
    alias_in = list(brs) + (list(stacked) if stacked is not None else [])
    alias_out = [0, 1, 2, 3] + ([5, 6, 8] if stacked is not None else [])
    aliases = {_N_SAMPLE_IN + k: o for k, o in enumerate(alias_out)}
    return pl.pallas_call(
        functools.partial(_sample_body_aliased, len(alias_in)),
        out_shape=(br_shape, br_shape, br_shape, br_shape, jax.ShapeDtypeStruct((N_SAMPLE, BW), f32),
                   stacked_shape(st_hgrn), stacked_shape(st_ssm),
                   per_layer(st_conv), stacked_shape(st_c), per_layer(st_n), per_layer(st_m)),
        grid=(DEC_BATCH // SB,),
        in_specs=in_specs + [pl.BlockSpec(memory_space=pl.ANY)] * len(alias_in),
        out_specs=(br_spec, br_spec, br_spec, br_spec, pl.BlockSpec((rows, BW), lambda i: (i, 0)))
        + tuple(out_state_specs),
        input_output_aliases=aliases,
        compiler_params=_cparams(("parallel",)),
        name="sample_mixer",
    )(*inputs, *alias_in)


def _merge_body(x_ref, nw_ref, ba_ref, bb_ref, bc_ref, bd_ref, wg_ref, wb_ref, o_ref, xn_ref, acc_ref):
    jc = pl.program_id(1)
    n = pl.program_id(2)

    @pl.when((jc == 0) & (n == 0))
    def _():
        xn_ref[...] = _rms(x_ref[...], nw_ref[...]).astype(bf16)

    gate = jax.nn.sigmoid(jnp.dot(xn_ref[...], wg_ref[...], preferred_element_type=f32))
    for k, br_ref in enumerate((ba_ref, bb_ref, bc_ref, bd_ref)):
        @pl.when(n == k)
        def _(br_ref=br_ref, k=k):
            contrib = gate * jnp.dot(br_ref[...], wb_ref[0], preferred_element_type=f32)
            if k == 0:
                acc_ref[...] = contrib
            else:
                acc_ref[...] = acc_ref[...] + contrib

    @pl.when(n == N_BRANCH - 1)
    def _():
        o_ref[...] = acc_ref[...].astype(bf16)


def _merge(x, nw, brs, w_gate, w_branch, tm, tn, layer):
    m = x.shape[0]
    ncol = D_MODEL // tn
    br_spec = pl.BlockSpec((tm, BW), lambda i, jc, n: (i, 0))
    return pl.pallas_call(
        _merge_body,
        out_shape=jax.ShapeDtypeStruct((m, D_MODEL), bf16),
        grid=(m // tm, ncol, N_BRANCH),
        in_specs=[pl.BlockSpec((tm, D_MODEL), lambda i, jc, n: (i, 0)),
                  pl.BlockSpec((1, D_MODEL), lambda i, jc, n: (0, 0)),
                  br_spec, br_spec, br_spec, br_spec,
                  pl.BlockSpec((D_MODEL, tn), lambda i, jc, n: (0, n * ncol + jc)),
                  pl.BlockSpec((None, 1, BW, tn), lambda i, jc, n: (layer, n, 0, jc))],
        out_specs=pl.BlockSpec((tm, tn), lambda i, jc, n: (i, jc)),
        scratch_shapes=[pltpu.VMEM((tm, D_MODEL), bf16), pltpu.VMEM((tm, tn), f32)],
        compiler_params=_cparams(("parallel", "arbitrary", "arbitrary")),
        name="merge",
    )(x, nw, *brs, w_gate, w_branch)


def _top16_rows(s, rid=None):
    if rid is None:
        rid = lax.broadcasted_iota(i32, s.shape, 0).astype(f32)
    out = []
    for _ in range(PEER_TOPK):
        m = jnp.max(s, axis=0, keepdims=True)
        am = jnp.min(jnp.where(s == m, rid, float(PEER_N)), axis=0, keepdims=True)
        out.append((m, am))
        s = jnp.where(rid == am, -jnp.inf, s)
    return out


def _collect16(pairs, tb):
    r16 = lax.broadcasted_iota(i32, (PEER_TOPK, tb), 0)
    v = jnp.zeros((PEER_TOPK, tb), f32)
    ix = jnp.zeros((PEER_TOPK, tb), f32)
    for k, (m, am) in enumerate(pairs):
        v = jnp.where(r16 == k, m, v)
        ix = jnp.where(r16 == k, am, ix)
    return v, ix.astype(i32)


def _peer_route_body(q_ref, keys_ref, ia_ref, ib_ref, gt_ref):
    tb = q_ref.shape[0]
    vals = []
    idxs = []
    for p in range(2):
        st = lax.dot_general(keys_ref[0, p], q_ref[:, p * LANE:(p + 1) * LANE], (((1,), (1,)), ((), ())),
                             preferred_element_type=f32)
        v, ix = _collect16(_top16_rows(st), tb)
        vals.append(v)
        idxs.append(ix)
    va, vb = vals
    half = PEER_TOPK // 2
    cand = jnp.concatenate([va[0:1, :] + vb] + [va[k:k + 1, :] + vb[0:half, :] for k in range(1, half)]
                           + [va[half:, :] + vb[0:1, :]], axis=0)
    r = lax.broadcasted_iota(i32, cand.shape, 0)
    r2 = r - PEER_TOPK
    mid = PEER_TOPK * (1 + (r2 >> 3)) + (r2 & (half - 1))
    pid = jnp.where(r < PEER_TOPK, r, jnp.where(r < PEER_TOPK + half * (half - 1), mid, (r - 8 * half) * PEER_TOPK))
    fs, pos = _collect16(_top16_rows(cand, pid.astype(f32)), tb)
    ka = pos >> 4
    kb = pos & (PEER_TOPK - 1)
    i1 = jnp.zeros((PEER_TOPK, tb), i32)
    i2 = jnp.zeros((PEER_TOPK, tb), i32)
    for j in range(PEER_TOPK):
        i1 = jnp.where(ka == j, idxs[0][j:j + 1, :], i1)
        i2 = jnp.where(kb == j, idxs[1][j:j + 1, :], i2)
    pe = jnp.exp(fs - fs[0:1, :])
    ia_ref[...] = i1
    ib_ref[...] = i2
    gt_ref[...] = pe / jnp.sum(pe, axis=0, keepdims=True)


def _peer_route(q, keys, tb, layer):
    m = q.shape[0]
    spec = pl.BlockSpec((PEER_TOPK, tb), lambda i, hd: (hd, i))
    nslot = PEER_HEADS * PEER_TOPK
    return pl.pallas_call(
        _peer_route_body,
        out_shape=(jax.ShapeDtypeStruct((nslot, m), i32), jax.ShapeDtypeStruct((nslot, m), i32),
                   jax.ShapeDtypeStruct((nslot, m), f32)),
        grid=(m // tb, PEER_HEADS),
        in_specs=[pl.BlockSpec((tb, 2 * LANE), lambda i, hd: (i, hd)),
                  pl.BlockSpec((None, 1, 2, PEER_NKEYS, LANE), lambda i, hd: (layer, hd, 0, 0, 0))],
        out_specs=(spec, spec, spec),
        compiler_params=_cparams(("parallel", "parallel")),
        name="peer_route",
    )(q, keys)


GATE_UNROLL = 64
RELAYOUT_TOKENS = 16


def _peer_gates_body(ia_ref, ib_ref, gt_ref, o_ref, a_s, b_s, g_s, gm_s):
    tg = ia_ref.shape[1]
    a_s[...] = ia_ref[...].astype(f32).T
    b_s[...] = ib_ref[...].astype(f32).T
    g_s[...] = gt_ref[...].T
    sub = lax.broadcasted_iota(i32, (LANE, LANE), 0).astype(f32)

    def step(r, carry):
        r8 = pl.multiple_of(r * GATE_UNROLL, GATE_UNROLL)
        a8 = a_s[pl.ds(r8, GATE_UNROLL), :]
        b8 = b_s[pl.ds(r8, GATE_UNROLL), :]
        g8 = g_s[pl.ds(r8, GATE_UNROLL), :]
        for u in range(GATE_UNROLL):
            at = jnp.where(sub == a8[u:u + 1, :], 1.0, 0.0).astype(bf16)
            bt = jnp.where(sub == b8[u:u + 1, :], g8[u:u + 1, :], 0.0).astype(bf16)
            gm = lax.dot_general(at, bt, (((1,), (1,)), ((), ())), preferred_element_type=f32)
            gm_s[r8 + u] = gm
        return carry

    lax.fori_loop(0, tg // GATE_UNROLL, step, 0)

    def relayout(c, carry):
        t0 = pl.multiple_of(c * RELAYOUT_TOKENS, RELAYOUT_TOKENS)
        for blk in range(PEER_NKEYS // 8):
            x = gm_s[pl.ds(t0, RELAYOUT_TOKENS), pl.ds(blk * 8, 8), :]
            y = jnp.swapaxes(x, 0, 1)
            for i in range(8):
                o_ref[pl.ds(t0, RELAYOUT_TOKENS), pl.ds((blk * 8 + i) * PEER_NKEYS, PEER_NKEYS)] = y[i].astype(bf16)
        return carry

    lax.fori_loop(0, tg // RELAYOUT_TOKENS, relayout, 0)


def _peer_gates(ia, ib, gt, tg):
    nslot, m = ia.shape
    spec = pl.BlockSpec((nslot, tg), lambda i: (0, i))
    return pl.pallas_call(
        _peer_gates_body,
        out_shape=jax.ShapeDtypeStruct((m, PEER_N), bf16),
        grid=(m // tg,),
        in_specs=[spec, spec, spec],
        out_specs=pl.BlockSpec((tg, PEER_N), lambda i: (i, 0)),
        scratch_shapes=[pltpu.VMEM((tg, nslot), f32)] * 3 + [pltpu.VMEM((tg, PEER_NKEYS, PEER_NKEYS), f32)],
        compiler_params=_cparams(("parallel",)),
        name="peer_gates",
    )(ia, ib, gt)


def _peer_experts_body(x_ref, nw_ref, u_ref, v_ref, g_ref, o_ref, xn_ref):
    @pl.when(pl.program_id(1) == 0)
    def _():
        x = x_ref[...]
        xn_ref[...] = _rms(x, nw_ref[...]).astype(bf16)
        o_ref[...] = x

    hmat = lax.dot_general(xn_ref[...], u_ref[...], (((1,), (1,)), ((), ())), preferred_element_type=f32)
    w = (_gelu(hmat) * g_ref[...].astype(f32)).astype(bf16)
    o_ref[...] += jnp.dot(w, v_ref[...], preferred_element_type=f32)


def _peer_experts(x, nw, u, v, g, tb, eb, layer):
    m = x.shape[0]
    return pl.pallas_call(
        _peer_experts_body,
        out_shape=jax.ShapeDtypeStruct((m, D_MODEL), f32),
        grid=(m // tb, PEER_N // eb),
        in_specs=[pl.BlockSpec((tb, D_MODEL), lambda i, j: (i, 0), pipeline_mode=pl.Buffered(1)),
                  pl.BlockSpec((1, D_MODEL), lambda i, j: (0, 0)),
                  pl.BlockSpec((None, eb, D_MODEL), lambda i, j: (layer, j, 0)),
                  pl.BlockSpec((None, eb, D_MODEL), lambda i, j: (layer, j, 0)),
                  pl.BlockSpec((tb, eb), lambda i, j: (i, j))],
        out_specs=pl.BlockSpec((tb, D_MODEL), lambda i, j: (i, 0), pipeline_mode=pl.Buffered(1)),
        scratch_shapes=[pltpu.VMEM((tb, D_MODEL), bf16)],
        compiler_params=_cparams(("parallel", "arbitrary")),
        name="peer_experts",
    )(x, nw, u, v, g)


_IN_OFFS = tuple(int(v) for v in np.cumsum((0,) + IN_SPLITS))
_O_CDT, _O_DQ, _O_DIG, _O_GATES, _O_END = _IN_OFFS[8], _IN_OFFS[9], _IN_OFFS[13], _IN_OFFS[15], _IN_OFFS[16]
PREP_ROWS = 128


def _prep_w_in_body(w_ref, om_ref, og_ref):
    rows = w_ref.shape[0]
    om_ref[:, 0:_O_CDT] = w_ref[:, 0:_O_CDT].astype(bf16)
    om_ref[:, _O_CDT:SMALL_COL] = w_ref[:, _O_DQ:_O_DIG].astype(bf16)
    small = jnp.concatenate([w_ref[:, _O_CDT:_O_DQ], w_ref[:, _O_DIG:_O_GATES],
                             jnp.zeros((rows, LANE - 16), w_ref.dtype)], axis=-1)
    om_ref[:, SMALL_COL:SMALL_COL + LANE] = small.astype(bf16)
    om_ref[:, SMALL_COL + LANE:MIX_W] = jnp.zeros((rows, MIX_W - SMALL_COL - LANE), bf16)
    og_ref[...] = w_ref[:, _O_GATES:_O_END].astype(bf16)


def _prep_w_in(w, layer):
    _, k, n = w.shape
    ng = _O_END - _O_GATES
    return pl.pallas_call(
        _prep_w_in_body,
        out_shape=(jax.ShapeDtypeStruct((k, MIX_W), bf16), jax.ShapeDtypeStruct((k, ng), bf16)),
        grid=(k // PREP_ROWS,),
        in_specs=[pl.BlockSpec((None, PREP_ROWS, n), lambda i: (layer, i, 0))],
        out_specs=(pl.BlockSpec((PREP_ROWS, MIX_W), lambda i: (i, 0)), pl.BlockSpec((PREP_ROWS, ng), lambda i: (i, 0))),
        compiler_params=_cparams(("parallel",)),
        name="prep_w_in",
    )(w)


def _lane_row(vals, start):
    row = jnp.zeros((LANE,), f32)
    return row.at[start:start + vals.shape[0]].set(vals)


def kernel(x_prompt, x_sample, state_hgrn, state_ssm, state_conv, state_mlstm_c, state_mlstm_n, state_mlstm_m, norm1_w, w_in, a_ln_w, a_ln_b, a_ws, a_bs, b_lb_logits, b_norm_w, c_conv_w, c_conv_b, c_dt_bias, c_a_log, c_d, c_norm_w, d_ig_b, d_fg_b, d_norm_w, w_branch, w_out, norm2_w, peer_wq, peer_keys, peer_u, peer_v, final_norm_w):
    x = jnp.concatenate([x_prompt.reshape(N_PROMPT, D_MODEL), x_sample.reshape(N_SAMPLE, D_MODEL)], axis=0)
    lbs = jax.nn.softmax(b_lb_logits.astype(f32), axis=0)
    lbs = jnp.cumsum(lbs, axis=0) - lbs[0]
    zeros8 = jnp.zeros((8, LANE), f32)
    news_p = []
    news_s = []
    stacked_s = None
    w_in_b, w_branch_b, w_out_b, wq_b = (w.astype(bf16) for w in (w_in, w_branch, w_out, peer_wq))
    keys_b, u_b, v_b = (w.astype(bf16) for w in (peer_keys, peer_u, peer_v))
    for l in range(DEPTH):
        w_mix, w_gate = _prep_w_in(w_in_b, l)
        h = _norm_mm(x, norm1_w[l][None, :], w_mix, f32, 1088, 1024)

        lnw = a_ln_w[l][None, :]
        lnb = a_ln_b[l][None, :]
        lb = lbs[l][None, :]
        bnw = b_norm_w[l][None, :]
        cw = c_conv_w[l]
        cb = c_conv_b[l][None, :]
        cpar = zeros8.at[0].set(_lane_row(c_dt_bias[l], 0)).at[1].set(_lane_row(c_a_log[l], 0)).at[2].set(_lane_row(c_d[l], 0))
        cnw = c_norm_w[l][None, :]
        dpar = zeros8.at[0].set(_lane_row(d_ig_b[l], 8) + _lane_row(d_fg_b[l], 12))
        dnw = d_norm_w[l][None, :]

        bs_full = jnp.repeat(a_bs[l].T, LANE, axis=1)
        br_a = _prompt_a(h, lnw, lnb, a_ws[l], bs_full)
        br_b, hg_p = _prompt_b(h, lb, bnw)
        br_c, ssm_p, cv_p = _prompt_c(h, cw, cb, cpar, cnw)
        br_d, mc_p, mn_p, mm_p = _prompt_d(h, dpar, dnw)
        news_p.append((hg_p, ssm_p, cv_p, mc_p, mn_p[:, :D_HEADS, :], mm_p[:, :D_HEADS, 0]))

        w4 = jnp.tril(a_ws[l][:, :DEC_SEQ, :DEC_SEQ])
        w8 = jnp.zeros((A_GROUPS, 8, 8), f32).at[:, :4, :4].set(w4).at[:, 4:, 4:].set(w4)
        ws8 = jnp.repeat(jnp.transpose(w8, (2, 1, 0)), LANE, axis=2)
        bs8 = jnp.repeat(jnp.tile(a_bs[l][:, :DEC_SEQ], (1, 2)).T, LANE, axis=1)
        outs = _sample_mixer(h, state_hgrn, state_ssm, state_conv, state_mlstm_c, state_mlstm_n, state_mlstm_m,
                             (lnw, lnb, ws8, bs8), (lb, bnw), (cw, cb, cpar, cnw), (dpar, dnw), l,
                             (br_a, br_b, br_c, br_d), stacked_s)
        brs = outs[:4]
        stacked_s = (outs[5], outs[6], outs[8])
        news_s.append((outs[7], outs[9], outs[10], outs[4].reshape(DEC_BATCH, DEC_SEQ, BW)))

        mixin = _merge(x, norm1_w[l][None, :], brs, w_gate, w_branch_b, 1088, 1024, l)
        x = _mm_res(mixin, w_out_b, x, 1088, 512, l)

        q = _norm_mm(x, norm2_w[l][None, :], wq_b, bf16, 1088, 512, l)
        ia, ib, gt = _peer_route(q, keys_b, 512, l)
        g = _peer_gates(ia, ib, gt, 128)
        x = _peer_experts(x, norm2_w[l][None, :], u_b, v_b, g, 1088, 1024, l)

    y_p, y_s = _final_norm(x, final_norm_w[None, :])
    y_prompt = y_p.reshape(BATCH, SEQ, D_MODEL)
    y_sample = y_s.reshape(DEC_BATCH, DEC_SEQ, D_MODEL)
    stack = lambda news, k: jnp.stack([n[k] for n in news], axis=0)
    hgrn_s, ssm_s, mc_s = stacked_s
    conv_s, mn_s, mm_s, chunk_v_s = (stack(news_s, k) for k in range(4))
    return ((y_prompt, y_sample) + tuple(stack(news_p, k) for k in range(6))
            + (hgrn_s, ssm_s, conv_s, mc_s, mn_s, mm_s, chunk_v_s))
```

```python
import functools

import jax
import jax.numpy as jnp
import numpy as np
from jax import lax
from jax.experimental import pallas as pl
from jax.experimental.pallas import tpu as pltpu

f32 = jnp.float32
bf16 = jnp.bfloat16
i32 = jnp.int32
HI = lax.Precision.HIGHEST

D_MODEL = 2048
BATCH = 4
SEQ = 2048
DEPTH = 2
DEC_BATCH = 128
DEC_SEQ = 4
N_BRANCH = 4
BW = 512
A_GROUPS = 4
A_CHUNK = 128
B_HEADS = 4
LB_FLOOR = 1e-30
C_HEADDIM = 64
C_HEADS = 8
C_GROUPS = 2
C_DSTATE = 128
C_CONV = 4
C_CONV_DIM = 1024
D_HEADS = 4
D_DK = 128
NEG_BIG = -1e30
IN_SPLITS = (512, 512, 512, 512, 512, 512, 512, 1024, 8, 512, 512, 512, 512, 4, 4, 8192)
PEER_NKEYS = 128
PEER_N = PEER_NKEYS * PEER_NKEYS
PEER_HEADS = 8
PEER_TOPK = 16
EPS = 1e-6
INV_SQRT2 = 0.7071067811865476

N_PROMPT = BATCH * SEQ
N_SAMPLE = DEC_BATCH * DEC_SEQ
N_TOK = N_PROMPT + N_SAMPLE

CB_AU, CB_AV, CB_BQ, CB_BF, CB_BI, CB_BG, CB_CZ, CB_CX0, CB_CX1, CB_DQ, CB_DK, CB_DV, CB_DO = range(13)
SMALL_COL = 13 * 512
MIX_W = 14 * 512
LANE = 128
CHUNK = 128
NCHUNK = SEQ // CHUNK
SUB = 16
SB = 8
VMEM_LIMIT = 56 * 1024 * 1024


def _gelu(x):
    return 0.5 * x * (1.0 + lax.erf(x * INV_SQRT2))


def _rms(x, w):
    ms = jnp.mean(x * x, axis=-1, keepdims=True)
    return x * lax.rsqrt(ms + EPS) * w


def _tri(n):
    r = lax.broadcasted_iota(i32, (n, n), 0)
    c = lax.broadcasted_iota(i32, (n, n), 1)
    return r >= c


def _cumsum_rows_small(x, n):
    row = lax.broadcasted_iota(i32, (n, 1), 0)
    acc = jnp.zeros_like(x)
    for s in range(n):
        acc = acc + jnp.where(row >= s, x[s:s + 1, :], 0.0)
    return acc


def _row_to_col(row, eye):
    return jnp.sum(jnp.where(eye, row, 0.0), axis=-1, keepdims=True)


def _cparams(sem, vmem=VMEM_LIMIT):
    return pltpu.CompilerParams(dimension_semantics=sem, vmem_limit_bytes=vmem)


def _norm_mm_body(x_ref, nw_ref, w_ref, o_ref, xn_ref):
    @pl.when(pl.program_id(1) == 0)
    def _():
        xn_ref[...] = _rms(x_ref[...], nw_ref[...]).astype(bf16)

    o_ref[...] = jnp.dot(xn_ref[...], w_ref[...], preferred_element_type=f32).astype(o_ref.dtype)


def _w_spec(w, layer, block, index_map):
    if layer is None:
        return pl.BlockSpec(block, index_map)
    return pl.BlockSpec((None,) + block, lambda *g: (layer,) + index_map(*g))


def _norm_mm(x, nw, w, out_dtype, tm, tn, layer=None):
    m, k = x.shape
    n = w.shape[-1]
    return pl.pallas_call(
        _norm_mm_body,
        out_shape=jax.ShapeDtypeStruct((m, n), out_dtype),
        grid=(m // tm, n // tn),
        in_specs=[pl.BlockSpec((tm, k), lambda i, j: (i, 0)),
                  pl.BlockSpec((1, k), lambda i, j: (0, 0)),
                  _w_spec(w, layer, (k, tn), lambda i, j: (0, j))],
        out_specs=pl.BlockSpec((tm, tn), lambda i, j: (i, j)),
        scratch_shapes=[pltpu.VMEM((tm, k), bf16)],
        compiler_params=_cparams(("parallel", "arbitrary")),
        name="norm_mm",
    )(x, nw, w)


def _mm_res_body(a_ref, w_ref, r_ref, o_ref):
    o_ref[...] = r_ref[...] + jnp.dot(a_ref[...], w_ref[...], preferred_element_type=f32)


def _mm_res(a, w, res, tm, tn, layer=None):
    m, k = a.shape
    n = w.shape[-1]
    return pl.pallas_call(
        _mm_res_body,
        out_shape=jax.ShapeDtypeStruct((m, n), f32),
        grid=(m // tm, n // tn),
        in_specs=[pl.BlockSpec((tm, k), lambda i, j: (i, 0)),
                  _w_spec(w, layer, (k, tn), lambda i, j: (0, j)),
                  pl.BlockSpec((tm, tn), lambda i, j: (i, j))],
        out_specs=pl.BlockSpec((tm, tn), lambda i, j: (i, j)),
        compiler_params=_cparams(("parallel", "arbitrary")),
        name="mm_res",
    )(a, w, res)


def _final_norm_body(x_ref, w_ref, op_ref, os_ref):
    y = _rms(x_ref[...], w_ref[...])
    i = pl.program_id(0)

    @pl.when(i < N_PROMPT // N_SAMPLE)
    def _():
        op_ref[...] = y

    @pl.when(i == N_PROMPT // N_SAMPLE)
    def _():
        os_ref[...] = y


def _final_norm(x, w):
    m, k = x.shape
    tm = N_SAMPLE
    last_p = N_PROMPT // tm - 1
    return pl.pallas_call(
        _final_norm_body,
        out_shape=(jax.ShapeDtypeStruct((N_PROMPT, k), f32), jax.ShapeDtypeStruct((N_SAMPLE, k), f32)),
        grid=(m // tm,),
        in_specs=[pl.BlockSpec((tm, k), lambda i: (i, 0)), pl.BlockSpec((1, k), lambda i: (0, 0))],
        out_specs=(pl.BlockSpec((tm, k), lambda i: (jnp.minimum(i, last_p), 0)),
                   pl.BlockSpec((tm, k), lambda i: (0, 0))),
        compiler_params=_cparams(("arbitrary",)),
        name="final_norm",
    )(x, w)


def _a_uv(au, av, lnw, lnb):
    u = _gelu(au)
    g = _gelu(av)
    xc = g - jnp.mean(g, axis=-1, keepdims=True)
    var = jnp.mean(xc * xc, axis=-1, keepdims=True)
    v = xc * lax.rsqrt(var + EPS) * lnw + lnb
    return u, v


def _b_pre(bq, bf_, lb):
    q = bq * jax.nn.sigmoid(bq)
    logf = jnp.logaddexp(jnp.log(jnp.maximum(lb, LB_FLOOR)), jnp.log1p(-lb) + jax.nn.log_sigmoid(bf_))
    kb = (1.0 - lb) * jax.nn.sigmoid(-bf_)
    return q, kb, logf


def _hgrn_chunks(chains, n, eye):
    row = lax.broadcasted_iota(i32, (n, 1), 0)
    qs, ks, vs, gs, ss = [], [], [], [], []
    for q, k, v, gl, s_mat, valid in chains:
        if valid is not None:
            gl = jnp.where(valid, gl, 0.0)
            k = jnp.where(valid, k, 0.0)
        qs.append(q)
        ks.append(k)
        vs.append(v)
        gs.append(_cumsum_rows_small(gl, n))
        ss.append(s_mat)
    nc = len(chains)
    os_ = [jnp.dot((qs[c] * jnp.exp(gs[c])).astype(bf16), ss[c].astype(bf16), preferred_element_type=f32)
           for c in range(nc)]
    upds = []
    for c in range(nc):
        k_dec = ks[c] * jnp.exp(gs[c][n - 1:n, :] - gs[c])
        upds.append(lax.dot_general(k_dec.astype(bf16), vs[c].astype(bf16), (((0,), (0,)), ((), ())),
                                    preferred_element_type=f32))
    for s in range(n):
        m = row >= s
        for c in range(nc):
            g = gs[c]
            p = jnp.where(m, qs[c] * ks[c][s:s + 1, :] * jnp.exp(g - g[s:s + 1, :]), 0.0)
            os_[c] = os_[c] + jnp.sum(p, axis=-1, keepdims=True) * vs[c][s:s + 1, :]
    out = []
    for c in range(nc):
        s_new = jnp.exp(_row_to_col(gs[c][n - 1:n, :], eye)) * ss[c] + upds[c]
        out.append((os_[c], s_new))
    return out


def _b_post(o, bg, nw):
    return _rms(o, nw) * (bg * jax.nn.sigmoid(bg))


def _hrow(b, c):
    return b * NCHUNK + c


def _h_spec(colblk):
    return pl.BlockSpec((CHUNK, BW), lambda b, c, cb=colblk: (_hrow(b, c), cb))


_SMALL_SPEC = pl.BlockSpec((CHUNK, LANE), lambda b, c: (_hrow(b, c), SMALL_COL // LANE))
_BR_SPEC = pl.BlockSpec((CHUNK, BW), lambda b, c: (_hrow(b, c), 0))


def _full_spec(shape):
    nd = len(shape)
    return pl.BlockSpec(shape, lambda b, c, nd=nd: (0,) * nd)


def _pa_body(au_ref, av_ref, lnw_ref, lnb_ref, ws_ref, bs_ref, o_ref):
    u, v = _a_uv(au_ref[...], av_ref[...], lnw_ref[...], lnb_ref[...])
    tri = _tri(CHUNK)
    vb = v.astype(bf16)
    parts = []
    for g in range(A_GROUPS):
        w = jnp.where(tri, ws_ref[g], 0.0).astype(bf16)
        parts.append(jnp.dot(w, vb[:, g * LANE:(g + 1) * LANE], preferred_element_type=f32))
    sp = jnp.concatenate(parts, axis=-1) + bs_ref[...]
    o_ref[...] = (u * sp).astype(bf16)


def _prompt_a(h, lnw, lnb, ws, bs_full):
    return pl.pallas_call(
        _pa_body,
        out_shape=jax.ShapeDtypeStruct((N_TOK, BW), bf16),
        grid=(BATCH, NCHUNK),
        in_specs=[_h_spec(CB_AU), _h_spec(CB_AV), _full_spec((1, BW)), _full_spec((1, BW)),
                  _full_spec((A_GROUPS, A_CHUNK, A_CHUNK)), _full_spec((A_CHUNK, BW))],
        out_specs=_BR_SPEC,
        compiler_params=_cparams(("parallel", "parallel")),
        name="prompt_gmlp",
    )(h, h, lnw, lnb, ws, bs_full)


def _pb_body(bq_ref, bf_ref, bi_ref, bg_ref, lb_ref, nw_ref, o_ref, st_ref, s_ref, q_s, k_s, v_s, g_s, o_s):
    c = pl.program_id(1)

    @pl.when(c == 0)
    def _():
        s_ref[...] = jnp.zeros_like(s_ref)

    q, kb, logf = _b_pre(bq_ref[...], bf_ref[...], lb_ref[...])
    q_s[...] = q
    k_s[...] = kb
    v_s[...] = bi_ref[...]
    g_s[...] = logf
    eye = lax.broadcasted_iota(i32, (LANE, LANE), 0) == lax.broadcasted_iota(i32, (LANE, LANE), 1)
    def sub(j, carry):
        r = pl.multiple_of(j * SUB, SUB)
        heads = [slice(hd * LANE, (hd + 1) * LANE) for hd in range(B_HEADS)]
        chains = [(q_s[pl.ds(r, SUB), hs], k_s[pl.ds(r, SUB), hs], v_s[pl.ds(r, SUB), hs], g_s[pl.ds(r, SUB), hs],
                   s_ref[hd], None) for hd, hs in enumerate(heads)]
        for hd, (o, s_new) in enumerate(_hgrn_chunks(chains, SUB, eye)):
            o_s[pl.ds(r, SUB), heads[hd]] = o
            s_ref[hd] = s_new
        return carry

    lax.fori_loop(0, CHUNK // SUB, sub, 0, unroll=2)
    bg = bg_ref[...]
    nw = nw_ref[...]
    for hd in range(B_HEADS):
        hs = slice(hd * LANE, (hd + 1) * LANE)
        o_ref[:, hs] = _b_post(o_s[:, hs], bg[:, hs], nw).astype(bf16)

    @pl.when(c == NCHUNK - 1)
    def _():
        st_ref[0] = s_ref[...]


def _prompt_b(h, lb, nw):
    return pl.pallas_call(
        _pb_body,
        out_shape=(jax.ShapeDtypeStruct((N_TOK, BW), bf16),
                   jax.ShapeDtypeStruct((BATCH, B_HEADS, LANE, LANE), f32)),
        grid=(BATCH, NCHUNK),
        in_specs=[_h_spec(CB_BQ), _h_spec(CB_BF), _h_spec(CB_BI), _h_spec(CB_BG),
                  _full_spec((1, BW)), _full_spec((1, LANE))],
        out_specs=(_BR_SPEC, pl.BlockSpec((1, B_HEADS, LANE, LANE), lambda b, c: (b, 0, 0, 0))),
        scratch_shapes=[pltpu.VMEM((B_HEADS, LANE, LANE), f32)] + [pltpu.VMEM((CHUNK, BW), f32)] * 5,
        compiler_params=_cparams(("parallel", "arbitrary")),
        name="prompt_hgrn",
    )(h, h, h, h, lb, nw)


def _c_conv_silu(win0, win1, win2, win3, cw_ref, cb_ref):
    y = cb_ref[...] + win0 * cw_ref[0:1, :] + win1 * cw_ref[1:2, :] + win2 * cw_ref[2:3, :] + win3 * cw_ref[3:4, :]
    return y * jax.nn.sigmoid(y)


def _c_post(yc, cz, nw):
    y = yc * (cz * jax.nn.sigmoid(cz))
    gw = BW // C_GROUPS
    parts = [_rms(y[:, g * gw:(g + 1) * gw], nw[:, g * gw:(g + 1) * gw]) for g in range(C_GROUPS)]
    return jnp.concatenate(parts, axis=-1)


def _pc_body(cz_ref, cx0_ref, cx1_ref, sm_ref, cw_ref, cb_ref, par_ref, nw_ref,
             o_ref, st_ref, cv_ref, xpad, sp_ref):
    c = pl.program_id(1)

    @pl.when(c == 0)
    def _():
        xpad[pl.ds(0, 8), :] = jnp.zeros((8, C_CONV_DIM), f32)
        sp_ref[...] = jnp.zeros_like(sp_ref)

    xpad[pl.ds(8, CHUNK), 0:BW] = cx0_ref[...]
    xpad[pl.ds(8, CHUNK), BW:2 * BW] = cx1_ref[...]
    xbc = _c_conv_silu(xpad[pl.ds(5, CHUNK), :], xpad[pl.ds(6, CHUNK), :], xpad[pl.ds(7, CHUNK), :],
                       xpad[pl.ds(8, CHUNK), :], cw_ref, cb_ref)

    @pl.when(c == NCHUNK - 1)
    def _():
        cv_ref[0] = xpad[pl.ds(CHUNK + 5, 3), :]

    xpad[pl.ds(0, 8), :] = xpad[pl.ds(CHUNK, 8), :]

    xs = xbc[:, 0:BW]
    bm = xbc[:, BW:BW + 2 * C_DSTATE]
    cm = xbc[:, BW + 2 * C_DSTATE:]
    par = par_ref[...]
    dt = jax.nn.softplus(sm_ref[...] + par[0:1, :])
    gl = dt * (-jnp.exp(par[1:2, :]))
    tri = _tri(CHUNK)
    g = jnp.dot(tri.astype(f32), gl, precision=HI, preferred_element_type=f32)
    gt = g.T
    dtt = dt.T
    lane = lax.broadcasted_iota(i32, (1, LANE), 1)
    lo = lane < C_HEADDIM
    cbs = []
    for grp in range(C_GROUPS):
        cg = cm[:, grp * C_DSTATE:(grp + 1) * C_DSTATE].astype(bf16)
        bg = bm[:, grp * C_DSTATE:(grp + 1) * C_DSTATE].astype(bf16)
        cbs.append(lax.dot_general(cg, bg, (((1,), (1,)), ((), ())), preferred_element_type=f32))
    ys = []
    for pr in range(C_HEADS // 2):
        xp = xs[:, pr * LANE:(pr + 1) * LANE]
        sp = sp_ref[pr]
        y = jnp.zeros((CHUNK, LANE), f32)
        upd = jnp.zeros((C_DSTATE, LANE), f32)
        dl = jnp.zeros((1, LANE), f32)
        cdl = jnp.zeros((1, LANE), f32)
        for sub in range(2):
            hd = 2 * pr + sub
            grp = hd // (C_HEADS // C_GROUPS)
            lm = lo if sub == 0 else jnp.logical_not(lo)
            col = g[:, hd:hd + 1]
            g_last = col[CHUNK - 1:CHUNK, :]
            dec = jnp.exp(col - gt[hd:hd + 1, :])
            sc = jnp.where(tri, cbs[grp] * dec * dtt[hd:hd + 1, :], 0.0)
            xm = jnp.where(lm, xp, 0.0).astype(bf16)
            cg = cm[:, grp * C_DSTATE:(grp + 1) * C_DSTATE]
            bg = bm[:, grp * C_DSTATE:(grp + 1) * C_DSTATE]
            y = y + jnp.dot(sc.astype(bf16), xm, preferred_element_type=f32)
            y = y + jnp.dot((cg * jnp.exp(col)).astype(bf16), jnp.where(lm, sp, 0.0).astype(bf16),
                            preferred_element_type=f32)
            kd = bg * (dt[:, hd:hd + 1] * jnp.exp(g_last - col))
            upd = upd + lax.dot_general(kd.astype(bf16), xm, (((0,), (0,)), ((), ())), preferred_element_type=f32)
            dl = jnp.where(lm, jnp.exp(g_last), dl)
            cdl = jnp.where(lm, par[2:3, hd:hd + 1], cdl)
        sp_ref[pr] = dl * sp + upd
        ys.append(y + cdl * xp)
    yc = jnp.concatenate(ys, axis=-1)
    o_ref[...] = _c_post(yc, cz_ref[...], nw_ref[...]).astype(bf16)

    @pl.when(c == NCHUNK - 1)
    def _():
        for pr in range(C_HEADS // 2):
            st_ref[0, 2 * pr] = sp_ref[pr][:, 0:C_HEADDIM]
            st_ref[0, 2 * pr + 1] = sp_ref[pr][:, C_HEADDIM:]


def _prompt_c(h, cw, cb, par, nw):
    return pl.pallas_call(
        _pc_body,
        out_shape=(jax.ShapeDtypeStruct((N_TOK, BW), bf16),
                   jax.ShapeDtypeStruct((BATCH, C_HEADS, C_DSTATE, C_HEADDIM), f32),
                   jax.ShapeDtypeStruct((BATCH, C_CONV - 1, C_CONV_DIM), f32)),
        grid=(BATCH, NCHUNK),
        in_specs=[_h_spec(CB_CZ), _h_spec(CB_CX0), _h_spec(CB_CX1), _SMALL_SPEC,
                  _full_spec((C_CONV, C_CONV_DIM)), _full_spec((1, C_CONV_DIM)), _full_spec((8, LANE)),
                  _full_spec((1, BW))],
        out_specs=(_BR_SPEC,
                   pl.BlockSpec((1, C_HEADS, C_DSTATE, C_HEADDIM), lambda b, c: (b, 0, 0, 0)),
                   pl.BlockSpec((1, C_CONV - 1, C_CONV_DIM), lambda b, c: (b, 0, 0))),
        scratch_shapes=[pltpu.VMEM((CHUNK + 8, C_CONV_DIM), f32), pltpu.VMEM((C_HEADS // 2, C_DSTATE, LANE), f32)],
        compiler_params=_cparams(("parallel", "arbitrary")),
        name="prompt_ssd",
    )(h, h, h, h, cw, cb, par, nw)


def _pd_body(dq_ref, dk_ref, dv_ref, do_ref, sm_ref, par_ref, nw_ref,
             o_ref, c_out, n_out, m_out, c_ref, n_ref, m_ref):
    c = pl.program_id(1)

    @pl.when(c == 0)
    def _():
        c_ref[...] = jnp.zeros_like(c_ref)
        n_ref[...] = jnp.zeros_like(n_ref)
        m_ref[...] = jnp.zeros_like(m_ref)

    sm = sm_ref[...] + par_ref[0:1, :]
    ls = jax.nn.log_sigmoid(sm)
    tri = _tri(CHUNK)
    bc = jnp.dot(tri.astype(f32), ls, precision=HI, preferred_element_type=f32)
    bct = bc.T
    smt = sm.T
    nw = nw_ref[...]
    for hd in range(D_HEADS):
        hs = slice(hd * LANE, (hd + 1) * LANE)
        li, lf = 8 + hd, 12 + hd
        bcol = bc[:, lf:lf + 1]
        brow = bct[lf:lf + 1, :]
        irow = smt[li:li + 1, :]
        icol = sm[:, li:li + 1]
        mprev = m_ref[hd:hd + 1, 0:1]
        dmat = jnp.where(tri, bcol - brow + irow, NEG_BIG)
        inter = bcol + mprev
        mt = jnp.maximum(inter, jnp.max(dmat, axis=-1, keepdims=True))
        w_intra = jnp.where(tri, jnp.exp(dmat - mt), 0.0)
        w_inter = jnp.exp(inter - mt)
        qh = dq_ref[:, hs] * (D_DK ** -0.5)
        kh = dk_ref[:, hs]
        vh = dv_ref[:, hs]
        qb = qh.astype(bf16)
        qk = lax.dot_general(qb, kh.astype(bf16), (((1,), (1,)), ((), ())), preferred_element_type=f32) * w_intra
        num = w_inter * jnp.dot(qb, c_ref[hd].astype(bf16), preferred_element_type=f32)
        num = num + jnp.dot(qk.astype(bf16), vh.astype(bf16), preferred_element_type=f32)
        den = w_inter * jnp.sum(qh * n_ref[hd:hd + 1, :], axis=-1, keepdims=True) + jnp.sum(qk, axis=-1, keepdims=True)
        hh = num / jnp.maximum(jnp.abs(den), jnp.exp(-mt))
        mnew = mt[CHUNK - 1:CHUNK, :]
        blast = bcol[CHUNK - 1:CHUNK, :]
        wk = jnp.exp(blast - bcol + icol - mnew)
        decay = jnp.exp(blast + mprev - mnew)
        wkk = wk * kh
        c_ref[hd] = decay * c_ref[hd] + lax.dot_general(wkk.astype(bf16), vh.astype(bf16), (((0,), (0,)), ((), ())),
                                                        preferred_element_type=f32)
        n_ref[hd:hd + 1, :] = decay * n_ref[hd:hd + 1, :] + jnp.sum(wkk, axis=0, keepdims=True)
        m_ref[hd:hd + 1, :] = jnp.broadcast_to(mnew, (1, LANE))
        o_ref[:, hs] = (jax.nn.sigmoid(do_ref[:, hs]) * _rms(hh, nw[:, hs])).astype(bf16)

    @pl.when(c == NCHUNK - 1)
    def _():
        c_out[0] = c_ref[...]
        n_out[0] = n_ref[...]
        m_out[0] = m_ref[...]


def _prompt_d(h, par, nw):
    return pl.pallas_call(
        _pd_body,
        out_shape=(jax.ShapeDtypeStruct((N_TOK, BW), bf16),
                   jax.ShapeDtypeStruct((BATCH, D_HEADS, LANE, LANE), f32),
                   jax.ShapeDtypeStruct((BATCH, 8, LANE), f32),
                   jax.ShapeDtypeStruct((BATCH, 8, LANE), f32)),
        grid=(BATCH, NCHUNK),
        in_specs=[_h_spec(CB_DQ), _h_spec(CB_DK), _h_spec(CB_DV), _h_spec(CB_DO), _SMALL_SPEC,
                  _full_spec((8, LANE)), _full_spec((1, BW))],
        out_specs=(_BR_SPEC,
                   pl.BlockSpec((1, D_HEADS, LANE, LANE), lambda b, c: (b, 0, 0, 0)),
                   pl.BlockSpec((1, 8, LANE), lambda b, c: (b, 0, 0)),
                   pl.BlockSpec((1, 8, LANE), lambda b, c: (b, 0, 0))),
        scratch_shapes=[pltpu.VMEM((D_HEADS, LANE, LANE), f32), pltpu.VMEM((8, LANE), f32), pltpu.VMEM((8, LANE), f32)],
        compiler_params=_cparams(("parallel", "arbitrary")),
        name="prompt_mlstm",
    )(h, h, h, h, h, par, nw)


def _sample_body(h_ref, hg_ref, ssm_ref, cv_ref, mc_ref, mn_ref, mm_ref,
                 lnw_ref, lnb_ref, ws8_ref, bs8_ref, lb_ref, bnw_ref,
                 cw_ref, cb_ref, cpar_ref, cnw_ref, dpar_ref, dnw_ref,
                 oa_ref, ob_ref, oc_ref, od_ref, chv_ref,
                 hg_out, ssm_out, cv_out, mc_out, mn_out, mm_out):
    row = lax.broadcasted_iota(i32, (8, 1), 0)
    first = row < DEC_SEQ
    tpos = row & (DEC_SEQ - 1)
    seg = [((row >= s) & first) if s < DEC_SEQ else (row >= s) for s in range(8)]

    def cumsum_seg(x):
        acc = jnp.zeros_like(x)
        for s in range(8):
            acc = acc + jnp.where(seg[s], x[s:s + 1, :], 0.0)
        return acc

    def both(x):
        return jnp.concatenate([jnp.where(first, x, 0.0), jnp.where(first, 0.0, x)], axis=-1)

    def last_rows(x):
        return jnp.where(first, x[DEC_SEQ - 1:DEC_SEQ, :], x[7:8, :])
    eye = lax.broadcasted_iota(i32, (LANE, LANE), 0) == lax.broadcasted_iota(i32, (LANE, LANE), 1)
    lane = lax.broadcasted_iota(i32, (1, LANE), 1)
    lo = lane < C_HEADDIM

    def tile(p, carry):
        r = pl.multiple_of(p * 8, 8)

        def col(blk, width=BW):
            return h_ref[pl.ds(r, 8), blk * BW:blk * BW + width]

        small = h_ref[pl.ds(r, 8), SMALL_COL:SMALL_COL + LANE]

        u, v = _a_uv(col(CB_AU), col(CB_AV), lnw_ref[...], lnb_ref[...])
        chv_ref[pl.ds(r, 8), :] = v
        sp = bs8_ref[...]
        for s in range(8):
            sp = sp + ws8_ref[s] * v[s:s + 1, :]
        oa_ref[pl.ds(r, 8), :] = (u * sp).astype(bf16)

        q, kb, logf = _b_pre(col(CB_BQ), col(CB_BF), lb_ref[...])
        bi = col(CB_BI)
        bg = col(CB_BG)
        for hd in range(B_HEADS):
            hs = slice(hd * LANE, (hd + 1) * LANE)
            qh, kh, vh = q[:, hs], kb[:, hs], bi[:, hs]
            g = cumsum_seg(logf[:, hs])
            s_a = hg_ref[2 * p, hd]
            s_b = hg_ref[2 * p + 1, hd]
            o = jnp.dot(both(qh * jnp.exp(g)).astype(bf16), jnp.concatenate([s_a, s_b], axis=0).astype(bf16),
                        preferred_element_type=f32)
            for s in range(8):
                m = seg[s]
                pp = jnp.where(m, qh * kh[s:s + 1, :] * jnp.exp(g - g[s:s + 1, :]), 0.0)
                o = o + jnp.sum(pp, axis=-1, keepdims=True) * vh[s:s + 1, :]
            k_dec = kh * jnp.exp(last_rows(g) - g)
            upd = lax.dot_general(k_dec.astype(bf16), both(vh).astype(bf16), (((0,), (0,)), ((), ())),
                                  preferred_element_type=f32)
            hg_out[2 * p, hd] = jnp.exp(_row_to_col(g[DEC_SEQ - 1:DEC_SEQ, :], eye)) * s_a + upd[:, 0:LANE]
            hg_out[2 * p + 1, hd] = jnp.exp(_row_to_col(g[7:8, :], eye)) * s_b + upd[:, LANE:]
            ob_ref[pl.ds(r, 8), hs] = _b_post(o, bg[:, hs], bnw_ref[...]).astype(bf16)

        x = h_ref[pl.ds(r, 8), CB_CX0 * BW:CB_CX0 * BW + C_CONV_DIM]
        bufs = [jnp.where(first, cv_ref[2 * p, k:k + 1, :], cv_ref[2 * p + 1, k:k + 1, :]) for k in range(3)]
        r1 = pltpu.roll(x, 1, 0)
        r2 = pltpu.roll(x, 2, 0)
        r3 = pltpu.roll(x, 3, 0)
        sh1 = jnp.where(tpos >= 1, r1, bufs[2])
        sh2 = jnp.where(tpos >= 2, r2, jnp.where(tpos == 0, bufs[1], bufs[2]))
        sh3 = jnp.where(tpos >= 3, r3, jnp.where(tpos == 0, bufs[0], jnp.where(tpos == 1, bufs[1], bufs[2])))
        xbc = _c_conv_silu(sh3, sh2, sh1, x, cw_ref, cb_ref)
        cv_out[2 * p] = pltpu.roll(x, 7, 0)[0:3, :]
        cv_out[2 * p + 1] = r3[0:3, :]
        xs = xbc[:, 0:BW]
        bm = xbc[:, BW:BW + 2 * C_DSTATE]
        cm = xbc[:, BW + 2 * C_DSTATE:]
        cpar = cpar_ref[...]
        dt = jax.nn.softplus(small + cpar[0:1, :])
        gl_all = dt * (-jnp.exp(cpar[1:2, :]))
        g = cumsum_seg(gl_all)
        g_lastr = last_rows(g)
        dots = []
        for grp in range(C_GROUPS):
            cg = cm[:, grp * C_DSTATE:(grp + 1) * C_DSTATE]
            bgp = bm[:, grp * C_DSTATE:(grp + 1) * C_DSTATE]
            dots.append([jnp.sum(cg * bgp[s:s + 1, :], axis=-1, keepdims=True) for s in range(8)])
        ys = []
        for pr in range(C_HEADS // 2):
            xp = xs[:, pr * LANE:(pr + 1) * LANE]
            grp = (2 * pr) // (C_HEADS // C_GROUPS)
            cg = cm[:, grp * C_DSTATE:(grp + 1) * C_DSTATE]
            bgp = bm[:, grp * C_DSTATE:(grp + 1) * C_DSTATE]
            spairs = [jnp.concatenate([ssm_ref[2 * p + w, 2 * pr], ssm_ref[2 * p + w, 2 * pr + 1]], axis=-1)
                      for w in range(2)]
            y = jnp.zeros((8, LANE), f32)
            ces, kds, xms = [], [], []
            for sub in range(2):
                hd = 2 * pr + sub
                lm = lo if sub == 0 else jnp.logical_not(lo)
                gcol = g[:, hd:hd + 1]
                xm = jnp.where(lm, xp, 0.0)
                for s in range(8):
                    coef = dots[grp][s] * jnp.where(seg[s], jnp.exp(gcol - gcol[s:s + 1, :]) * dt[s:s + 1, hd:hd + 1], 0.0)
                    y = y + coef * xm[s:s + 1, :]
                ces.append(cg * jnp.exp(gcol))
                kds.append(bgp * (dt[:, hd:hd + 1] * jnp.exp(g_lastr[:, hd:hd + 1] - gcol)))
                xms.append(xm)
            lhs = jnp.concatenate([jnp.where(first, ces[0], 0.0), jnp.where(first, ces[1], 0.0),
                                   jnp.where(first, 0.0, ces[0]), jnp.where(first, 0.0, ces[1])], axis=-1)
            rhs = jnp.concatenate([jnp.where(lo, spairs[0], 0.0), jnp.where(lo, 0.0, spairs[0]),
                                   jnp.where(lo, spairs[1], 0.0), jnp.where(lo, 0.0, spairs[1])], axis=0)
            y = y + jnp.dot(lhs.astype(bf16), rhs.astype(bf16), preferred_element_type=f32)
            kd16 = jnp.concatenate(kds, axis=0)
            xm16 = jnp.concatenate([both(xms[0]), both(xms[1])], axis=0)
            upd = lax.dot_general(kd16.astype(bf16), xm16.astype(bf16), (((0,), (0,)), ((), ())),
                                  preferred_element_type=f32)
            for w in range(2):
                last = DEC_SEQ - 1 if w == 0 else 7
                dl = jnp.where(lo, jnp.exp(g[last:last + 1, 2 * pr:2 * pr + 1]),
                               jnp.exp(g[last:last + 1, 2 * pr + 1:2 * pr + 2]))
                snew = dl * spairs[w] + upd[:, w * LANE:(w + 1) * LANE]
                ssm_out[2 * p + w, 2 * pr] = snew[:, 0:C_HEADDIM]
                ssm_out[2 * p + w, 2 * pr + 1] = snew[:, C_HEADDIM:]
            cdl = jnp.where(lo, cpar[2:3, 2 * pr:2 * pr + 1], cpar[2:3, 2 * pr + 1:2 * pr + 2])
            ys.append(y + cdl * xp)
        yc = jnp.concatenate(ys, axis=-1)
        oc_ref[pl.ds(r, 8), :] = _c_post(yc, col(CB_CZ), cnw_ref[...]).astype(bf16)

        smd = small + dpar_ref[0:1, :]
        lsd = jax.nn.log_sigmoid(smd)
        dq = col(CB_DQ)
        dk = col(CB_DK)
        dv = col(CB_DV)
        do = col(CB_DO)
        dnw = dnw_ref[...]
        bcs = cumsum_seg(lsd)
        mrows = [mm_ref[pl.ds(2 * p + w, 1), :] for w in range(2)]
        mnew_rows = [jnp.zeros((1, D_HEADS), f32), jnp.zeros((1, D_HEADS), f32)]
        hlane = lax.broadcasted_iota(i32, (1, D_HEADS), 1)
        for hd in range(D_HEADS):
            hs = slice(hd * LANE, (hd + 1) * LANE)
            li, lf = 8 + hd, 12 + hd
            bcol = bcs[:, lf:lf + 1]
            icol = smd[:, li:li + 1]
            mprev = jnp.where(first, mrows[0][:, hd:hd + 1], mrows[1][:, hd:hd + 1])
            dcols = [jnp.where(seg[s], bcol - bcol[s:s + 1, :] + icol[s:s + 1, :], NEG_BIG) for s in range(8)]
            mx = dcols[0]
            for s in range(1, 8):
                mx = jnp.maximum(mx, dcols[s])
            inter = bcol + mprev
            mt = jnp.maximum(inter, mx)
            w_inter = jnp.exp(inter - mt)
            qh = dq[:, hs] * (D_DK ** -0.5)
            kh = dk[:, hs]
            vh = dv[:, hs]
            cmats = [mc_ref[2 * p + w, hd] for w in range(2)]
            nrows = [mn_ref[2 * p + w, pl.ds(hd, 1), :] for w in range(2)]
            num = w_inter * jnp.dot(both(qh).astype(bf16), jnp.concatenate(cmats, axis=0).astype(bf16),
                                    preferred_element_type=f32)
            den = w_inter * jnp.sum(qh * jnp.where(first, nrows[0], nrows[1]), axis=-1, keepdims=True)
            for s in range(8):
                w = jnp.where(seg[s], jnp.exp(dcols[s] - mt), 0.0)
                qk = jnp.sum(qh * kh[s:s + 1, :], axis=-1, keepdims=True) * w
                num = num + qk * vh[s:s + 1, :]
                den = den + qk
            hh = num / jnp.maximum(jnp.abs(den), jnp.exp(-mt))
            wk = jnp.exp(last_rows(bcol) - bcol + icol - last_rows(mt))
            wkk = wk * kh
            upd = lax.dot_general(wkk.astype(bf16), both(vh).astype(bf16), (((0,), (0,)), ((), ())),
                                  preferred_element_type=f32)
            for w in range(2):
                last = DEC_SEQ - 1 if w == 0 else 7
                mnew = mt[last:last + 1, :]
                decay = jnp.exp(bcol[last:last + 1, :] + mrows[w][:, hd:hd + 1] - mnew)
                mine = first if w == 0 else jnp.logical_not(first)
                mc_out[2 * p + w, hd] = decay * cmats[w] + upd[:, w * LANE:(w + 1) * LANE]
                mn_out[2 * p + w, pl.ds(hd, 1), :] = decay * nrows[w] + jnp.sum(jnp.where(mine, wkk, 0.0), axis=0,
                                                                                 keepdims=True)
                mnew_rows[w] = jnp.where(hlane == hd, mnew, mnew_rows[w])
            od_ref[pl.ds(r, 8), hs] = (jax.nn.sigmoid(do[:, hs]) * _rms(hh, dnw[:, hs])).astype(bf16)
        for w in range(2):
            mm_out[pl.ds(2 * p + w, 1), :] = mnew_rows[w]
        return carry

    lax.fori_loop(0, SB // 2, tile, 0)


_N_SAMPLE_IN = 19


def _sample_body_aliased(n_alias, *refs):
    _sample_body(*refs[:_N_SAMPLE_IN], *refs[_N_SAMPLE_IN + n_alias:])


def _sample_mixer(h, st_hgrn, st_ssm, st_conv, st_c, st_n, st_m, pa, pb, pc, pd, layer, brs, stacked):
    rows = SB * DEC_SEQ
    row0 = N_PROMPT // rows

    def blk(shape):
        nd = len(shape)
        return pl.BlockSpec((SB,) + shape, lambda i, nd=nd: (i,) + (0,) * nd)

    def blk_stacked(shape):
        nd = len(shape)
        return pl.BlockSpec((None, SB) + shape, lambda i, nd=nd: (layer, i) + (0,) * nd)

    def full(shape):
        nd = len(shape)
        return pl.BlockSpec(shape, lambda i, nd=nd: (0,) * nd)

    big = ((B_HEADS, LANE, LANE), (C_HEADS, C_DSTATE, C_HEADDIM), (D_HEADS, LANE, LANE))
    in_state_specs = [blk_stacked(big[0]), blk_stacked(big[1]), blk_stacked((C_CONV - 1, C_CONV_DIM)),
                      blk_stacked(big[2]), blk_stacked((D_HEADS, LANE)), blk_stacked((D_HEADS,))]
    out_state_specs = [blk_stacked(big[0]), blk_stacked(big[1]), blk((C_CONV - 1, C_CONV_DIM)), blk_stacked(big[2]),
                       blk((D_HEADS, LANE)), blk((D_HEADS,))]
    params = list(pa) + list(pb) + list(pc) + list(pd)
    br_spec = pl.BlockSpec((rows, BW), lambda i: (row0 + i, 0))
    br_shape = jax.ShapeDtypeStruct((N_TOK, BW), bf16)
    stacked_shape = lambda a: jax.ShapeDtypeStruct(a.shape, f32)
    per_layer = lambda a: jax.ShapeDtypeStruct(a.shape[1:], f32)
    inputs = [h, st_hgrn, st_ssm, st_conv, st_c, st_n, st_m] + params
    in_specs = [pl.BlockSpec((rows, MIX_W), lambda i: (row0 + i, 0))] + in_state_specs + [full(p.shape) for p in params]
    assert len(inputs) == _N_SAMPLE_IN
    alias_in = list(brs) + (list(stacked) if stacked is not None else [])
    alias_out = [0, 1, 2, 3] + ([5, 6, 8] if stacked is not None else [])
    aliases = {_N_SAMPLE_IN + k: o for k, o in enumerate(alias_out)}
    return pl.pallas_call(
        functools.partial(_sample_body_aliased, len(alias_in)),
        out_shape=(br_shape, br_shape, br_shape, br_shape, jax.ShapeDtypeStruct((N_SAMPLE, BW), f32),
                   stacked_shape(st_hgrn), stacked_shape(st_ssm),
                   per_layer(st_conv), stacked_shape(st_c), per_layer(st_n), per_layer(st_m)),
        grid=(DEC_BATCH // SB,),
        in_specs=in_specs + [pl.BlockSpec(memory_space=pl.ANY)] * len(alias_in),
        out_specs=(br_spec, br_spec, br_spec, br_spec, pl.BlockSpec((rows, BW), lambda i: (i, 0)))
        + tuple(out_state_specs),
        input_output_aliases=aliases,
        compiler_params=_cparams(("parallel",)),
        name="sample_mixer",
    )(*inputs, *alias_in)


def _merge_body(x_ref, nw_ref, ba_ref, bb_ref, bc_ref, bd_ref, wg_ref, wb_ref, o_ref, xn_ref, acc_ref):
    jc = pl.program_id(1)
    n = pl.program_id(2)

    @pl.when((jc == 0) & (n == 0))
    def _():
        xn_ref[...] = _rms(x_ref[...], nw_ref[...]).astype(bf16)

    gate = jax.nn.sigmoid(jnp.dot(xn_ref[...], wg_ref[...], preferred_element_type=f32))
    for k, br_ref in enumerate((ba_ref, bb_ref, bc_ref, bd_ref)):
        @pl.when(n == k)
        def _(br_ref=br_ref, k=k):
            contrib = gate * jnp.dot(br_ref[...], wb_ref[0], preferred_element_type=f32)
            if k == 0:
                acc_ref[...] = contrib
            else:
                acc_ref[...] = acc_ref[...] + contrib

    @pl.when(n == N_BRANCH - 1)
    def _():
        o_ref[...] = acc_ref[...].astype(bf16)


def _merge(x, nw, brs, w_gate, w_branch, tm, tn, layer):
    m = x.shape[0]
    ncol = D_MODEL // tn
    br_spec = pl.BlockSpec((tm, BW), lambda i, jc, n: (i, 0))
    return pl.pallas_call(
        _merge_body,
        out_shape=jax.ShapeDtypeStruct((m, D_MODEL), bf16),
        grid=(m // tm, ncol, N_BRANCH),
        in_specs=[pl.BlockSpec((tm, D_MODEL), lambda i, jc, n: (i, 0)),
                  pl.BlockSpec((1, D_MODEL), lambda i, jc, n: (0, 0)),
                  br_spec, br_spec, br_spec, br_spec,
                  pl.BlockSpec((D_MODEL, tn), lambda i, jc, n: (0, n * ncol + jc)),
                  pl.BlockSpec((None, 1, BW, tn), lambda i, jc, n: (layer, n, 0, jc))],
        out_specs=pl.BlockSpec((tm, tn), lambda i, jc, n: (i, jc)),
        scratch_shapes=[pltpu.VMEM((tm, D_MODEL), bf16), pltpu.VMEM((tm, tn), f32)],
        compiler_params=_cparams(("parallel", "arbitrary", "arbitrary")),
        name="merge",
    )(x, nw, *brs, w_gate, w_branch)


def _top16_rows(s, rid=None):
    if rid is None:
        rid = lax.broadcasted_iota(i32, s.shape, 0).astype(f32)
    out = []
    for _ in range(PEER_TOPK):
        m = jnp.max(s, axis=0, keepdims=True)
        am = jnp.min(jnp.where(s == m, rid, float(PEER_N)), axis=0, keepdims=True)
        out.append((m, am))
        s = jnp.where(rid == am, -jnp.inf, s)
    return out


def _collect16(pairs, tb):
    r16 = lax.broadcasted_iota(i32, (PEER_TOPK, tb), 0)
    v = jnp.zeros((PEER_TOPK, tb), f32)
    ix = jnp.zeros((PEER_TOPK, tb), f32)
    for k, (m, am) in enumerate(pairs):
        v = jnp.where(r16 == k, m, v)
        ix = jnp.where(r16 == k, am, ix)
    return v, ix.astype(i32)


def _peer_route_body(q_ref, keys_ref, ia_ref, ib_ref, gt_ref):
    tb = q_ref.shape[0]
    vals = []
    idxs = []
    for p in range(2):
        st = lax.dot_general(keys_ref[0, p], q_ref[:, p * LANE:(p + 1) * LANE], (((1,), (1,)), ((), ())),
                             preferred_element_type=f32)
        v, ix = _collect16(_top16_rows(st), tb)
        vals.append(v)
        idxs.append(ix)
    va, vb = vals
    half = PEER_TOPK // 2
    cand = jnp.concatenate([va[0:1, :] + vb] + [va[k:k + 1, :] + vb[0:half, :] for k in range(1, half)]
                           + [va[half:, :] + vb[0:1, :]], axis=0)
    r = lax.broadcasted_iota(i32, cand.shape, 0)
    r2 = r - PEER_TOPK
    mid = PEER_TOPK * (1 + (r2 >> 3)) + (r2 & (half - 1))
    pid = jnp.where(r < PEER_TOPK, r, jnp.where(r < PEER_TOPK + half * (half - 1), mid, (r - 8 * half) * PEER_TOPK))
    fs, pos = _collect16(_top16_rows(cand, pid.astype(f32)), tb)
    ka = pos >> 4
    kb = pos & (PEER_TOPK - 1)
    i1 = jnp.zeros((PEER_TOPK, tb), i32)
    i2 = jnp.zeros((PEER_TOPK, tb), i32)
    for j in range(PEER_TOPK):
        i1 = jnp.where(ka == j, idxs[0][j:j + 1, :], i1)
        i2 = jnp.where(kb == j, idxs[1][j:j + 1, :], i2)
    pe = jnp.exp(fs - fs[0:1, :])
    ia_ref[...] = i1
    ib_ref[...] = i2
    gt_ref[...] = pe / jnp.sum(pe, axis=0, keepdims=True)


def _peer_route(q, keys, tb, layer):
    m = q.shape[0]
    spec = pl.BlockSpec((PEER_TOPK, tb), lambda i, hd: (hd, i))
    nslot = PEER_HEADS * PEER_TOPK
    return pl.pallas_call(
        _peer_route_body,
        out_shape=(jax.ShapeDtypeStruct((nslot, m), i32), jax.ShapeDtypeStruct((nslot, m), i32),
                   jax.ShapeDtypeStruct((nslot, m), f32)),
        grid=(m // tb, PEER_HEADS),
        in_specs=[pl.BlockSpec((tb, 2 * LANE), lambda i, hd: (i, hd)),
                  pl.BlockSpec((None, 1, 2, PEER_NKEYS, LANE), lambda i, hd: (layer, hd, 0, 0, 0))],
        out_specs=(spec, spec, spec),
        compiler_params=_cparams(("parallel", "parallel")),
        name="peer_route",
    )(q, keys)


GATE_UNROLL = 64
RELAYOUT_TOKENS = 16


def _peer_gates_body(ia_ref, ib_ref, gt_ref, o_ref, a_s, b_s, g_s, gm_s):
    tg = ia_ref.shape[1]
    a_s[...] = ia_ref[...].astype(f32).T
    b_s[...] = ib_ref[...].astype(f32).T
    g_s[...] = gt_ref[...].T
    sub = lax.broadcasted_iota(i32, (LANE, LANE), 0).astype(f32)

    def step(r, carry):
        r8 = pl.multiple_of(r * GATE_UNROLL, GATE_UNROLL)
        a8 = a_s[pl.ds(r8, GATE_UNROLL), :]
        b8 = b_s[pl.ds(r8, GATE_UNROLL), :]
        g8 = g_s[pl.ds(r8, GATE_UNROLL), :]
        for u in range(GATE_UNROLL):
            at = jnp.where(sub == a8[u:u + 1, :], 1.0, 0.0).astype(bf16)
            bt = jnp.where(sub == b8[u:u + 1, :], g8[u:u + 1, :], 0.0).astype(bf16)
            gm = lax.dot_general(at, bt, (((1,), (1,)), ((), ())), preferred_element_type=f32)
            gm_s[r8 + u] = gm
        return carry

    lax.fori_loop(0, tg // GATE_UNROLL, step, 0)

    def relayout(c, carry):
        t0 = pl.multiple_of(c * RELAYOUT_TOKENS, RELAYOUT_TOKENS)
        for blk in range(PEER_NKEYS // 8):
            x = gm_s[pl.ds(t0, RELAYOUT_TOKENS), pl.ds(blk * 8, 8), :]
            y = jnp.swapaxes(x, 0, 1)
            for i in range(8):
                o_ref[pl.ds(t0, RELAYOUT_TOKENS), pl.ds((blk * 8 + i) * PEER_NKEYS, PEER_NKEYS)] = y[i].astype(bf16)
        return carry

    lax.fori_loop(0, tg // RELAYOUT_TOKENS, relayout, 0)


def _peer_gates(ia, ib, gt, tg):
    nslot, m = ia.shape
    spec = pl.BlockSpec((nslot, tg), lambda i: (0, i))
    return pl.pallas_call(
        _peer_gates_body,
        out_shape=jax.ShapeDtypeStruct((m, PEER_N), bf16),
        grid=(m // tg,),
        in_specs=[spec, spec, spec],
        out_specs=pl.BlockSpec((tg, PEER_N), lambda i: (i, 0)),
        scratch_shapes=[pltpu.VMEM((tg, nslot), f32)] * 3 + [pltpu.VMEM((tg, PEER_NKEYS, PEER_NKEYS), f32)],
        compiler_params=_cparams(("parallel",)),
        name="peer_gates",
    )(ia, ib, gt)


def _peer_experts_body(x_ref, nw_ref, u_ref, v_ref, g_ref, o_ref, xn_ref):
    @pl.when(pl.program_id(1) == 0)
    def _():
        x = x_ref[...]
        xn_ref[...] = _rms(x, nw_ref[...]).astype(bf16)
        o_ref[...] = x

    hmat = lax.dot_general(xn_ref[...], u_ref[...], (((1,), (1,)), ((), ())), preferred_element_type=f32)
    w = (_gelu(hmat) * g_ref[...].astype(f32)).astype(bf16)
    o_ref[...] += jnp.dot(w, v_ref[...], preferred_element_type=f32)


def _peer_experts(x, nw, u, v, g, tb, eb, layer):
    m = x.shape[0]
    return pl.pallas_call(
        _peer_experts_body,
        out_shape=jax.ShapeDtypeStruct((m, D_MODEL), f32),
        grid=(m // tb, PEER_N // eb),
        in_specs=[pl.BlockSpec((tb, D_MODEL), lambda i, j: (i, 0), pipeline_mode=pl.Buffered(1)),
                  pl.BlockSpec((1, D_MODEL), lambda i, j: (0, 0)),
                  pl.BlockSpec((None, eb, D_MODEL), lambda i, j: (layer, j, 0)),
                  pl.BlockSpec((None, eb, D_MODEL), lambda i, j: (layer, j, 0)),
                  pl.BlockSpec((tb, eb), lambda i, j: (i, j))],
        out_specs=pl.BlockSpec((tb, D_MODEL), lambda i, j: (i, 0), pipeline_mode=pl.Buffered(1)),
        scratch_shapes=[pltpu.VMEM((tb, D_MODEL), bf16)],
        compiler_params=_cparams(("parallel", "arbitrary")),
        name="peer_experts",
    )(x, nw, u, v, g)


_IN_OFFS = tuple(int(v) for v in np.cumsum((0,) + IN_SPLITS))
_O_CDT, _O_DQ, _O_DIG, _O_GATES, _O_END = _IN_OFFS[8], _IN_OFFS[9], _IN_OFFS[13], _IN_OFFS[15], _IN_OFFS[16]
PREP_ROWS = 128


def _prep_w_in_body(w_ref, om_ref, og_ref):
    rows = w_ref.shape[0]
    om_ref[:, 0:_O_CDT] = w_ref[:, 0:_O_CDT].astype(bf16)
    om_ref[:, _O_CDT:SMALL_COL] = w_ref[:, _O_DQ:_O_DIG].astype(bf16)
    small = jnp.concatenate([w_ref[:, _O_CDT:_O_DQ], w_ref[:, _O_DIG:_O_GATES],
                             jnp.zeros((rows, LANE - 16), w_ref.dtype)], axis=-1)
    om_ref[:, SMALL_COL:SMALL_COL + LANE] = small.astype(bf16)
    om_ref[:, SMALL_COL + LANE:MIX_W] = jnp.zeros((rows, MIX_W - SMALL_COL - LANE), bf16)
    og_ref[...] = w_ref[:, _O_GATES:_O_END].astype(bf16)


def _prep_w_in(w, layer):
    _, k, n = w.shape
    ng = _O_END - _O_GATES
    return pl.pallas_call(
        _prep_w_in_body,
        out_shape=(jax.ShapeDtypeStruct((k, MIX_W), bf16), jax.ShapeDtypeStruct((k, ng), bf16)),
        grid=(k // PREP_ROWS,),
        in_specs=[pl.BlockSpec((None, PREP_ROWS, n), lambda i: (layer, i, 0))],
        out_specs=(pl.BlockSpec((PREP_ROWS, MIX_W), lambda i: (i, 0)), pl.BlockSpec((PREP_ROWS, ng), lambda i: (i, 0))),
        compiler_params=_cparams(("parallel",)),
        name="prep_w_in",
    )(w)


def _lane_row(vals, start):
    row = jnp.zeros((LANE,), f32)
    return row.at[start:start + vals.shape[0]].set(vals)


def kernel(x_prompt, x_sample, state_hgrn, state_ssm, state_conv, state_mlstm_c, state_mlstm_n, state_mlstm_m, norm1_w, w_in, a_ln_w, a_ln_b, a_ws, a_bs, b_lb_logits, b_norm_w, c_conv_w, c_conv_b, c_dt_bias, c_a_log, c_d, c_norm_w, d_ig_b, d_fg_b, d_norm_w, w_branch, w_out, norm2_w, peer_wq, peer_keys, peer_u, peer_v, final_norm_w):
    x = jnp.concatenate([x_prompt.reshape(N_PROMPT, D_MODEL), x_sample.reshape(N_SAMPLE, D_MODEL)], axis=0)
    lbs = jax.nn.softmax(b_lb_logits.astype(f32), axis=0)
    lbs = jnp.cumsum(lbs, axis=0) - lbs[0]
    zeros8 = jnp.zeros((8, LANE), f32)
    news_p = []
    news_s = []
    stacked_s = None
    w_in_b, w_branch_b, w_out_b, wq_b = (w.astype(bf16) for w in (w_in, w_branch, w_out, peer_wq))
    keys_b, u_b, v_b = (w.astype(bf16) for w in (peer_keys, peer_u, peer_v))
    for l in range(DEPTH):
        w_mix, w_gate = _prep_w_in(w_in_b, l)
        h = _norm_mm(x, norm1_w[l][None, :], w_mix, f32, 1088, 1024)

        lnw = a_ln_w[l][None, :]
        lnb = a_ln_b[l][None, :]
        lb = lbs[l][None, :]
        bnw = b_norm_w[l][None, :]
        cw = c_conv_w[l]
        cb = c_conv_b[l][None, :]
        cpar = zeros8.at[0].set(_lane_row(c_dt_bias[l], 0)).at[1].set(_lane_row(c_a_log[l], 0)).at[2].set(_lane_row(c_d[l], 0))
        cnw = c_norm_w[l][None, :]
        dpar = zeros8.at[0].set(_lane_row(d_ig_b[l], 8) + _lane_row(d_fg_b[l], 12))
        dnw = d_norm_w[l][None, :]

        bs_full = jnp.repeat(a_bs[l].T, LANE, axis=1)
        br_a = _prompt_a(h, lnw, lnb, a_ws[l], bs_full)
        br_b, hg_p = _prompt_b(h, lb, bnw)
        br_c, ssm_p, cv_p = _prompt_c(h, cw, cb, cpar, cnw)
        br_d, mc_p, mn_p, mm_p = _prompt_d(h, dpar, dnw)
        news_p.append((hg_p, ssm_p, cv_p, mc_p, mn_p[:, :D_HEADS, :], mm_p[:, :D_HEADS, 0]))

        w4 = jnp.tril(a_ws[l][:, :DEC_SEQ, :DEC_SEQ])
        w8 = jnp.zeros((A_GROUPS, 8, 8), f32).at[:, :4, :4].set(w4).at[:, 4:, 4:].set(w4)
        ws8 = jnp.repeat(jnp.transpose(w8, (2, 1, 0)), LANE, axis=2)
        bs8 = jnp.repeat(jnp.tile(a_bs[l][:, :DEC_SEQ], (1, 2)).T, LANE, axis=1)
        outs = _sample_mixer(h, state_hgrn, state_ssm, state_conv, state_mlstm_c, state_mlstm_n, state_mlstm_m,
                             (lnw, lnb, ws8, bs8), (lb, bnw), (cw, cb, cpar, cnw), (dpar, dnw), l,
                             (br_a, br_b, br_c, br_d), stacked_s)
        brs = outs[:4]
        stacked_s = (outs[5], outs[6], outs[8])
        news_s.append((outs[7], outs[9], outs[10], outs[4].reshape(DEC_BATCH, DEC_SEQ, BW)))

        mixin = _merge(x, norm1_w[l][None, :], brs, w_gate, w_branch_b, 1088, 1024, l)
        x = _mm_res(mixin, w_out_b, x, 1088, 512, l)

        q = _norm_mm(x, norm2_w[l][None, :], wq_b, bf16, 1088, 512, l)
        ia, ib, gt = _peer_route(q, keys_b, 512, l)
        g = _peer_gates(ia, ib, gt, 128)
        x = _peer_experts(x, norm2_w[l][None, :], u_b, v_b, g, 1088, 1024, l)

    y_p, y_s = _final_norm(x, final_norm_w[None, :])
    y_prompt = y_p.reshape(BATCH, SEQ, D_MODEL)
    y_sample = y_s.reshape(DEC_BATCH, DEC_SEQ, D_MODEL)
    stack = lambda news, k: jnp.stack([n[k] for n in news], axis=0)
    hgrn_s, ssm_s, mc_s = stacked_s
    conv_s, mn_s, mm_s, chunk_v_s = (stack(news_s, k) for k in range(4))
    return ((y_prompt, y_sample) + tuple(stack(news_p, k) for k in range(6))
            + (hgrn_s, ssm_s, conv_s, mc_s, mn_s, mm_s, chunk_v_s))
```

```python
import functools

import jax
import jax.numpy as jnp
import numpy as np
from jax import lax
from jax.experimental import pallas as pl
from jax.experimental.pallas import tpu as pltpu

f32 = jnp.float32
bf16 = jnp.bfloat16
i32 = jnp.int32
HI = lax.Precision.HIGHEST

D_MODEL = 2048
BATCH = 4
SEQ = 2048
DEPTH = 2
DEC_BATCH = 128
DEC_SEQ = 4
N_BRANCH = 4
BW = 512
A_GROUPS = 4
A_CHUNK = 128
B_HEADS = 4
LB_FLOOR = 1e-30
C_HEADDIM = 64
C_HEADS = 8
C_GROUPS = 2
C_DSTATE = 128
C_CONV = 4
C_CONV_DIM = 1024
D_HEADS = 4
D_DK = 128
NEG_BIG = -1e30
IN_SPLITS = (512, 512, 512, 512, 512, 512, 512, 1024, 8, 512, 512, 512, 512, 4, 4, 8192)
PEER_NKEYS = 128
PEER_N = PEER_NKEYS * PEER_NKEYS
PEER_HEADS = 8
PEER_TOPK = 16
EPS = 1e-6
INV_SQRT2 = 0.7071067811865476

N_PROMPT = BATCH * SEQ
N_SAMPLE = DEC_BATCH * DEC_SEQ
N_TOK = N_PROMPT + N_SAMPLE

CB_AU, CB_AV, CB_BQ, CB_BF, CB_BI, CB_BG, CB_CZ, CB_CX0, CB_CX1, CB_DQ, CB_DK, CB_DV, CB_DO = range(13)
SMALL_COL = 13 * 512
MIX_W = 14 * 512
LANE = 128
CHUNK = 128
NCHUNK = SEQ // CHUNK
SUB = 16
SB = 8
VMEM_LIMIT = 56 * 1024 * 1024


def _gelu(x):
    return 0.5 * x * (1.0 + lax.erf(x * INV_SQRT2))


def _rms(x, w):
    ms = jnp.mean(x * x, axis=-1, keepdims=True)
    return x * lax.rsqrt(ms + EPS) * w


def _tri(n):
    r = lax.broadcasted_iota(i32, (n, n), 0)
    c = lax.broadcasted_iota(i32, (n, n), 1)
    return r >= c


def _cumsum_rows_small(x, n):
    row = lax.broadcasted_iota(i32, (n, 1), 0)
    acc = jnp.zeros_like(x)
    for s in range(n):
        acc = acc + jnp.where(row >= s, x[s:s + 1, :], 0.0)
    return acc


def _row_to_col(row, eye):
    return jnp.sum(jnp.where(eye, row, 0.0), axis=-1, keepdims=True)


def _cparams(sem, vmem=VMEM_LIMIT):
    return pltpu.CompilerParams(dimension_semantics=sem, vmem_limit_bytes=vmem)


def _norm_mm_body(x_ref, nw_ref, w_ref, o_ref, xn_ref):
    @pl.when(pl.program_id(1) == 0)
    def _():
        xn_ref[...] = _rms(x_ref[...], nw_ref[...]).astype(bf16)

    o_ref[...] = jnp.dot(xn_ref[...], w_ref[...], preferred_element_type=f32).astype(o_ref.dtype)


def _w_spec(w, layer, block, index_map):
    if layer is None:
        return pl.BlockSpec(block, index_map)
    return pl.BlockSpec((None,) + block, lambda *g: (layer,) + index_map(*g))


def _norm_mm(x, nw, w, out_dtype, tm, tn, layer=None):
    m, k = x.shape
    n = w.shape[-1]
    return pl.pallas_call(
        _norm_mm_body,
        out_shape=jax.ShapeDtypeStruct((m, n), out_dtype),
        grid=(m // tm, n // tn),
        in_specs=[pl.BlockSpec((tm, k), lambda i, j: (i, 0)),
                  pl.BlockSpec((1, k), lambda i, j: (0, 0)),
                  _w_spec(w, layer, (k, tn), lambda i, j: (0, j))],
        out_specs=pl.BlockSpec((tm, tn), lambda i, j: (i, j)),
        scratch_shapes=[pltpu.VMEM((tm, k), bf16)],
        compiler_params=_cparams(("parallel", "arbitrary")),
        name="norm_mm",
    )(x, nw, w)


def _mm_res_body(a_ref, w_ref, r_ref, o_ref):
    o_ref[...] = r_ref[...] + jnp.dot(a_ref[...], w_ref[...], preferred_element_type=f32)


def _mm_res(a, w, res, tm, tn, layer=None):
    m, k = a.shape
    n = w.shape[-1]
    return pl.pallas_call(
        _mm_res_body,
        out_shape=jax.ShapeDtypeStruct((m, n), f32),
        grid=(m // tm, n // tn),
        in_specs=[pl.BlockSpec((tm, k), lambda i, j: (i, 0)),
                  _w_spec(w, layer, (k, tn), lambda i, j: (0, j)),
                  pl.BlockSpec((tm, tn), lambda i, j: (i, j))],
        out_specs=pl.BlockSpec((tm, tn), lambda i, j: (i, j)),
        compiler_params=_cparams(("parallel", "arbitrary")),
        name="mm_res",
    )(a, w, res)


def _final_norm_body(x_ref, w_ref, op_ref, os_ref):
    y = _rms(x_ref[...], w_ref[...])
    i = pl.program_id(0)

    @pl.when(i < N_PROMPT // N_SAMPLE)
    def _():
        op_ref[...] = y

    @pl.when(i == N_PROMPT // N_SAMPLE)
    def _():
        os_ref[...] = y


def _final_norm(x, w):
    m, k = x.shape
    tm = N_SAMPLE
    last_p = N_PROMPT // tm - 1
    return pl.pallas_call(
        _final_norm_body,
        out_shape=(jax.ShapeDtypeStruct((N_PROMPT, k), f32), jax.ShapeDtypeStruct((N_SAMPLE, k), f32)),
        grid=(m // tm,),
        in_specs=[pl.BlockSpec((tm, k), lambda i: (i, 0)), pl.BlockSpec((1, k), lambda i: (0, 0))],
        out_specs=(pl.BlockSpec((tm, k), lambda i: (jnp.minimum(i, last_p), 0)),
                   pl.BlockSpec((tm, k), lambda i: (0, 0))),
        compiler_params=_cparams(("arbitrary",)),
        name="final_norm",
    )(x, w)


def _a_uv(au, av, lnw, lnb):
    u = _gelu(au)
    g = _gelu(av)
    xc = g - jnp.mean(g, axis=-1, keepdims=True)
    var = jnp.mean(xc * xc, axis=-1, keepdims=True)
    v = xc * lax.rsqrt(var + EPS) * lnw + lnb
    return u, v


def _b_pre(bq, bf_, lb):
    q = bq * jax.nn.sigmoid(bq)
    logf = jnp.logaddexp(jnp.log(jnp.maximum(lb, LB_FLOOR)), jnp.log1p(-lb) + jax.nn.log_sigmoid(bf_))
    kb = (1.0 - lb) * jax.nn.sigmoid(-bf_)
    return q, kb, logf


def _hgrn_chunks(chains, n, eye):
    row = lax.broadcasted_iota(i32, (n, 1), 0)
    qs, ks, vs, gs, ss = [], [], [], [], []
    for q, k, v, gl, s_mat, valid in chains:
        if valid is not None:
            gl = jnp.where(valid, gl, 0.0)
            k = jnp.where(valid, k, 0.0)
        qs.append(q)
        ks.append(k)
        vs.append(v)
        gs.append(_cumsum_rows_small(gl, n))
        ss.append(s_mat)
    nc = len(chains)
    os_ = [jnp.dot((qs[c] * jnp.exp(gs[c])).astype(bf16), ss[c].astype(bf16), preferred_element_type=f32)
           for c in range(nc)]
    upds = []
    for c in range(nc):
        k_dec = ks[c] * jnp.exp(gs[c][n - 1:n, :] - gs[c])
        upds.append(lax.dot_general(k_dec.astype(bf16), vs[c].astype(bf16), (((0,), (0,)), ((), ())),
                                    preferred_element_type=f32))
    for s in range(n):
        m = row >= s
        for c in range(nc):
            g = gs[c]
            p = jnp.where(m, qs[c] * ks[c][s:s + 1, :] * jnp.exp(g - g[s:s + 1, :]), 0.0)
            os_[c] = os_[c] + jnp.sum(p, axis=-1, keepdims=True) * vs[c][s:s + 1, :]
    out = []
    for c in range(nc):
        s_new = jnp.exp(_row_to_col(gs[c][n - 1:n, :], eye)) * ss[c] + upds[c]
        out.append((os_[c], s_new))
    return out


def _b_post(o, bg, nw):
    return _rms(o, nw) * (bg * jax.nn.sigmoid(bg))


def _hrow(b, c):
    return b * NCHUNK + c


def _h_spec(colblk):
    return pl.BlockSpec((CHUNK, BW), lambda b, c, cb=colblk: (_hrow(b, c), cb))


_SMALL_SPEC = pl.BlockSpec((CHUNK, LANE), lambda b, c: (_hrow(b, c), SMALL_COL // LANE))
_BR_SPEC = pl.BlockSpec((CHUNK, BW), lambda b, c: (_hrow(b, c), 0))


def _full_spec(shape):
    nd = len(shape)
    return pl.BlockSpec(shape, lambda b, c, nd=nd: (0,) * nd)


def _pa_body(au_ref, av_ref, lnw_ref, lnb_ref, ws_ref, bs_ref, o_ref):
    u, v = _a_uv(au_ref[...], av_ref[...], lnw_ref[...], lnb_ref[...])
    tri = _tri(CHUNK)
    vb = v.astype(bf16)
    parts = []
    for g in range(A_GROUPS):
        w = jnp.where(tri, ws_ref[g], 0.0).astype(bf16)
        parts.append(jnp.dot(w, vb[:, g * LANE:(g + 1) * LANE], preferred_element_type=f32))
    sp = jnp.concatenate(parts, axis=-1) + bs_ref[...]
    o_ref[...] = (u * sp).astype(bf16)


def _prompt_a(h, lnw, lnb, ws, bs_full):
    return pl.pallas_call(
        _pa_body,
        out_shape=jax.ShapeDtypeStruct((N_TOK, BW), bf16),
        grid=(BATCH, NCHUNK),
        in_specs=[_h_spec(CB_AU), _h_spec(CB_AV), _full_spec((1, BW)), _full_spec((1, BW)),
                  _full_spec((A_GROUPS, A_CHUNK, A_CHUNK)), _full_spec((A_CHUNK, BW))],
        out_specs=_BR_SPEC,
        compiler_params=_cparams(("parallel", "parallel")),
        name="prompt_gmlp",
    )(h, h, lnw, lnb, ws, bs_full)


def _pb_body(bq_ref, bf_ref, bi_ref, bg_ref, lb_ref, nw_ref, o_ref, st_ref, s_ref, q_s, k_s, v_s, g_s, o_s):
    c = pl.program_id(1)

    @pl.when(c == 0)
    def _():
        s_ref[...] = jnp.zeros_like(s_ref)

    q, kb, logf = _b_pre(bq_ref[...], bf_ref[...], lb_ref[...])
    q_s[...] = q
    k_s[...] = kb
    v_s[...] = bi_ref[...]
    g_s[...] = logf
    eye = lax.broadcasted_iota(i32, (LANE, LANE), 0) == lax.broadcasted_iota(i32, (LANE, LANE), 1)
    def sub(j, carry):
        r = pl.multiple_of(j * SUB, SUB)
        heads = [slice(hd * LANE, (hd + 1) * LANE) for hd in range(B_HEADS)]
        chains = [(q_s[pl.ds(r, SUB), hs], k_s[pl.ds(r, SUB), hs], v_s[pl.ds(r, SUB), hs], g_s[pl.ds(r, SUB), hs],
                   s_ref[hd], None) for hd, hs in enumerate(heads)]
        for hd, (o, s_new) in enumerate(_hgrn_chunks(chains, SUB, eye)):
            o_s[pl.ds(r, SUB), heads[hd]] = o
            s_ref[hd] = s_new
        return carry

    lax.fori_loop(0, CHUNK // SUB, sub, 0, unroll=True)
    bg = bg_ref[...]
    nw = nw_ref[...]
    for hd in range(B_HEADS):
        hs = slice(hd * LANE, (hd + 1) * LANE)
        o_ref[:, hs] = _b_post(o_s[:, hs], bg[:, hs], nw).astype(bf16)

    @pl.when(c == NCHUNK - 1)
    def _():
        st_ref[0] = s_ref[...]


def _prompt_b(h, lb, nw):
    return pl.pallas_call(
        _pb_body,
        out_shape=(jax.ShapeDtypeStruct((N_TOK, BW), bf16),
                   jax.ShapeDtypeStruct((BATCH, B_HEADS, LANE, LANE), f32)),
        grid=(BATCH, NCHUNK),
        in_specs=[_h_spec(CB_BQ), _h_spec(CB_BF), _h_spec(CB_BI), _h_spec(CB_BG),
                  _full_spec((1, BW)), _full_spec((1, LANE))],
        out_specs=(_BR_SPEC, pl.BlockSpec((1, B_HEADS, LANE, LANE), lambda b, c: (b, 0, 0, 0))),
        scratch_shapes=[pltpu.VMEM((B_HEADS, LANE, LANE), f32)] + [pltpu.VMEM((CHUNK, BW), f32)] * 5,
        compiler_params=_cparams(("parallel", "arbitrary")),
        name="prompt_hgrn",
    )(h, h, h, h, lb, nw)


def _c_conv_silu(win0, win1, win2, win3, cw_ref, cb_ref):
    y = cb_ref[...] + win0 * cw_ref[0:1, :] + win1 * cw_ref[1:2, :] + win2 * cw_ref[2:3, :] + win3 * cw_ref[3:4, :]
    return y * jax.nn.sigmoid(y)


def _c_post(yc, cz, nw):
    y = yc * (cz * jax.nn.sigmoid(cz))
    gw = BW // C_GROUPS
    parts = [_rms(y[:, g * gw:(g + 1) * gw], nw[:, g * gw:(g + 1) * gw]) for g in range(C_GROUPS)]
    return jnp.concatenate(parts, axis=-1)


def _pc_body(cz_ref, cx0_ref, cx1_ref, sm_ref, cw_ref, cb_ref, par_ref, nw_ref,
             o_ref, st_ref, cv_ref, xpad, sp_ref):
    c = pl.program_id(1)

    @pl.when(c == 0)
    def _():
        xpad[pl.ds(0, 8), :] = jnp.zeros((8, C_CONV_DIM), f32)
        sp_ref[...] = jnp.zeros_like(sp_ref)

    xpad[pl.ds(8, CHUNK), 0:BW] = cx0_ref[...]
    xpad[pl.ds(8, CHUNK), BW:2 * BW] = cx1_ref[...]
    xbc = _c_conv_silu(xpad[pl.ds(5, CHUNK), :], xpad[pl.ds(6, CHUNK), :], xpad[pl.ds(7, CHUNK), :],
                       xpad[pl.ds(8, CHUNK), :], cw_ref, cb_ref)

    @pl.when(c == NCHUNK - 1)
    def _():
        cv_ref[0] = xpad[pl.ds(CHUNK + 5, 3), :]

    xpad[pl.ds(0, 8), :] = xpad[pl.ds(CHUNK, 8), :]

    xs = xbc[:, 0:BW]
    bm = xbc[:, BW:BW + 2 * C_DSTATE]
    cm = xbc[:, BW + 2 * C_DSTATE:]
    par = par_ref[...]
    dt = jax.nn.softplus(sm_ref[...] + par[0:1, :])
    gl = dt * (-jnp.exp(par[1:2, :]))
    tri = _tri(CHUNK)
    g = jnp.dot(tri.astype(f32), gl, precision=HI, preferred_element_type=f32)
    gt = g.T
    dtt = dt.T
    lane = lax.broadcasted_iota(i32, (1, LANE), 1)
    lo = lane < C_HEADDIM
    cbs = []
    for grp in range(C_GROUPS):
        cg = cm[:, grp * C_DSTATE:(grp + 1) * C_DSTATE].astype(bf16)
        bg = bm[:, grp * C_DSTATE:(grp + 1) * C_DSTATE].astype(bf16)
        cbs.append(lax.dot_general(cg, bg, (((1,), (1,)), ((), ())), preferred_element_type=f32))
    ys = []
    for pr in range(C_HEADS // 2):
        xp = xs[:, pr * LANE:(pr + 1) * LANE]
        sp = sp_ref[pr]
        y = jnp.zeros((CHUNK, LANE), f32)
        upd = jnp.zeros((C_DSTATE, LANE), f32)
        dl = jnp.zeros((1, LANE), f32)
        cdl = jnp.zeros((1, LANE), f32)
        for sub in range(2):
            hd = 2 * pr + sub
            grp = hd // (C_HEADS // C_GROUPS)
            lm = lo if sub == 0 else jnp.logical_not(lo)
            col = g[:, hd:hd + 1]
            g_last = col[CHUNK - 1:CHUNK, :]
            dec = jnp.exp(col - gt[hd:hd + 1, :])
            sc = jnp.where(tri, cbs[grp] * dec * dtt[hd:hd + 1, :], 0.0)
            xm = jnp.where(lm, xp, 0.0).astype(bf16)
            cg = cm[:, grp * C_DSTATE:(grp + 1) * C_DSTATE]
            bg = bm[:, grp * C_DSTATE:(grp + 1) * C_DSTATE]
            y = y + jnp.dot(sc.astype(bf16), xm, preferred_element_type=f32)
            y = y + jnp.dot((cg * jnp.exp(col)).astype(bf16), jnp.where(lm, sp, 0.0).astype(bf16),
                            preferred_element_type=f32)
            kd = bg * (dt[:, hd:hd + 1] * jnp.exp(g_last - col))
            upd = upd + lax.dot_general(kd.astype(bf16), xm, (((0,), (0,)), ((), ())), preferred_element_type=f32)
            dl = jnp.where(lm, jnp.exp(g_last), dl)
            cdl = jnp.where(lm, par[2:3, hd:hd + 1], cdl)
        sp_ref[pr] = dl * sp + upd
        ys.append(y + cdl * xp)
    yc = jnp.concatenate(ys, axis=-1)
    o_ref[...] = _c_post(yc, cz_ref[...], nw_ref[...]).astype(bf16)

    @pl.when(c == NCHUNK - 1)
    def _():
        for pr in range(C_HEADS // 2):
            st_ref[0, 2 * pr] = sp_ref[pr][:, 0:C_HEADDIM]
            st_ref[0, 2 * pr + 1] = sp_ref[pr][:, C_HEADDIM:]


def _prompt_c(h, cw, cb, par, nw):
    return pl.pallas_call(
        _pc_body,
        out_shape=(jax.ShapeDtypeStruct((N_TOK, BW), bf16),
                   jax.ShapeDtypeStruct((BATCH, C_HEADS, C_DSTATE, C_HEADDIM), f32),
                   jax.ShapeDtypeStruct((BATCH, C_CONV - 1, C_CONV_DIM), f32)),
        grid=(BATCH, NCHUNK),
        in_specs=[_h_spec(CB_CZ), _h_spec(CB_CX0), _h_spec(CB_CX1), _SMALL_SPEC,
                  _full_spec((C_CONV, C_CONV_DIM)), _full_spec((1, C_CONV_DIM)), _full_spec((8, LANE)),
                  _full_spec((1, BW))],
        out_specs=(_BR_SPEC,
                   pl.BlockSpec((1, C_HEADS, C_DSTATE, C_HEADDIM), lambda b, c: (b, 0, 0, 0)),
                   pl.BlockSpec((1, C_CONV - 1, C_CONV_DIM), lambda b, c: (b, 0, 0))),
        scratch_shapes=[pltpu.VMEM((CHUNK + 8, C_CONV_DIM), f32), pltpu.VMEM((C_HEADS // 2, C_DSTATE, LANE), f32)],
        compiler_params=_cparams(("parallel", "arbitrary")),
        name="prompt_ssd",
    )(h, h, h, h, cw, cb, par, nw)


def _pd_body(dq_ref, dk_ref, dv_ref, do_ref, sm_ref, par_ref, nw_ref,
             o_ref, c_out, n_out, m_out, c_ref, n_ref, m_ref):
    c = pl.program_id(1)

    @pl.when(c == 0)
    def _():
        c_ref[...] = jnp.zeros_like(c_ref)
        n_ref[...] = jnp.zeros_like(n_ref)
        m_ref[...] = jnp.zeros_like(m_ref)

    sm = sm_ref[...] + par_ref[0:1, :]
    ls = jax.nn.log_sigmoid(sm)
    tri = _tri(CHUNK)
    bc = jnp.dot(tri.astype(f32), ls, precision=HI, preferred_element_type=f32)
    bct = bc.T
    smt = sm.T
    nw = nw_ref[...]
    for hd in range(D_HEADS):
        hs = slice(hd * LANE, (hd + 1) * LANE)
        li, lf = 8 + hd, 12 + hd
        bcol = bc[:, lf:lf + 1]
        brow = bct[lf:lf + 1, :]
        irow = smt[li:li + 1, :]
        icol = sm[:, li:li + 1]
        mprev = m_ref[hd:hd + 1, 0:1]
        dmat = jnp.where(tri, bcol - brow + irow, NEG_BIG)
        inter = bcol + mprev
        mt = jnp.maximum(inter, jnp.max(dmat, axis=-1, keepdims=True))
        w_intra = jnp.where(tri, jnp.exp(dmat - mt), 0.0)
        w_inter = jnp.exp(inter - mt)
        qh = dq_ref[:, hs] * (D_DK ** -0.5)
        kh = dk_ref[:, hs]
        vh = dv_ref[:, hs]
        qb = qh.astype(bf16)
        qk = lax.dot_general(qb, kh.astype(bf16), (((1,), (1,)), ((), ())), preferred_element_type=f32) * w_intra
        num = w_inter * jnp.dot(qb, c_ref[hd].astype(bf16), preferred_element_type=f32)
        num = num + jnp.dot(qk.astype(bf16), vh.astype(bf16), preferred_element_type=f32)
        den = w_inter * jnp.sum(qh * n_ref[hd:hd + 1, :], axis=-1, keepdims=True) + jnp.sum(qk, axis=-1, keepdims=True)
        hh = num / jnp.maximum(jnp.abs(den), jnp.exp(-mt))
        mnew = mt[CHUNK - 1:CHUNK, :]
        blast = bcol[CHUNK - 1:CHUNK, :]
        wk = jnp.exp(blast - bcol + icol - mnew)
        decay = jnp.exp(blast + mprev - mnew)
        wkk = wk * kh
        c_ref[hd] = decay * c_ref[hd] + lax.dot_general(wkk.astype(bf16), vh.astype(bf16), (((0,), (0,)), ((), ())),
                                                        preferred_element_type=f32)
        n_ref[hd:hd + 1, :] = decay * n_ref[hd:hd + 1, :] + jnp.sum(wkk, axis=0, keepdims=True)
        m_ref[hd:hd + 1, :] = jnp.broadcast_to(mnew, (1, LANE))
        o_ref[:, hs] = (jax.nn.sigmoid(do_ref[:, hs]) * _rms(hh, nw[:, hs])).astype(bf16)

    @pl.when(c == NCHUNK - 1)
    def _():
        c_out[0] = c_ref[...]
        n_out[0] = n_ref[...]
        m_out[0] = m_ref[...]


def _prompt_d(h, par, nw):
    return pl.pallas_call(
        _pd_body,
        out_shape=(jax.ShapeDtypeStruct((N_TOK, BW), bf16),
                   jax.ShapeDtypeStruct((BATCH, D_HEADS, LANE, LANE), f32),
                   jax.ShapeDtypeStruct((BATCH, 8, LANE), f32),
                   jax.ShapeDtypeStruct((BATCH, 8, LANE), f32)),
        grid=(BATCH, NCHUNK),
        in_specs=[_h_spec(CB_DQ), _h_spec(CB_DK), _h_spec(CB_DV), _h_spec(CB_DO), _SMALL_SPEC,
                  _full_spec((8, LANE)), _full_spec((1, BW))],
        out_specs=(_BR_SPEC,
                   pl.BlockSpec((1, D_HEADS, LANE, LANE), lambda b, c: (b, 0, 0, 0)),
                   pl.BlockSpec((1, 8, LANE), lambda b, c: (b, 0, 0)),
                   pl.BlockSpec((1, 8, LANE), lambda b, c: (b, 0, 0))),
        scratch_shapes=[pltpu.VMEM((D_HEADS, LANE, LANE), f32), pltpu.VMEM((8, LANE), f32), pltpu.VMEM((8, LANE), f32)],
        compiler_params=_cparams(("parallel", "arbitrary")),
        name="prompt_mlstm",
    )(h, h, h, h, h, par, nw)


def _sample_body(h_ref, hg_ref, ssm_ref, cv_ref, mc_ref, mn_ref, mm_ref,
                 lnw_ref, lnb_ref, ws8_ref, bs8_ref, lb_ref, bnw_ref,
                 cw_ref, cb_ref, cpar_ref, cnw_ref, dpar_ref, dnw_ref,
                 oa_ref, ob_ref, oc_ref, od_ref, chv_ref,
                 hg_out, ssm_out, cv_out, mc_out, mn_out, mm_out):
    row = lax.broadcasted_iota(i32, (8, 1), 0)
    first = row < DEC_SEQ
    tpos = row & (DEC_SEQ - 1)
    seg = [((row >= s) & first) if s < DEC_SEQ else (row >= s) for s in range(8)]

    def cumsum_seg(x):
        acc = jnp.zeros_like(x)
        for s in range(8):
            acc = acc + jnp.where(seg[s], x[s:s + 1, :], 0.0)
        return acc

    def both(x):
        return jnp.concatenate([jnp.where(first, x, 0.0), jnp.where(first, 0.0, x)], axis=-1)

    def last_rows(x):
        return jnp.where(first, x[DEC_SEQ - 1:DEC_SEQ, :], x[7:8, :])
    eye = lax.broadcasted_iota(i32, (LANE, LANE), 0) == lax.broadcasted_iota(i32, (LANE, LANE), 1)
    lane = lax.broadcasted_iota(i32, (1, LANE), 1)
    lo = lane < C_HEADDIM

    def tile(p, carry):
        r = pl.multiple_of(p * 8, 8)

        def col(blk, width=BW):
            return h_ref[pl.ds(r, 8), blk * BW:blk * BW + width]

        small = h_ref[pl.ds(r, 8), SMALL_COL:SMALL_COL + LANE]

        u, v = _a_uv(col(CB_AU), col(CB_AV), lnw_ref[...], lnb_ref[...])
        chv_ref[pl.ds(r, 8), :] = v
        sp = bs8_ref[...]
        for s in range(8):
            sp = sp + ws8_ref[s] * v[s:s + 1, :]
        oa_ref[pl.ds(r, 8), :] = (u * sp).astype(bf16)

        q, kb, logf = _b_pre(col(CB_BQ), col(CB_BF), lb_ref[...])
        bi = col(CB_BI)
        bg = col(CB_BG)
        for hd in range(B_HEADS):
            hs = slice(hd * LANE, (hd + 1) * LANE)
            qh, kh, vh = q[:, hs], kb[:, hs], bi[:, hs]
            g = cumsum_seg(logf[:, hs])
            s_a = hg_ref[2 * p, hd]
            s_b = hg_ref[2 * p + 1, hd]
            o = jnp.dot(both(qh * jnp.exp(g)).astype(bf16), jnp.concatenate([s_a, s_b], axis=0).astype(bf16),
                        preferred_element_type=f32)
            for s in range(8):
                m = seg[s]
                pp = jnp.where(m, qh * kh[s:s + 1, :] * jnp.exp(g - g[s:s + 1, :]), 0.0)
                o = o + jnp.sum(pp, axis=-1, keepdims=True) * vh[s:s + 1, :]
            k_dec = kh * jnp.exp(last_rows(g) - g)
            upd = lax.dot_general(k_dec.astype(bf16), both(vh).astype(bf16), (((0,), (0,)), ((), ())),
                                  preferred_element_type=f32)
            hg_out[2 * p, hd] = jnp.exp(_row_to_col(g[DEC_SEQ - 1:DEC_SEQ, :], eye)) * s_a + upd[:, 0:LANE]
            hg_out[2 * p + 1, hd] = jnp.exp(_row_to_col(g[7:8, :], eye)) * s_b + upd[:, LANE:]
            ob_ref[pl.ds(r, 8), hs] = _b_post(o, bg[:, hs], bnw_ref[...]).astype(bf16)

        x = h_ref[pl.ds(r, 8), CB_CX0 * BW:CB_CX0 * BW + C_CONV_DIM]
        bufs = [jnp.where(first, cv_ref[2 * p, k:k + 1, :], cv_ref[2 * p + 1, k:k + 1, :]) for k in range(3)]
        r1 = pltpu.roll(x, 1, 0)
        r2 = pltpu.roll(x, 2, 0)
        r3 = pltpu.roll(x, 3, 0)
        sh1 = jnp.where(tpos >= 1, r1, bufs[2])
        sh2 = jnp.where(tpos >= 2, r2, jnp.where(tpos == 0, bufs[1], bufs[2]))
        sh3 = jnp.where(tpos >= 3, r3, jnp.where(tpos == 0, bufs[0], jnp.where(tpos == 1, bufs[1], bufs[2])))
        xbc = _c_conv_silu(sh3, sh2, sh1, x, cw_ref, cb_ref)
        cv_out[2 * p] = pltpu.roll(x, 7, 0)[0:3, :]
        cv_out[2 * p + 1] = r3[0:3, :]
        xs = xbc[:, 0:BW]
        bm = xbc[:, BW:BW + 2 * C_DSTATE]
        cm = xbc[:, BW + 2 * C_DSTATE:]
        cpar = cpar_ref[...]
        dt = jax.nn.softplus(small + cpar[0:1, :])
        gl_all = dt * (-jnp.exp(cpar[1:2, :]))
        g = cumsum_seg(gl_all)
        g_lastr = last_rows(g)
        dots = []
        for grp in range(C_GROUPS):
            cg = cm[:, grp * C_DSTATE:(grp + 1) * C_DSTATE]
            bgp = bm[:, grp * C_DSTATE:(grp + 1) * C_DSTATE]
            dots.append([jnp.sum(cg * bgp[s:s + 1, :], axis=-1, keepdims=True) for s in range(8)])
        ys = []
        for pr in range(C_HEADS // 2):
            xp = xs[:, pr * LANE:(pr + 1) * LANE]
            grp = (2 * pr) // (C_HEADS // C_GROUPS)
            cg = cm[:, grp * C_DSTATE:(grp + 1) * C_DSTATE]
            bgp = bm[:, grp * C_DSTATE:(grp + 1) * C_DSTATE]
            spairs = [jnp.concatenate([ssm_ref[2 * p + w, 2 * pr], ssm_ref[2 * p + w, 2 * pr + 1]], axis=-1)
                      for w in range(2)]
            y = jnp.zeros((8, LANE), f32)
            ces, kds, xms = [], [], []
            for sub in range(2):
                hd = 2 * pr + sub
                lm = lo if sub == 0 else jnp.logical_not(lo)
                gcol = g[:, hd:hd + 1]
                xm = jnp.where(lm, xp, 0.0)
                for s in range(8):
                    coef = dots[grp][s] * jnp.where(seg[s], jnp.exp(gcol - gcol[s:s + 1, :]) * dt[s:s + 1, hd:hd + 1], 0.0)
                    y = y + coef * xm[s:s + 1, :]
                ces.append(cg * jnp.exp(gcol))
                kds.append(bgp * (dt[:, hd:hd + 1] * jnp.exp(g_lastr[:, hd:hd + 1] - gcol)))
                xms.append(xm)
            lhs = jnp.concatenate([jnp.where(first, ces[0], 0.0), jnp.where(first, ces[1], 0.0),
                                   jnp.where(first, 0.0, ces[0]), jnp.where(first, 0.0, ces[1])], axis=-1)
            rhs = jnp.concatenate([jnp.where(lo, spairs[0], 0.0), jnp.where(lo, 0.0, spairs[0]),
                                   jnp.where(lo, spairs[1], 0.0), jnp.where(lo, 0.0, spairs[1])], axis=0)
            y = y + jnp.dot(lhs.astype(bf16), rhs.astype(bf16), preferred_element_type=f32)
            kd16 = jnp.concatenate(kds, axis=0)
            xm16 = jnp.concatenate([both(xms[0]), both(xms[1])], axis=0)
            upd = lax.dot_general(kd16.astype(bf16), xm16.astype(bf16), (((0,), (0,)), ((), ())),
                                  preferred_element_type=f32)
            for w in range(2):
                last = DEC_SEQ - 1 if w == 0 else 7
                dl = jnp.where(lo, jnp.exp(g[last:last + 1, 2 * pr:2 * pr + 1]),
                               jnp.exp(g[last:last + 1, 2 * pr + 1:2 * pr + 2]))
                snew = dl * spairs[w] + upd[:, w * LANE:(w + 1) * LANE]
                ssm_out[2 * p + w, 2 * pr] = snew[:, 0:C_HEADDIM]
                ssm_out[2 * p + w, 2 * pr + 1] = snew[:, C_HEADDIM:]
            cdl = jnp.where(lo, cpar[2:3, 2 * pr:2 * pr + 1], cpar[2:3, 2 * pr + 1:2 * pr + 2])
            ys.append(y + cdl * xp)
        yc = jnp.concatenate(ys, axis=-1)
        oc_ref[pl.ds(r, 8), :] = _c_post(yc, col(CB_CZ), cnw_ref[...]).astype(bf16)

        smd = small + dpar_ref[0:1, :]
        lsd = jax.nn.log_sigmoid(smd)
        dq = col(CB_DQ)
        dk = col(CB_DK)
        dv = col(CB_DV)
        do = col(CB_DO)
        dnw = dnw_ref[...]
        bcs = cumsum_seg(lsd)
        mrows = [mm_ref[pl.ds(2 * p + w, 1), :] for w in range(2)]
        mnew_rows = [jnp.zeros((1, D_HEADS), f32), jnp.zeros((1, D_HEADS), f32)]
        hlane = lax.broadcasted_iota(i32, (1, D_HEADS), 1)
        for hd in range(D_HEADS):
            hs = slice(hd * LANE, (hd + 1) * LANE)
            li, lf = 8 + hd, 12 + hd
            bcol = bcs[:, lf:lf + 1]
            icol = smd[:, li:li + 1]
            mprev = jnp.where(first, mrows[0][:, hd:hd + 1], mrows[1][:, hd:hd + 1])
            dcols = [jnp.where(seg[s], bcol - bcol[s:s + 1, :] + icol[s:s + 1, :], NEG_BIG) for s in range(8)]
            mx = dcols[0]
            for s in range(1, 8):
                mx = jnp.maximum(mx, dcols[s])
            inter = bcol + mprev
            mt = jnp.maximum(inter, mx)
            w_inter = jnp.exp(inter - mt)
            qh = dq[:, hs] * (D_DK ** -0.5)
            kh = dk[:, hs]
            vh = dv[:, hs]
            cmats = [mc_ref[2 * p + w, hd] for w in range(2)]
            nrows = [mn_ref[2 * p + w, pl.ds(hd, 1), :] for w in range(2)]
            num = w_inter * jnp.dot(both(qh).astype(bf16), jnp.concatenate(cmats, axis=0).astype(bf16),
                                    preferred_element_type=f32)
            den = w_inter * jnp.sum(qh * jnp.where(first, nrows[0], nrows[1]), axis=-1, keepdims=True)
            for s in range(8):
                w = jnp.where(seg[s], jnp.exp(dcols[s] - mt), 0.0)
                qk = jnp.sum(qh * kh[s:s + 1, :], axis=-1, keepdims=True) * w
                num = num + qk * vh[s:s + 1, :]
                den = den + qk
            hh = num / jnp.maximum(jnp.abs(den), jnp.exp(-mt))
            wk = jnp.exp(last_rows(bcol) - bcol + icol - last_rows(mt))
            wkk = wk * kh
            upd = lax.dot_general(wkk.astype(bf16), both(vh).astype(bf16), (((0,), (0,)), ((), ())),
                                  preferred_element_type=f32)
            for w in range(2):
                last = DEC_SEQ - 1 if w == 0 else 7
                mnew = mt[last:last + 1, :]
                decay = jnp.exp(bcol[last:last + 1, :] + mrows[w][:, hd:hd + 1] - mnew)
                mine = first if w == 0 else jnp.logical_not(first)
                mc_out[2 * p + w, hd] = decay * cmats[w] + upd[:, w * LANE:(w + 1) * LANE]
                mn_out[2 * p + w, pl.ds(hd, 1), :] = decay * nrows[w] + jnp.sum(jnp.where(mine, wkk, 0.0), axis=0,
                                                                                 keepdims=True)
                mnew_rows[w] = jnp.where(hlane == hd, mnew, mnew_rows[w])
            od_ref[pl.ds(r, 8), hs] = (jax.nn.sigmoid(do[:, hs]) * _rms(hh, dnw[:, hs])).astype(bf16)
        for w in range(2):
            mm_out[pl.ds(2 * p + w, 1), :] = mnew_rows[w]
        return carry

    lax.fori_loop(0, SB // 2, tile, 0)


_N_SAMPLE_IN = 19


def _sample_body_aliased(n_alias, *refs):
    _sample_body(*refs[:_N_SAMPLE_IN], *refs[_N_SAMPLE_IN + n_alias:])


def _sample_mixer(h, st_hgrn, st_ssm, st_conv, st_c, st_n, st_m, pa, pb, pc, pd, layer, brs, stacked):
    rows = SB * DEC_SEQ
    row0 = N_PROMPT // rows

    def blk(shape):
        nd = len(shape)
        return pl.BlockSpec((SB,) + shape, lambda i, nd=nd: (i,) + (0,) * nd)

    def blk_stacked(shape):
        nd = len(shape)
        return pl.BlockSpec((None, SB) + shape, lambda i, nd=nd: (layer, i) + (0,) * nd)

    def full(shape):
        nd = len(shape)
        return pl.BlockSpec(shape, lambda i, nd=nd: (0,) * nd)

    big = ((B_HEADS, LANE, LANE), (C_HEADS, C_DSTATE, C_HEADDIM), (D_HEADS, LANE, LANE))
    in_state_specs = [blk_stacked(big[0]), blk_stacked(big[1]), blk_stacked((C_CONV - 1, C_CONV_DIM)),
                      blk_stacked(big[2]), blk_stacked((D_HEADS, LANE)), blk_stacked((D_HEADS,))]
    out_state_specs = [blk_stacked(big[0]), blk_stacked(big[1]), blk((C_CONV - 1, C_CONV_DIM)), blk_stacked(big[2]),
                       blk((D_HEADS, LANE)), blk((D_HEADS,))]
    params = list(pa) + list(pb) + list(pc) + list(pd)
    br_spec = pl.BlockSpec((rows, BW), lambda i: (row0 + i, 0))
    br_shape = jax.ShapeDtypeStruct((N_TOK, BW), bf16)
    stacked_shape = lambda a: jax.ShapeDtypeStruct(a.shape, f32)
    per_layer = lambda a: jax.ShapeDtypeStruct(a.shape[1:], f32)
    inputs = [h, st_hgrn, st_ssm, st_conv, st_c, st_n, st_m] + params
    in_specs = [pl.BlockSpec((rows, MIX_W), lambda i: (row0 + i, 0))] + in_state_specs + [full(p.shape) for p in params]
    assert len(inputs) == _N_SAMPLE_IN
    alias_in = list(brs) + (list(stacked) if stacked is not None else [])
    alias_out = [0, 1, 2, 3] + ([5, 6, 8] if stacked is not None else [])
    aliases = {_N_SAMPLE_IN + k: o for k, o in enumerate(alias_out)}
    return pl.pallas_call(
        functools.partial(_sample_body_aliased, len(alias_in)),
        out_shape=(br_shape, br_shape, br_shape, br_shape, jax.ShapeDtypeStruct((N_SAMPLE, BW), f32),
                   stacked_shape(st_hgrn), stacked_shape(st_ssm),
                   per_layer(st_conv), stacked_shape(st_c), per_layer(st_n), per_layer(st_m)),
        grid=(DEC_BATCH // SB,),
        in_specs=in_specs + [pl.BlockSpec(memory_space=pl.ANY)] * len(alias_in),
        out_specs=(br_spec, br_spec, br_spec, br_spec, pl.BlockSpec((rows, BW), lambda i: (i, 0)))
        + tuple(out_state_specs),
        input_output_aliases=aliases,
        compiler_params=_cparams(("parallel",)),
        name="sample_mixer",
    )(*inputs, *alias_in)


def _merge_body(x_ref, nw_ref, ba_ref, bb_ref, bc_ref, bd_ref, wg_ref, wb_ref, o_ref, xn_ref, acc_ref):
    jc = pl.program_id(1)
    n = pl.program_id(2)

    @pl.when((jc == 0) & (n == 0))
    def _():
        xn_ref[...] = _rms(x_ref[...], nw_ref[...]).astype(bf16)

    gate = jax.nn.sigmoid(jnp.dot(xn_ref[...], wg_ref[...], preferred_element_type=f32))
    for k, br_ref in enumerate((ba_ref, bb_ref, bc_ref, bd_ref)):
        @pl.when(n == k)
        def _(br_ref=br_ref, k=k):
            contrib = gate * jnp.dot(br_ref[...], wb_ref[0], preferred_element_type=f32)
            if k == 0:
                acc_ref[...] = contrib
            else:
                acc_ref[...] = acc_ref[...] + contrib

    @pl.when(n == N_BRANCH - 1)
    def _():
        o_ref[...] = acc_ref[...].astype(bf16)


def _merge(x, nw, brs, w_gate, w_branch, tm, tn, layer):
    m = x.shape[0]
    ncol = D_MODEL // tn
    br_spec = pl.BlockSpec((tm, BW), lambda i, jc, n: (i, 0))
    return pl.pallas_call(
        _merge_body,
        out_shape=jax.ShapeDtypeStruct((m, D_MODEL), bf16),
        grid=(m // tm, ncol, N_BRANCH),
        in_specs=[pl.BlockSpec((tm, D_MODEL), lambda i, jc, n: (i, 0)),
                  pl.BlockSpec((1, D_MODEL), lambda i, jc, n: (0, 0)),
                  br_spec, br_spec, br_spec, br_spec,
                  pl.BlockSpec((D_MODEL, tn), lambda i, jc, n: (0, n * ncol + jc)),
                  pl.BlockSpec((None, 1, BW, tn), lambda i, jc, n: (layer, n, 0, jc))],
        out_specs=pl.BlockSpec((tm, tn), lambda i, jc, n: (i, jc)),
        scratch_shapes=[pltpu.VMEM((tm, D_MODEL), bf16), pltpu.VMEM((tm, tn), f32)],
        compiler_params=_cparams(("parallel", "arbitrary", "arbitrary")),
        name="merge",
    )(x, nw, *brs, w_gate, w_branch)


def _top16_rows(s, rid=None):
    if rid is None:
        rid = lax.broadcasted_iota(i32, s.shape, 0).astype(f32)
    out = []
    for _ in range(PEER_TOPK):
        m = jnp.max(s, axis=0, keepdims=True)
        am = jnp.min(jnp.where(s == m, rid, float(PEER_N)), axis=0, keepdims=True)
        out.append((m, am))
        s = jnp.where(rid == am, -jnp.inf, s)
    return out


def _collect16(pairs, tb):
    r16 = lax.broadcasted_iota(i32, (PEER_TOPK, tb), 0)
    v = jnp.zeros((PEER_TOPK, tb), f32)
    ix = jnp.zeros((PEER_TOPK, tb), f32)
    for k, (m, am) in enumerate(pairs):
        v = jnp.where(r16 == k, m, v)
        ix = jnp.where(r16 == k, am, ix)
    return v, ix.astype(i32)


def _peer_route_body(q_ref, keys_ref, ia_ref, ib_ref, gt_ref):
    tb = q_ref.shape[0]
    vals = []
    idxs = []
    for p in range(2):
        st = lax.dot_general(keys_ref[0, p], q_ref[:, p * LANE:(p + 1) * LANE], (((1,), (1,)), ((), ())),
                             preferred_element_type=f32)
        v, ix = _collect16(_top16_rows(st), tb)
        vals.append(v)
        idxs.append(ix)
    va, vb = vals
    half = PEER_TOPK // 2
    cand = jnp.concatenate([va[0:1, :] + vb] + [va[k:k + 1, :] + vb[0:half, :] for k in range(1, half)]
                           + [va[half:, :] + vb[0:1, :]], axis=0)
    r = lax.broadcasted_iota(i32, cand.shape, 0)
    r2 = r - PEER_TOPK
    mid = PEER_TOPK * (1 + (r2 >> 3)) + (r2 & (half - 1))
    pid = jnp.where(r < PEER_TOPK, r, jnp.where(r < PEER_TOPK + half * (half - 1), mid, (r - 8 * half) * PEER_TOPK))
    fs, pos = _collect16(_top16_rows(cand, pid.astype(f32)), tb)
    ka = pos >> 4
    kb = pos & (PEER_TOPK - 1)
    i1 = jnp.zeros((PEER_TOPK, tb), i32)
    i2 = jnp.zeros((PEER_TOPK, tb), i32)
    for j in range(PEER_TOPK):
        i1 = jnp.where(ka == j, idxs[0][j:j + 1, :], i1)
        i2 = jnp.where(kb == j, idxs[1][j:j + 1, :], i2)
    pe = jnp.exp(fs - fs[0:1, :])
    ia_ref[...] = i1
    ib_ref[...] = i2
    gt_ref[...] = pe / jnp.sum(pe, axis=0, keepdims=True)


def _peer_route(q, keys, tb, layer):
    m = q.shape[0]
    spec = pl.BlockSpec((PEER_TOPK, tb), lambda i, hd: (hd, i))
    nslot = PEER_HEADS * PEER_TOPK
    return pl.pallas_call(
        _peer_route_body,
        out_shape=(jax.ShapeDtypeStruct((nslot, m), i32), jax.ShapeDtypeStruct((nslot, m), i32),
                   jax.ShapeDtypeStruct((nslot, m), f32)),
        grid=(m // tb, PEER_HEADS),
        in_specs=[pl.BlockSpec((tb, 2 * LANE), lambda i, hd: (i, hd)),
                  pl.BlockSpec((None, 1, 2, PEER_NKEYS, LANE), lambda i, hd: (layer, hd, 0, 0, 0))],
        out_specs=(spec, spec, spec),
        compiler_params=_cparams(("parallel", "parallel")),
        name="peer_route",
    )(q, keys)


GATE_UNROLL = 128
RELAYOUT_TOKENS = 16


def _peer_gates_body(ia_ref, ib_ref, gt_ref, o_ref, a_s, b_s, g_s, gm_s):
    tg = ia_ref.shape[1]
    a_s[...] = ia_ref[...].astype(f32).T
    b_s[...] = ib_ref[...].astype(f32).T
    g_s[...] = gt_ref[...].T
    sub = lax.broadcasted_iota(i32, (LANE, LANE), 0).astype(f32)

    def step(r, carry):
        r8 = pl.multiple_of(r * GATE_UNROLL, GATE_UNROLL)
        a8 = a_s[pl.ds(r8, GATE_UNROLL), :]
        b8 = b_s[pl.ds(r8, GATE_UNROLL), :]
        g8 = g_s[pl.ds(r8, GATE_UNROLL), :]
        for u in range(GATE_UNROLL):
            at = jnp.where(sub == a8[u:u + 1, :], 1.0, 0.0).astype(bf16)
            bt = jnp.where(sub == b8[u:u + 1, :], g8[u:u + 1, :], 0.0).astype(bf16)
            gm = lax.dot_general(at, bt, (((1,), (1,)), ((), ())), preferred_element_type=f32)
            gm_s[r8 + u] = gm
        return carry

    lax.fori_loop(0, tg // GATE_UNROLL, step, 0)

    def relayout(c, carry):
        t0 = pl.multiple_of(c * RELAYOUT_TOKENS, RELAYOUT_TOKENS)
        for blk in range(PEER_NKEYS // 8):
            x = gm_s[pl.ds(t0, RELAYOUT_TOKENS), pl.ds(blk * 8, 8), :]
            y = jnp.swapaxes(x, 0, 1)
            for i in range(8):
                o_ref[pl.ds(t0, RELAYOUT_TOKENS), pl.ds((blk * 8 + i) * PEER_NKEYS, PEER_NKEYS)] = y[i].astype(bf16)
        return carry

    lax.fori_loop(0, tg // RELAYOUT_TOKENS, relayout, 0)


def _peer_gates(ia, ib, gt, tg):
    nslot, m = ia.shape
    spec = pl.BlockSpec((nslot, tg), lambda i: (0, i))
    return pl.pallas_call(
        _peer_gates_body,
        out_shape=jax.ShapeDtypeStruct((m, PEER_N), bf16),
        grid=(m // tg,),
        in_specs=[spec, spec, spec],
        out_specs=pl.BlockSpec((tg, PEER_N), lambda i: (i, 0)),
        scratch_shapes=[pltpu.VMEM((tg, nslot), f32)] * 3 + [pltpu.VMEM((tg, PEER_NKEYS, PEER_NKEYS), f32)],
        compiler_params=_cparams(("parallel",)),
        name="peer_gates",
    )(ia, ib, gt)


def _peer_experts_body(x_ref, nw_ref, u_ref, v_ref, g_ref, o_ref, xn_ref):
    @pl.when(pl.program_id(1) == 0)
    def _():
        x = x_ref[...]
        xn_ref[...] = _rms(x, nw_ref[...]).astype(bf16)
        o_ref[...] = x

    hmat = lax.dot_general(xn_ref[...], u_ref[...], (((1,), (1,)), ((), ())), preferred_element_type=f32)
    w = (_gelu(hmat) * g_ref[...].astype(f32)).astype(bf16)
    o_ref[...] += jnp.dot(w, v_ref[...], preferred_element_type=f32)


def _peer_experts(x, nw, u, v, g, tb, eb, layer):
    m = x.shape[0]
    return pl.pallas_call(
        _peer_experts_body,
        out_shape=jax.ShapeDtypeStruct((m, D_MODEL), f32),
        grid=(m // tb, PEER_N // eb),
        in_specs=[pl.BlockSpec((tb, D_MODEL), lambda i, j: (i, 0), pipeline_mode=pl.Buffered(1)),
                  pl.BlockSpec((1, D_MODEL), lambda i, j: (0, 0)),
                  pl.BlockSpec((None, eb, D_MODEL), lambda i, j: (layer, j, 0)),
                  pl.BlockSpec((None, eb, D_MODEL), lambda i, j: (layer, j, 0)),
                  pl.BlockSpec((tb, eb), lambda i, j: (i, j))],
        out_specs=pl.BlockSpec((tb, D_MODEL), lambda i, j: (i, 0), pipeline_mode=pl.Buffered(1)),
        scratch_shapes=[pltpu.VMEM((tb, D_MODEL), bf16)],
        compiler_params=_cparams(("parallel", "arbitrary")),
        name="peer_experts",
    )(x, nw, u, v, g)


_IN_OFFS = tuple(int(v) for v in np.cumsum((0,) + IN_SPLITS))
_O_CDT, _O_DQ, _O_DIG, _O_GATES, _O_END = _IN_OFFS[8], _IN_OFFS[9], _IN_OFFS[13], _IN_OFFS[15], _IN_OFFS[16]
PREP_ROWS = 128


def _prep_w_in_body(w_ref, om_ref, og_ref):
    rows = w_ref.shape[0]
    om_ref[:, 0:_O_CDT] = w_ref[:, 0:_O_CDT].astype(bf16)
    om_ref[:, _O_CDT:SMALL_COL] = w_ref[:, _O_DQ:_O_DIG].astype(bf16)
    small = jnp.concatenate([w_ref[:, _O_CDT:_O_DQ], w_ref[:, _O_DIG:_O_GATES],
                             jnp.zeros((rows, LANE - 16), w_ref.dtype)], axis=-1)
    om_ref[:, SMALL_COL:SMALL_COL + LANE] = small.astype(bf16)
    om_ref[:, SMALL_COL + LANE:MIX_W] = jnp.zeros((rows, MIX_W - SMALL_COL - LANE), bf16)
    og_ref[...] = w_ref[:, _O_GATES:_O_END].astype(bf16)


def _prep_w_in(w, layer):
    _, k, n = w.shape
    ng = _O_END - _O_GATES
    return pl.pallas_call(
        _prep_w_in_body,
        out_shape=(jax.ShapeDtypeStruct((k, MIX_W), bf16), jax.ShapeDtypeStruct((k, ng), bf16)),
        grid=(k // PREP_ROWS,),
        in_specs=[pl.BlockSpec((None, PREP_ROWS, n), lambda i: (layer, i, 0))],
        out_specs=(pl.BlockSpec((PREP_ROWS, MIX_W), lambda i: (i, 0)), pl.BlockSpec((PREP_ROWS, ng), lambda i: (i, 0))),
        compiler_params=_cparams(("parallel",)),
        name="prep_w_in",
    )(w)


def _lane_row(vals, start):
    row = jnp.zeros((LANE,), f32)
    return row.at[start:start + vals.shape[0]].set(vals)


def kernel(x_prompt, x_sample, state_hgrn, state_ssm, state_conv, state_mlstm_c, state_mlstm_n, state_mlstm_m, norm1_w, w_in, a_ln_w, a_ln_b, a_ws, a_bs, b_lb_logits, b_norm_w, c_conv_w, c_conv_b, c_dt_bias, c_a_log, c_d, c_norm_w, d_ig_b, d_fg_b, d_norm_w, w_branch, w_out, norm2_w, peer_wq, peer_keys, peer_u, peer_v, final_norm_w):
    x = jnp.concatenate([x_prompt.reshape(N_PROMPT, D_MODEL), x_sample.reshape(N_SAMPLE, D_MODEL)], axis=0)
    lbs = jax.nn.softmax(b_lb_logits.astype(f32), axis=0)
    lbs = jnp.cumsum(lbs, axis=0) - lbs[0]
    zeros8 = jnp.zeros((8, LANE), f32)
    news_p = []
    news_s = []
    stacked_s = None
    w_in_b, w_branch_b, w_out_b, wq_b = (w.astype(bf16) for w in (w_in, w_branch, w_out, peer_wq))
    keys_b, u_b, v_b = (w.astype(bf16) for w in (peer_keys, peer_u, peer_v))
    for l in range(DEPTH):
        w_mix, w_gate = _prep_w_in(w_in_b, l)
        h = _norm_mm(x, norm1_w[l][None, :], w_mix, f32, 1088, 1024)

        lnw = a_ln_w[l][None, :]
        lnb = a_ln_b[l][None, :]
        lb = lbs[l][None, :]
        bnw = b_norm_w[l][None, :]
        cw = c_conv_w[l]
        cb = c_conv_b[l][None, :]
        cpar = zeros8.at[0].set(_lane_row(c_dt_bias[l], 0)).at[1].set(_lane_row(c_a_log[l], 0)).at[2].set(_lane_row(c_d[l], 0))
        cnw = c_norm_w[l][None, :]
        dpar = zeros8.at[0].set(_lane_row(d_ig_b[l], 8) + _lane_row(d_fg_b[l], 12))
        dnw = d_norm_w[l][None, :]

        bs_full = jnp.repeat(a_bs[l].T, LANE, axis=1)
        br_a = _prompt_a(h, lnw, lnb, a_ws[l], bs_full)
        br_b, hg_p = _prompt_b(h, lb, bnw)
        br_c, ssm_p, cv_p = _prompt_c(h, cw, cb, cpar, cnw)
        br_d, mc_p, mn_p, mm_p = _prompt_d(h, dpar, dnw)
        news_p.append((hg_p, ssm_p, cv_p, mc_p, mn_p[:, :D_HEADS, :], mm_p[:, :D_HEADS, 0]))

        w4 = jnp.tril(a_ws[l][:, :DEC_SEQ, :DEC_SEQ])
        w8 = jnp.zeros((A_GROUPS, 8, 8), f32).at[:, :4, :4].set(w4).at[:, 4:, 4:].set(w4)
        ws8 = jnp.repeat(jnp.transpose(w8, (2, 1, 0)), LANE, axis=2)
        bs8 = jnp.repeat(jnp.tile(a_bs[l][:, :DEC_SEQ], (1, 2)).T, LANE, axis=1)
        outs = _sample_mixer(h, state_hgrn, state_ssm, state_conv, state_mlstm_c, state_mlstm_n, state_mlstm_m,
                             (lnw, lnb, ws8, bs8), (lb, bnw), (cw, cb, cpar, cnw), (dpar, dnw), l,
                             (br_a, br_b, br_c, br_d), stacked_s)
        brs = outs[:4]
        stacked_s = (outs[5], outs[6], outs[8])
        news_s.append((outs[7], outs[9], outs[10], outs[4].reshape(DEC_BATCH, DEC_SEQ, BW)))

        mixin = _merge(x, norm1_w[l][None, :], brs, w_gate, w_branch_b, 1088, 1024, l)
        x = _mm_res(mixin, w_out_b, x, 1088, 512, l)

        q = _norm_mm(x, norm2_w[l][None, :], wq_b, bf16, 1088, 512, l)
        ia, ib, gt = _peer_route(q, keys_b, 512, l)
        g = _peer_gates(ia, ib, gt, 128)
        x = _peer_experts(x, norm2_w[l][None, :], u_b, v_b, g, 1088, 1024, l)

    y_p, y_s = _final_norm(x, final_norm_w[None, :])
    y_prompt = y_p.reshape(BATCH, SEQ, D_MODEL)
    y_sample = y_s.reshape(DEC_BATCH, DEC_SEQ, D_MODEL)
    stack = lambda news, k: jnp.stack([n[k] for n in news], axis=0)
    hgrn_s, ssm_s, mc_s = stacked_s
    conv_s, mn_s, mm_s, chunk_v_s = (stack(news_s, k) for k in range(4))
    return ((y_prompt, y_sample) + tuple(stack(news_p, k) for k in range(6))
            + (hgrn_s, ssm_s, conv_s, mc_s, mn_s, mm_s, chunk_v_s))
```
